```python
import math
import jax, jax.numpy as jnp
from jax import lax
import numpy as np

D_MODEL = 2048
BATCH = 1
SEQ = 8192
DEPTH = 1

GRID_W = 64
CTX_LEN = 256
D_MIX = D_MODEL
DIFF_HEADS = 8
DIFF_HEAD_DIM = 64
DIFF_V_DIM = 2 * DIFF_HEAD_DIM
ATT_W = DIFF_HEADS * DIFF_V_DIM
CONV_W = D_MIX - ATT_W
CONV_K = 31
Q_BLOCK = 128
ROPE_BASE = 10000.0
EPS = 1e-6

Q0 = 0
K0 = Q0 + DIFF_HEADS * 2 * DIFF_HEAD_DIM
V0 = K0 + DIFF_HEADS * 2 * DIFF_HEAD_DIM
GA0 = V0 + ATT_W
U0 = GA0 + ATT_W
GC0 = U0 + 2 * CONV_W
IN_W = GC0 + CONV_W

kernel_name = "hybrid_diffattn_conformer_dit_layer"


def rms(x, g):
    xf = x.astype(jnp.float32)
    y = xf * lax.rsqrt(jnp.mean(xf * xf, axis=-1, keepdims=True) + EPS)
    return (y * g).astype(x.dtype)


def layernorm(x, g, b):
    xf = x.astype(jnp.float32)
    mu = jnp.mean(xf, axis=-1, keepdims=True)
    var = jnp.mean(jnp.square(xf - mu), axis=-1, keepdims=True)
    return ((xf - mu) * lax.rsqrt(var + EPS) * g + b).astype(x.dtype)


def adaln(cvec, w, b):
    m = jax.nn.silu(cvec) @ w + b
    return jnp.split(m, 3, axis=-1)


def modulate(x, g, shift, scale):
    return rms(x, g) * (1.0 + scale[:, None]) + shift[:, None]


def rope_tables(n_tokens):
    rows = n_tokens // GRID_W
    row = jnp.repeat(jnp.arange(rows), GRID_W).astype(jnp.float32)
    col = jnp.tile(jnp.arange(GRID_W), rows).astype(jnp.float32)
    nf = DIFF_HEAD_DIM // 4
    inv = ROPE_BASE ** (-jnp.arange(nf, dtype=jnp.float32) / nf)
    ang_r = row[:, None] * inv
    ang_c = col[:, None] * inv
    ang = jnp.concatenate([ang_r, ang_r, ang_c, ang_c], axis=-1)
    return jnp.cos(ang), jnp.sin(ang)


def rotate_half_2d(t):
    nf = DIFF_HEAD_DIM // 4
    xs = t.reshape(t.shape[:-1] + (2, 2, nf))
    x1 = xs[..., 0, :]
    x2 = xs[..., 1, :]
    return jnp.stack([-x2, x1], axis=-2).reshape(t.shape)


def apply_rope(t, cos, sin):
    c = cos[None, :, None, None, :]
    s = sin[None, :, None, None, :]
    return (t * c + rotate_half_2d(t) * s).astype(t.dtype)


def diff_attend(q, k_all, v_all, lam):
    B, N, H, _, d = q.shape
    nblk = N // Q_BLOCK
    qb = jnp.moveaxis(q.reshape(B, nblk, Q_BLOCK, H, 2, d), 1, 0)
    scale = 1.0 / math.sqrt(d)

    def one_block(qi):
        s = jnp.einsum('bqhmd,bkhmd->bhmqk', qi, k_all).astype(jnp.float32) * scale
        p = jax.nn.softmax(s, axis=-1)
        a = p[:, :, 0] - lam * p[:, :, 1]
        return jnp.einsum('bhqk,bkhe->bqhe', a.astype(v_all.dtype), v_all)

    o = lax.map(one_block, qb)
    return jnp.moveaxis(o, 0, 1).reshape(B, N, H, DIFF_V_DIM)


def diff_attn_branch(q, k_all, v_all, lam, lam_init, sub_g):
    B, N = q.shape[0], q.shape[1]
    o = diff_attend(q, k_all, v_all, lam)
    o = rms(o, sub_g) * (1.0 - lam_init)
    return o.reshape(B, N, ATT_W)


def depthwise_conv(y, w, b):
    out = lax.conv_general_dilated(
        y, w[:, None, :].astype(y.dtype), window_strides=(1,),
        padding=[(CONV_K // 2, CONV_K // 2)],
        dimension_numbers=('NWC', 'WIO', 'NWC'),
        feature_group_count=y.shape[-1])
    return out + b


def conformer_conv_branch(u, w, b, ln_g, ln_b):
    a, g = jnp.split(u, 2, axis=-1)
    y = a * jax.nn.sigmoid(g)
    y = depthwise_conv(y, w, b)
    y = layernorm(y, ln_g, ln_b)
    return jax.nn.silu(y)


def setup_inputs(seed: int = 0) -> dict:
    key = jax.random.key(seed)
    ks = jax.random.split(key, 24)
    f32 = jnp.float32
    nrm = lambda k, s, sc: (jax.random.normal(k, s, f32) * sc)
    d = DIFF_HEAD_DIM
    return {
        "x": nrm(ks[0], (BATCH, SEQ, D_MODEL), 1.0),
        "c": nrm(ks[1], (BATCH, D_MODEL), 1.0),
        "ctx": nrm(ks[2], (BATCH, CTX_LEN, D_MODEL), 1.0),
        "c_ctx": nrm(ks[3], (D_MODEL,), 1.0),
        "w_ada": nrm(ks[4], (DEPTH, D_MODEL, 3 * D_MODEL), 0.5 * D_MODEL ** -0.5),
        "b_ada": nrm(ks[5], (DEPTH, 3 * D_MODEL), 0.02),
        "norm_g": 1.0 + nrm(ks[6], (DEPTH, D_MODEL), 0.02),
        "w_in": nrm(ks[7], (DEPTH, D_MODEL, IN_W), D_MODEL ** -0.5),
        "q_norm_g": 1.0 + nrm(ks[8], (DEPTH, d), 0.02),
        "k_norm_g": 1.0 + nrm(ks[9], (DEPTH, d), 0.02),
        "lam_q1": nrm(ks[10], (DEPTH, d), 0.1),
        "lam_k1": nrm(ks[11], (DEPTH, d), 0.1),
        "lam_q2": nrm(ks[12], (DEPTH, d), 0.1),
        "lam_k2": nrm(ks[13], (DEPTH, d), 0.1),
        "sub_norm_g": 1.0 + nrm(ks[14], (DEPTH, DIFF_V_DIM), 0.02),
        "conv_w": nrm(ks[15], (DEPTH, CONV_K, CONV_W), CONV_K ** -0.5),
        "conv_b": nrm(ks[16], (DEPTH, CONV_W), 0.02),
        "conv_ln_g": 1.0 + nrm(ks[17], (DEPTH, CONV_W), 0.02),
        "conv_ln_b": nrm(ks[18], (DEPTH, CONV_W), 0.02),
        "w_out": nrm(ks[19], (DEPTH, D_MIX, D_MODEL), D_MIX ** -0.5),
    }


def reference(x, c, ctx, c_ctx, w_ada, b_ada, norm_g, w_in, q_norm_g, k_norm_g,
              lam_q1, lam_k1, lam_q2, lam_k2, sub_norm_g, conv_w, conv_b,
              conv_ln_g, conv_ln_b, w_out):
    B, N, _ = x.shape
    L_ctx = ctx.shape[1]
    H, d = DIFF_HEADS, DIFF_HEAD_DIM
    cos, sin = rope_tables(N)
    for i in range(DEPTH):
        last = i == DEPTH - 1
        lam_init = 0.8 - 0.6 * math.exp(-0.3 * i)
        lam = (jnp.exp(jnp.sum(lam_q1[i] * lam_k1[i]).astype(jnp.float32))
               - jnp.exp(jnp.sum(lam_q2[i] * lam_k2[i]).astype(jnp.float32)) + lam_init)

        shift_x, scale_x, gate_x = adaln(c, w_ada[i], b_ada[i])
        shift_c, scale_c, gate_c = adaln(c_ctx[None], w_ada[i], b_ada[i])
        h_x = modulate(x, norm_g[i], shift_x, scale_x)
        h_c = modulate(ctx, norm_g[i], shift_c, scale_c)

        p_x = h_x @ w_in[i]
        if last:
            p_c_kv = h_c @ w_in[i][:, K0:GA0]
        else:
            p_c = h_c @ w_in[i]
            p_c_kv = p_c[..., K0:GA0]

        k_c = rms(p_c_kv[..., :V0 - K0].reshape(B, L_ctx, H, 2, d), k_norm_g[i])
        v_c = p_c_kv[..., V0 - K0:].reshape(B, L_ctx, H, DIFF_V_DIM)

        q_x = apply_rope(rms(p_x[..., Q0:K0].reshape(B, N, H, 2, d), q_norm_g[i]), cos, sin)
        k_x = apply_rope(rms(p_x[..., K0:V0].reshape(B, N, H, 2, d), k_norm_g[i]), cos, sin)
        v_x = p_x[..., V0:GA0].reshape(B, N, H, DIFF_V_DIM)
        k_all = jnp.concatenate([k_c, k_x], axis=1)
        v_all = jnp.concatenate([v_c, v_x], axis=1)

        att_x = diff_attn_branch(q_x, k_all, v_all, lam, lam_init, sub_norm_g[i])
        conv_x = conformer_conv_branch(p_x[..., U0:GC0], conv_w[i], conv_b[i],
                                       conv_ln_g[i], conv_ln_b[i])
        mixed_x = jnp.concatenate([att_x * jax.nn.silu(p_x[..., GA0:U0]),
                                   conv_x * jax.nn.silu(p_x[..., GC0:IN_W])], axis=-1)
        y_x = mixed_x @ w_out[i]

        if not last:
            q_c = rms(p_c[..., Q0:K0].reshape(B, L_ctx, H, 2, d), q_norm_g[i])
            att_c = diff_attn_branch(q_c, k_c, v_c, lam, lam_init, sub_norm_g[i])
            conv_c = conformer_conv_branch(p_c[..., U0:GC0], conv_w[i], conv_b[i],
                                           conv_ln_g[i], conv_ln_b[i])
            mixed_c = jnp.concatenate([att_c * jax.nn.silu(p_c[..., GA0:U0]),
                                       conv_c * jax.nn.silu(p_c[..., GC0:IN_W])], axis=-1)
            ctx = ctx + gate_c[:, None] * (mixed_c @ w_out[i])

        x = x + gate_x[:, None] * y_x
    return x
```

```python
import functools
import math

import jax
import jax.numpy as jnp
from jax import lax
from jax.experimental import pallas as pl
from jax.experimental.pallas import tpu as pltpu

D_MODEL = 2048
SEQ = 8192
GRID_W = 64
CTX_LEN = 256
L_ALL = SEQ + CTX_LEN
HEADS = 8
HEAD_DIM = 64
V_DIM = 2 * HEAD_DIM
ATT_W = HEADS * V_DIM
CONV_W = D_MODEL - ATT_W
CONV_K = 31
CONV_HALO = 16
ROPE_BASE = 10000.0
EPS = 1e-6
LAM_INIT = 0.8 - 0.6 * math.exp(-0.3 * 0)

Q0 = 0
K0 = Q0 + HEADS * 2 * HEAD_DIM
V0 = K0 + HEADS * 2 * HEAD_DIM
GA0 = V0 + ATT_W
U0 = GA0 + ATT_W
GC0 = U0 + 2 * CONV_W
IN_W = GC0 + CONV_W

MOD_ROWS = 8
ADA_TN = 512
MOD_TM = 256
PROJ_TM = 512
PROJ_TM_ALL = 768
PROJ_TN = 1024
GLU_TN = 512
ATT_TQ = 256
ATT_KB = 768
OUT_TM = 256
NORM_GROUP = 256
VMEM_LIMIT = 48 * 1024 * 1024

_F32 = jnp.float32
_BF16 = jnp.bfloat16


def _sigmoid(x):
    return 1.0 / (1.0 + jnp.exp(-x))


def _silu(x):
    return x * _sigmoid(x)


def _adaln_kernel(cc_ref, w_ref, b_ref, o_ref):
    s = _silu(cc_ref[...])
    o_ref[...] = jnp.dot(s, w_ref[...], precision=lax.Precision.HIGHEST,
                         preferred_element_type=_F32) + b_ref[...]


def _adaln(cc, w, b):
    n = w.shape[1]
    return pl.pallas_call(
        _adaln_kernel,
        grid=(n // ADA_TN,),
        in_specs=[pl.BlockSpec((MOD_ROWS, D_MODEL), lambda j: (0, 0)),
                  pl.BlockSpec((D_MODEL, ADA_TN), lambda j: (0, j)),
                  pl.BlockSpec((1, ADA_TN), lambda j: (0, j))],
        out_specs=pl.BlockSpec((MOD_ROWS, ADA_TN), lambda j: (0, j)),
        out_shape=jax.ShapeDtypeStruct((MOD_ROWS, n), _F32),
        compiler_params=pltpu.CompilerParams(vmem_limit_bytes=VMEM_LIMIT),
        name="adaln",
    )(cc, w, b)


def _modulate_rows(xin, g, shift, scale):
    r = lax.rsqrt(jnp.mean(xin * xin, axis=-1, keepdims=True) + EPS)
    return ((xin * r) * g) * (1.0 + scale) + shift


def _modulate_kernel(x_ref, ctx_ref, mod_ref, g_ref, h_ref):
    i = pl.program_id(0)
    n_lat = pl.num_programs(0) - 1
    g = g_ref[...]

    @pl.when(i < n_lat)
    def _():
        h_ref[...] = _modulate_rows(x_ref[...], g, mod_ref[0:1, 0:D_MODEL],
                                    mod_ref[0:1, D_MODEL:2 * D_MODEL]).astype(h_ref.dtype)

    @pl.when(i == n_lat)
    def _():
        h_ref[...] = _modulate_rows(ctx_ref[...], g, mod_ref[1:2, 0:D_MODEL],
                                    mod_ref[1:2, D_MODEL:2 * D_MODEL]).astype(h_ref.dtype)


def _modulate(x2, ctx2, mod, g):
    n_lat = SEQ // MOD_TM
    return pl.pallas_call(
        _modulate_kernel,
        grid=(n_lat + CTX_LEN // MOD_TM,),
        in_specs=[pl.BlockSpec((MOD_TM, D_MODEL), lambda i: (jnp.minimum(i, n_lat - 1), 0)),
                  pl.BlockSpec((CTX_LEN, D_MODEL), lambda i: (0, 0)),
                  pl.BlockSpec((MOD_ROWS, 3 * D_MODEL), lambda i: (0, 0)),
                  pl.BlockSpec((1, D_MODEL), lambda i: (0, 0))],
        out_specs=pl.BlockSpec((MOD_TM, D_MODEL), lambda i: (i, 0)),
        out_shape=jax.ShapeDtypeStruct((L_ALL, D_MODEL), _BF16),
        compiler_params=pltpu.CompilerParams(vmem_limit_bytes=VMEM_LIMIT),
        name="modulate",
    )(x2, ctx2, mod, g)


def _proj_call(kernel, h, w_list, extras, *, rows, tm, tn, n_col_tiles, out_dtype, name):
    in_specs = [pl.BlockSpec((tm, D_MODEL), lambda j, i: (i, 0))]
    args = [h]
    for w, c0 in w_list:
        in_specs.append(pl.BlockSpec((D_MODEL, tn), lambda j, i, c0=c0: (0, c0 // tn + j)))
        args.append(w)
    for a, bs, im in extras:
        in_specs.append(pl.BlockSpec(bs, im))
        args.append(a)
    return pl.pallas_call(
        kernel,
        grid=(n_col_tiles, rows // tm),
        in_specs=in_specs,
        out_specs=pl.BlockSpec((tm, tn), lambda j, i: (i, j)),
        out_shape=jax.ShapeDtypeStruct((rows, n_col_tiles * tn), out_dtype),
        compiler_params=pltpu.CompilerParams(vmem_limit_bytes=VMEM_LIMIT),
        name=name,
    )(*args)


def _plain_kernel(h_ref, w_ref, o_ref):
    o_ref[...] = jnp.dot(h_ref[...], w_ref[...], preferred_element_type=_F32).astype(o_ref.dtype)


def _silu_kernel(h_ref, w_ref, o_ref):
    p = jnp.dot(h_ref[...], w_ref[...], preferred_element_type=_F32)
    o_ref[...] = _silu(p).astype(o_ref.dtype)


def _glu_kernel(h_ref, wa_ref, wg_ref, o_ref):
    h = h_ref[...]
    a = jnp.dot(h, wa_ref[...], preferred_element_type=_F32)
    g = jnp.dot(h, wg_ref[...], preferred_element_type=_F32)
    o_ref[...] = (a * _sigmoid(g)).astype(o_ref.dtype)


def _qk_kernel(h_ref, w_ref, g_ref, avg_ref, cos_ref, sa_ref, sb_ref, o_ref):
    p = jnp.dot(h_ref[...], w_ref[...], preferred_element_type=_F32)
    tn = p.shape[1]
    ss = p * p
    hi = ss.astype(_BF16)
    lo = (ss - hi.astype(_F32)).astype(_BF16)
    avg = avg_ref[...]
    ms = []
    for c in range(tn // NORM_GROUP):
        sl = slice(c * NORM_GROUP, (c + 1) * NORM_GROUP)
        ms.append(jnp.dot(hi[:, sl], avg, preferred_element_type=_F32)
                  + jnp.dot(lo[:, sl], avg, preferred_element_type=_F32))
    ms = jnp.concatenate(ms, axis=1)
    y = (p * lax.rsqrt(ms + EPS)) * g_ref[...]
    reps = tn // cos_ref.shape[1]
    cos = jnp.concatenate([cos_ref[...]] * reps, axis=1)
    sa = jnp.concatenate([sa_ref[...]] * reps, axis=1)
    sb = jnp.concatenate([sb_ref[...]] * reps, axis=1)
    quarter = HEAD_DIM // 4
    out = y * cos + pltpu.roll(y, tn - quarter, 1) * sa + pltpu.roll(y, quarter, 1) * sb
    o_ref[...] = out.astype(o_ref.dtype)


def _attn_kernel(lamv_ref, q_ref, k_ref, v_ref, ga_ref, subg_ref, o_ref):
    tq = q_ref.shape[0]
    q = q_ref[...]
    lane = lax.broadcasted_iota(jnp.int32, q.shape, 1)
    zero = jnp.zeros_like(q)
    qs = jnp.concatenate([jnp.where(lane < HEAD_DIM, q, zero),
                          jnp.where(lane >= HEAD_DIM, q, zero)], axis=0)

    def body(kb, carry):
        m, l, acc = carry
        start = pl.multiple_of(kb * ATT_KB, ATT_KB)
        kblk = k_ref[pl.ds(start, ATT_KB), :]
        vblk = v_ref[pl.ds(start, ATT_KB), :]
        s = lax.dot_general(qs, kblk, (((1,), (1,)), ((), ())), preferred_element_type=_F32)
        m_new = jnp.maximum(m, jnp.max(s, axis=1, keepdims=True))
        alpha = jnp.exp(m - m_new)
        p = jnp.exp(s - m_new)
        l = alpha * l + jnp.sum(p, axis=1, keepdims=True)
        acc = alpha * acc + jnp.dot(p.astype(_BF16), vblk, preferred_element_type=_F32)
        return m_new, l, acc

    m0 = jnp.full((2 * tq, 1), -1e30, _F32)
    l0 = jnp.zeros((2 * tq, 1), _F32)
    a0 = jnp.zeros((2 * tq, V_DIM), _F32)
    _, l, acc = lax.fori_loop(0, L_ALL // ATT_KB, body, (m0, l0, a0))
    o = acc / l
    lv = lamv_ref[...]
    lam = (jnp.exp(jnp.sum(lv[0:1] * lv[1:2], axis=1, keepdims=True))
           - jnp.exp(jnp.sum(lv[2:3] * lv[3:4], axis=1, keepdims=True)) + LAM_INIT)
    od = o[:tq] - lam * o[tq:]
    r = lax.rsqrt(jnp.mean(od * od, axis=-1, keepdims=True) + EPS)
    on = ((od * r) * subg_ref[...]) * (1.0 - LAM_INIT)
    o_ref[...] = (on * ga_ref[...].astype(_F32)).astype(o_ref.dtype)


def _attention(lamv, q, k, v, ga, subg):
    return pl.pallas_call(
        _attn_kernel,
        grid=(HEADS, SEQ // ATT_TQ),
        in_specs=[pl.BlockSpec(lamv.shape, lambda h, i: (0, 0)),
                  pl.BlockSpec((ATT_TQ, V_DIM), lambda h, i: (i, h)),
                  pl.BlockSpec((L_ALL, V_DIM), lambda h, i: (0, h)),
                  pl.BlockSpec((L_ALL, V_DIM), lambda h, i: (0, h)),
                  pl.BlockSpec((ATT_TQ, V_DIM), lambda h, i: (i, h)),
                  pl.BlockSpec((1, V_DIM), lambda h, i: (0, 0))],
        out_specs=pl.BlockSpec((ATT_TQ, V_DIM), lambda h, i: (i, h)),
        out_shape=jax.ShapeDtypeStruct((SEQ, ATT_W), _BF16),
        compiler_params=pltpu.CompilerParams(vmem_limit_bytes=VMEM_LIMIT),
        name="diff_attn",
    )(lamv, q, k, v, ga, subg)


def _conv_out_kernel(y_ref, yp_ref, yn_ref, cw_ref, cb_ref, lng_ref, lnb_ref, gc_ref, att_ref,
                     wa_ref, wc_ref, x_ref, mod_ref, o_ref, win_ref):
    i = pl.program_id(0)
    tm = y_ref.shape[0]
    halo = CONV_HALO
    keep_prev = (i > 0).astype(_F32)
    keep_next = (i < pl.num_programs(0) - 1).astype(_F32)
    win_ref[0:halo, :] = yp_ref[...].astype(_F32) * keep_prev
    win_ref[halo:halo + tm, :] = y_ref[...].astype(_F32)
    win_ref[halo + tm:, :] = yn_ref[...].astype(_F32) * keep_next
    acc = jnp.zeros((tm, CONV_W), _F32) + cb_ref[...]
    first = halo - CONV_K // 2
    for t in range(CONV_K):
        acc = acc + win_ref[first + t:first + t + tm, :] * cw_ref[t:t + 1, :]
    mu = jnp.mean(acc, axis=-1, keepdims=True)
    cen = acc - mu
    var = jnp.mean(cen * cen, axis=-1, keepdims=True)
    ln = (cen * lax.rsqrt(var + EPS)) * lng_ref[...] + lnb_ref[...]
    conv = (_silu(ln) * gc_ref[...].astype(_F32)).astype(_BF16)
    y = (jnp.dot(att_ref[...], wa_ref[...], preferred_element_type=_F32)
         + jnp.dot(conv, wc_ref[...], preferred_element_type=_F32))
    gate = mod_ref[0:1, 2 * D_MODEL:3 * D_MODEL]
    o_ref[...] = x_ref[...] + gate * y


def _conv_out(yglu, cw, cb, lng, lnb, gc, att, w_out_bf, x2, mod):
    tm = OUT_TM
    nb = tm // CONV_HALO
    last = SEQ // CONV_HALO - 1
    row = lambda i: (i, 0)
    fixed = lambda i: (0, 0)
    return pl.pallas_call(
        _conv_out_kernel,
        grid=(SEQ // tm,),
        in_specs=[pl.BlockSpec((tm, CONV_W), row),
                  pl.BlockSpec((CONV_HALO, CONV_W), lambda i: (jnp.maximum(i * nb - 1, 0), 0)),
                  pl.BlockSpec((CONV_HALO, CONV_W), lambda i: (jnp.minimum((i + 1) * nb, last), 0)),
                  pl.BlockSpec((CONV_K, CONV_W), fixed),
                  pl.BlockSpec((1, CONV_W), fixed),
                  pl.BlockSpec((1, CONV_W), fixed),
                  pl.BlockSpec((1, CONV_W), fixed),
                  pl.BlockSpec((tm, CONV_W), row),
                  pl.BlockSpec((tm, ATT_W), row),
                  pl.BlockSpec((ATT_W, D_MODEL), fixed),
                  pl.BlockSpec((CONV_W, D_MODEL), lambda i: (1, 0)),
                  pl.BlockSpec((tm, D_MODEL), row),
                  pl.BlockSpec((MOD_ROWS, 3 * D_MODEL), fixed)],
        out_specs=pl.BlockSpec((tm, D_MODEL), row),
        out_shape=jax.ShapeDtypeStruct((SEQ, D_MODEL), _F32),
        scratch_shapes=[pltpu.VMEM((tm + 2 * CONV_HALO, CONV_W), _F32)],
        compiler_params=pltpu.CompilerParams(vmem_limit_bytes=VMEM_LIMIT),
        name="conv_out",
    )(yglu, yglu, yglu, cw, cb, lng, lnb, gc, att, w_out_bf, w_out_bf, x2, mod)


def _rope_tables():
    rows = SEQ // GRID_W
    row = jnp.repeat(jnp.arange(rows), GRID_W).astype(_F32)
    col = jnp.tile(jnp.arange(GRID_W), rows).astype(_F32)
    nf = HEAD_DIM // 4
    inv = ROPE_BASE ** (-jnp.arange(nf, dtype=_F32) / nf)
    ang_r = row[:, None] * inv
    ang_c = col[:, None] * inv
    ang = jnp.concatenate([ang_r, ang_r, ang_c, ang_c], axis=-1)
    cos, sin = jnp.cos(ang), jnp.sin(ang)
    first_half = (jnp.arange(HEAD_DIM) % (2 * nf)) < nf
    sa = jnp.where(first_half, -sin, 0.0)
    sb = jnp.where(first_half, 0.0, sin)
    pad = lambda t, v: jnp.concatenate([t, jnp.full((CTX_LEN, HEAD_DIM), v, _F32)], axis=0)
    two = lambda t: jnp.concatenate([t, t], axis=1)
    return two(pad(cos, 1.0)), two(pad(sa, 0.0)), two(pad(sb, 0.0))


def _chunk_avg():
    r = jnp.arange(NORM_GROUP) // HEAD_DIM
    return jnp.where(r[:, None] == r[None, :], 1.0 / HEAD_DIM, 0.0).astype(_BF16)


def kernel(x, c, ctx, c_ctx, w_ada, b_ada, norm_g, w_in, q_norm_g, k_norm_g, lam_q1, lam_k1,
           lam_q2, lam_k2, sub_norm_g, conv_w, conv_b, conv_ln_g, conv_ln_b, w_out):
    assert x.shape == (1, SEQ, D_MODEL) and ctx.shape == (1, CTX_LEN, D_MODEL)
    assert w_ada.shape[0] == 1 and w_in.shape == (1, D_MODEL, IN_W)
    x2 = x[0]
    ctx2 = ctx[0]

    cc = jnp.zeros((MOD_ROWS, D_MODEL), _F32).at[0].set(c[0]).at[1].set(c_ctx)
    mod = _adaln(cc, w_ada[0], b_ada)
    h = _modulate(x2, ctx2, mod, norm_g)

    w = w_in[0].astype(_BF16)
    w_out_bf = w_out[0].astype(_BF16)
    cos, sa, sb = _rope_tables()
    scale = 1.0 / math.sqrt(HEAD_DIM)
    avg = _chunk_avg()
    reps = PROJ_TN // HEAD_DIM
    gq = jnp.tile(q_norm_g, (1, reps))
    gk = jnp.tile(k_norm_g, (1, reps))

    def qk_extras(g, tabs, tm):
        fixed = lambda j, i: (0, 0)
        tab = lambda j, i: (i, 0)
        return ([(g, (1, PROJ_TN), fixed), (avg, (NORM_GROUP, NORM_GROUP), fixed)]
                + [(t, (tm, 2 * HEAD_DIM), tab) for t in tabs])

    q = _proj_call(_qk_kernel, h, [(w, Q0)], qk_extras(gq, (cos * scale, sa * scale, sb * scale), PROJ_TM),
                   rows=SEQ, tm=PROJ_TM, tn=PROJ_TN, n_col_tiles=1, out_dtype=_BF16, name="proj_q")
    k = _proj_call(_qk_kernel, h, [(w, K0)], qk_extras(gk, (cos, sa, sb), PROJ_TM_ALL),
                   rows=L_ALL, tm=PROJ_TM_ALL, tn=PROJ_TN, n_col_tiles=1, out_dtype=_BF16, name="proj_k")
    v = _proj_call(_plain_kernel, h, [(w, V0)], [], rows=L_ALL, tm=PROJ_TM_ALL, tn=PROJ_TN,
                   n_col_tiles=1, out_dtype=_BF16, name="proj_v")
    ga = _proj_call(_silu_kernel, h, [(w, GA0)], [], rows=SEQ, tm=PROJ_TM, tn=PROJ_TN,
                    n_col_tiles=1, out_dtype=_BF16, name="proj_ga")
    gc = _proj_call(_silu_kernel, h, [(w, GC0)], [], rows=SEQ, tm=PROJ_TM, tn=PROJ_TN,
                    n_col_tiles=1, out_dtype=_BF16, name="proj_gc")
    yglu = _proj_call(_glu_kernel, h, [(w, U0), (w, U0 + CONV_W)], [], rows=SEQ, tm=PROJ_TM,
                      tn=GLU_TN, n_col_tiles=CONV_W // GLU_TN, out_dtype=_BF16, name="proj_glu")

    lamv = jnp.concatenate([lam_q1, lam_k1, lam_q2, lam_k2], axis=0)
    att = _attention(lamv, q, k, v, ga, sub_norm_g)

    out = _conv_out(yglu, conv_w[0], conv_b, conv_ln_g, conv_ln_b, gc, att, w_out_bf, x2, mod)
    return out[None]
```

```python
import math

import jax
import jax.numpy as jnp
from jax import lax
from jax.experimental import pallas as pl
from jax.experimental.pallas import tpu as pltpu

D_MODEL = 2048
SEQ = 8192
GRID_W = 64
CTX_LEN = 256
L_ALL = SEQ + CTX_LEN
HEADS = 8
HEAD_DIM = 64
V_DIM = 2 * HEAD_DIM
ATT_W = HEADS * V_DIM
CONV_W = D_MODEL - ATT_W
CONV_K = 31
CONV_HALO = 16
ROPE_BASE = 10000.0
EPS = 1e-6
LAM_INIT = 0.8 - 0.6 * math.exp(-0.3 * 0)
LOG2E = math.log2(math.e)

Q0 = 0
K0 = Q0 + HEADS * 2 * HEAD_DIM
V0 = K0 + HEADS * 2 * HEAD_DIM
GA0 = V0 + ATT_W
U0 = GA0 + ATT_W
GC0 = U0 + 2 * CONV_W
IN_W = GC0 + CONV_W

SUBLANES = 8
MOD_ROWS = SUBLANES
ADA_TN = 512
MOD_TM = 256
PROJ_TM = 512
PROJ_TM_ALL = 768
PROJ_TN = 1024
GLU_TN = 512
ATT_TQ = 256
ATT_KB = PROJ_TM_ALL
N_KB = L_ALL // ATT_KB
OUT_TM = 256
NORM_GROUP = 256
VMEM_LIMIT = 48 * 1024 * 1024
MAX_FIXED_SHIFT = 40.0

_F32 = jnp.float32
_BF16 = jnp.bfloat16


def _sigmoid(x):
    return 1.0 / (1.0 + jnp.exp(-x))


def _silu(x):
    return x * _sigmoid(x)


def _adaln_kernel(cc_ref, w_ref, b_ref, o_ref):
    s = _silu(cc_ref[...])
    o_ref[...] = jnp.dot(s, w_ref[...], precision=lax.Precision.HIGHEST,
                         preferred_element_type=_F32) + b_ref[...]


def _adaln(cc, w, b):
    n = w.shape[1]
    return pl.pallas_call(
        _adaln_kernel,
        grid=(n // ADA_TN,),
        in_specs=[pl.BlockSpec((MOD_ROWS, D_MODEL), lambda j: (0, 0)),
                  pl.BlockSpec((D_MODEL, ADA_TN), lambda j: (0, j)),
                  pl.BlockSpec((1, ADA_TN), lambda j: (0, j))],
        out_specs=pl.BlockSpec((MOD_ROWS, ADA_TN), lambda j: (0, j)),
        out_shape=jax.ShapeDtypeStruct((MOD_ROWS, n), _F32),
        compiler_params=pltpu.CompilerParams(vmem_limit_bytes=VMEM_LIMIT),
        name="adaln",
    )(cc, w, b)


def _modulate_rows(xin, g, shift, scale):
    r = lax.rsqrt(jnp.mean(xin * xin, axis=-1, keepdims=True) + EPS)
    return ((xin * r) * g) * (1.0 + scale) + shift


def _modulate_kernel(x_ref, ctx_ref, mod_ref, g_ref, h_ref):
    i = pl.program_id(0)
    n_lat = pl.num_programs(0) - 1
    g = g_ref[...]

    @pl.when(i < n_lat)
    def _():
        h_ref[...] = _modulate_rows(x_ref[...], g, mod_ref[0:1, 0:D_MODEL],
                                    mod_ref[0:1, D_MODEL:2 * D_MODEL]).astype(h_ref.dtype)

    @pl.when(i == n_lat)
    def _():
        h_ref[...] = _modulate_rows(ctx_ref[...], g, mod_ref[1:2, 0:D_MODEL],
                                    mod_ref[1:2, D_MODEL:2 * D_MODEL]).astype(h_ref.dtype)


def _modulate(x2, ctx2, mod, g):
    n_lat = SEQ // MOD_TM
    return pl.pallas_call(
        _modulate_kernel,
        grid=(n_lat + CTX_LEN // MOD_TM,),
        in_specs=[pl.BlockSpec((MOD_TM, D_MODEL), lambda i: (jnp.minimum(i, n_lat - 1), 0)),
                  pl.BlockSpec((CTX_LEN, D_MODEL), lambda i: (0, 0)),
                  pl.BlockSpec((MOD_ROWS, 3 * D_MODEL), lambda i: (0, 0)),
                  pl.BlockSpec((1, D_MODEL), lambda i: (0, 0))],
        out_specs=pl.BlockSpec((MOD_TM, D_MODEL), lambda i: (i, 0)),
        out_shape=jax.ShapeDtypeStruct((L_ALL, D_MODEL), _BF16),
        compiler_params=pltpu.CompilerParams(vmem_limit_bytes=VMEM_LIMIT),
        name="modulate",
    )(x2, ctx2, mod, g)


def _proj_call(kernel, h, w_list, extras, *, rows, tm, tn, n_col_tiles, out_dtype, name,
               out_spec=None, out_shape=None):
    in_specs = [pl.BlockSpec((tm, D_MODEL), lambda j, i: (i, 0))]
    args = [h]
    for w, c0 in w_list:
        in_specs.append(pl.BlockSpec((D_MODEL, tn), lambda j, i, c0=c0: (0, c0 // tn + j)))
        args.append(w)
    for a, bs, im in extras:
        in_specs.append(pl.BlockSpec(bs, im))
        args.append(a)
    if out_spec is None:
        out_spec = pl.BlockSpec((tm, tn), lambda j, i: (i, j))
        out_shape = jax.ShapeDtypeStruct((rows, n_col_tiles * tn), out_dtype)
    return pl.pallas_call(
        kernel,
        grid=(n_col_tiles, rows // tm),
        in_specs=in_specs,
        out_specs=out_spec,
        out_shape=out_shape,
        compiler_params=pltpu.CompilerParams(vmem_limit_bytes=VMEM_LIMIT),
        name=name,
    )(*args)


def _vt_kernel(h_ref, w_ref, o_ref):
    p = jnp.dot(h_ref[...], w_ref[...], preferred_element_type=_F32)
    pt = p.T.astype(o_ref.dtype)
    for hd in range(HEADS):
        o_ref[hd, 0] = pt[hd * V_DIM:(hd + 1) * V_DIM, :]


def _silu_kernel(h_ref, w_ref, o_ref):
    p = jnp.dot(h_ref[...], w_ref[...], preferred_element_type=_F32)
    o_ref[...] = _silu(p).astype(o_ref.dtype)


def _glu_kernel(h_ref, wa_ref, wg_ref, o_ref):
    h = h_ref[...]
    a = jnp.dot(h, wa_ref[...], preferred_element_type=_F32)
    g = jnp.dot(h, wg_ref[...], preferred_element_type=_F32)
    o_ref[...] = (a * _sigmoid(g)).astype(o_ref.dtype)


def _qk_kernel(h_ref, w_ref, g_ref, avg_ref, cos_ref, sa_ref, sb_ref, o_ref):
    p = jnp.dot(h_ref[...], w_ref[...], preferred_element_type=_F32)
    tn = p.shape[1]
    ss = p * p
    hi = ss.astype(_BF16)
    lo = (ss - hi.astype(_F32)).astype(_BF16)
    avg = avg_ref[...]
    ms = []
    for c in range(tn // NORM_GROUP):
        sl = slice(c * NORM_GROUP, (c + 1) * NORM_GROUP)
        ms.append(jnp.dot(hi[:, sl], avg, preferred_element_type=_F32)
                  + jnp.dot(lo[:, sl], avg, preferred_element_type=_F32))
    ms = jnp.concatenate(ms, axis=1)
    y = (p * lax.rsqrt(ms + EPS)) * g_ref[...]
    reps = tn // cos_ref.shape[1]
    cos = jnp.concatenate([cos_ref[...]] * reps, axis=1)
    sa = jnp.concatenate([sa_ref[...]] * reps, axis=1)
    sb = jnp.concatenate([sb_ref[...]] * reps, axis=1)
    quarter = HEAD_DIM // 4
    out = y * cos + pltpu.roll(y, tn - quarter, 1) * sa + pltpu.roll(y, quarter, 1) * sb
    o_ref[...] = out.astype(o_ref.dtype)


def _attn_kernel(shift_ref, lamv_ref, q_ref, k_ref, vt_ref, ga_ref, subg_ref, o_ref):
    tq = q_ref.shape[0]
    qt = q_ref[...].astype(_F32).T
    row = lax.broadcasted_iota(jnp.int32, qt.shape, 0)
    qst = jnp.concatenate([jnp.where(row < HEAD_DIM, qt, 0.0),
                           jnp.where(row >= HEAD_DIM, qt, 0.0)], axis=1).astype(_BF16)
    groups = ATT_KB // SUBLANES

    def scores(kb):
        start = pl.multiple_of(kb * ATT_KB, ATT_KB)
        return jnp.dot(k_ref[pl.ds(start, ATT_KB), :], qst, preferred_element_type=_F32)

    def finish(l, acc):
        o = acc / jnp.sum(l, axis=0, keepdims=True)
        lv = lamv_ref[...]
        lam = (jnp.exp(jnp.sum(lv[0:1] * lv[1:2], axis=1, keepdims=True))
               - jnp.exp(jnp.sum(lv[2:3] * lv[3:4], axis=1, keepdims=True)) + LAM_INIT)
        od = (o[:, :tq] - lam * o[:, tq:]).T
        r = lax.rsqrt(jnp.mean(od * od, axis=-1, keepdims=True) + EPS)
        on = ((od * r) * subg_ref[...]) * (1.0 - LAM_INIT)
        o_ref[...] = (on * ga_ref[...].astype(_F32)).astype(o_ref.dtype)

    l0 = jnp.zeros((SUBLANES, 2 * tq), _F32)
    a0 = jnp.zeros((V_DIM, 2 * tq), _F32)
    fixed_ok = shift_ref[1] > 0.5

    @pl.when(fixed_ok)
    def _():
        shift = shift_ref[0]

        def body(kb, carry):
            l, acc = carry
            p = jnp.exp2(scores(kb) - shift)
            l = l + jnp.sum(p.reshape(groups, SUBLANES, 2 * tq), axis=0)
            acc = acc + jnp.dot(vt_ref[0, kb], p.astype(_BF16), preferred_element_type=_F32)
            return l, acc

        finish(*lax.fori_loop(0, N_KB, body, (l0, a0), unroll=True))

    @pl.when(jnp.logical_not(fixed_ok))
    def _():
        def body(kb, carry):
            m, l, acc = carry
            st = scores(kb)
            mb = jnp.max(jnp.max(st.reshape(groups, SUBLANES, 2 * tq), axis=0), axis=0, keepdims=True)
            m_new = jnp.maximum(m, mb)
            alpha = jnp.exp2(m - m_new)
            p = jnp.exp2(st - m_new)
            l = alpha * l + jnp.sum(p.reshape(groups, SUBLANES, 2 * tq), axis=0)
            acc = alpha * acc + jnp.dot(vt_ref[0, kb], p.astype(_BF16), preferred_element_type=_F32)
            return m_new, l, acc

        m0 = jnp.full((1, 2 * tq), -1e30, _F32)
        _, l, acc = lax.fori_loop(0, N_KB, body, (m0, l0, a0))
        finish(l, acc)


def _attention(shift, lamv, q, k, vt, ga, subg):
    return pl.pallas_call(
        _attn_kernel,
        grid=(HEADS, SEQ // ATT_TQ),
        in_specs=[pl.BlockSpec(memory_space=pltpu.SMEM),
                  pl.BlockSpec(lamv.shape, lambda h, i: (0, 0)),
                  pl.BlockSpec((ATT_TQ, V_DIM), lambda h, i: (i, h)),
                  pl.BlockSpec((L_ALL, V_DIM), lambda h, i: (0, h)),
                  pl.BlockSpec((1, N_KB, V_DIM, ATT_KB), lambda h, i: (h, 0, 0, 0)),
                  pl.BlockSpec((ATT_TQ, V_DIM), lambda h, i: (i, h)),
                  pl.BlockSpec((1, V_DIM), lambda h, i: (0, 0))],
        out_specs=pl.BlockSpec((ATT_TQ, V_DIM), lambda h, i: (i, h)),
        out_shape=jax.ShapeDtypeStruct((SEQ, ATT_W), _BF16),
        compiler_params=pltpu.CompilerParams(vmem_limit_bytes=VMEM_LIMIT),
        name="diff_attn",
    )(shift, lamv, q, k, vt, ga, subg)


def _conv_out_kernel(y_ref, yp_ref, yn_ref, cw_ref, cb_ref, lng_ref, lnb_ref, gc_ref, att_ref,
                     wa_ref, wc_ref, x_ref, mod_ref, o_ref, win_ref):
    i = pl.program_id(0)
    tm = y_ref.shape[0]
    halo = CONV_HALO
    keep_prev = (i > 0).astype(_F32)
    keep_next = (i < pl.num_programs(0) - 1).astype(_F32)
    win_ref[0:halo, :] = yp_ref[...].astype(_F32) * keep_prev
    win_ref[halo:halo + tm, :] = y_ref[...].astype(_F32)
    win_ref[halo + tm:, :] = yn_ref[...].astype(_F32) * keep_next
    acc = jnp.zeros((tm, CONV_W), _F32) + cb_ref[...]
    first = halo - CONV_K // 2
    for t in range(CONV_K):
        acc = acc + win_ref[first + t:first + t + tm, :] * cw_ref[t:t + 1, :]
    mu = jnp.mean(acc, axis=-1, keepdims=True)
    cen = acc - mu
    var = jnp.mean(cen * cen, axis=-1, keepdims=True)
    ln = (cen * lax.rsqrt(var + EPS)) * lng_ref[...] + lnb_ref[...]
    conv = (_silu(ln) * gc_ref[...].astype(_F32)).astype(_BF16)
    y = (jnp.dot(att_ref[...], wa_ref[...], preferred_element_type=_F32)
         + jnp.dot(conv, wc_ref[...], preferred_element_type=_F32))
    gate = mod_ref[0:1, 2 * D_MODEL:3 * D_MODEL]
    o_ref[...] = x_ref[...] + gate * y


def _conv_out(yglu, cw, cb, lng, lnb, gc, att, w_out_bf, x2, mod):
    tm = OUT_TM
    nb = tm // CONV_HALO
    last = SEQ // CONV_HALO - 1
    row = lambda i: (i, 0)
    fixed = lambda i: (0, 0)
    return pl.pallas_call(
        _conv_out_kernel,
        grid=(SEQ // tm,),
        in_specs=[pl.BlockSpec((tm, CONV_W), row),
                  pl.BlockSpec((CONV_HALO, CONV_W), lambda i: (jnp.maximum(i * nb - 1, 0), 0)),
                  pl.BlockSpec((CONV_HALO, CONV_W), lambda i: (jnp.minimum((i + 1) * nb, last), 0)),
                  pl.BlockSpec((CONV_K, CONV_W), fixed),
                  pl.BlockSpec((1, CONV_W), fixed),
                  pl.BlockSpec((1, CONV_W), fixed),
                  pl.BlockSpec((1, CONV_W), fixed),
                  pl.BlockSpec((tm, CONV_W), row),
                  pl.BlockSpec((tm, ATT_W), row),
                  pl.BlockSpec((ATT_W, D_MODEL), fixed),
                  pl.BlockSpec((CONV_W, D_MODEL), lambda i: (1, 0)),
                  pl.BlockSpec((tm, D_MODEL), row),
                  pl.BlockSpec((MOD_ROWS, 3 * D_MODEL), fixed)],
        out_specs=pl.BlockSpec((tm, D_MODEL), row),
        out_shape=jax.ShapeDtypeStruct((SEQ, D_MODEL), _F32),
        scratch_shapes=[pltpu.VMEM((tm + 2 * CONV_HALO, CONV_W), _F32)],
        compiler_params=pltpu.CompilerParams(vmem_limit_bytes=VMEM_LIMIT),
        name="conv_out",
    )(yglu, yglu, yglu, cw, cb, lng, lnb, gc, att, w_out_bf, w_out_bf, x2, mod)


def _rope_tables():
    rows = SEQ // GRID_W
    row = jnp.repeat(jnp.arange(rows), GRID_W).astype(_F32)
    col = jnp.tile(jnp.arange(GRID_W), rows).astype(_F32)
    nf = HEAD_DIM // 4
    inv = ROPE_BASE ** (-jnp.arange(nf, dtype=_F32) / nf)
    ang_r = row[:, None] * inv
    ang_c = col[:, None] * inv
    ang = jnp.concatenate([ang_r, ang_r, ang_c, ang_c], axis=-1)
    cos, sin = jnp.cos(ang), jnp.sin(ang)
    first_half = (jnp.arange(HEAD_DIM) % (2 * nf)) < nf
    sa = jnp.where(first_half, -sin, 0.0)
    sb = jnp.where(first_half, 0.0, sin)
    pad = lambda t, v: jnp.concatenate([t, jnp.full((CTX_LEN, HEAD_DIM), v, _F32)], axis=0)
    two = lambda t: jnp.concatenate([t, t], axis=1)
    return two(pad(cos, 1.0)), two(pad(sa, 0.0)), two(pad(sb, 0.0))


def _chunk_avg():
    r = jnp.arange(NORM_GROUP) // HEAD_DIM
    return jnp.where(r[:, None] == r[None, :], 1.0 / HEAD_DIM, 0.0).astype(_BF16)


def _softmax_shift(q_norm_g, k_norm_g, q_scale):
    bound = (HEAD_DIM * q_scale * 1.01) * jnp.max(jnp.abs(q_norm_g)) * jnp.max(jnp.abs(k_norm_g))
    ok = (bound <= MAX_FIXED_SHIFT).astype(_F32)
    return jnp.stack([jnp.where(ok > 0.5, bound, 0.0), ok]).astype(_F32)


def kernel(x, c, ctx, c_ctx, w_ada, b_ada, norm_g, w_in, q_norm_g, k_norm_g, lam_q1, lam_k1,
           lam_q2, lam_k2, sub_norm_g, conv_w, conv_b, conv_ln_g, conv_ln_b, w_out):
    assert x.shape == (1, SEQ, D_MODEL) and ctx.shape == (1, CTX_LEN, D_MODEL)
    assert w_ada.shape[0] == 1 and w_in.shape == (1, D_MODEL, IN_W)
    x2 = x[0]
    ctx2 = ctx[0]

    cc = jnp.zeros((MOD_ROWS, D_MODEL), _F32).at[0].set(c[0]).at[1].set(c_ctx)
    mod = _adaln(cc, w_ada[0], b_ada)
    h = _modulate(x2, ctx2, mod, norm_g)

    w = w_in[0].astype(_BF16)
    w_out_bf = w_out[0].astype(_BF16)
    cos, sa, sb = _rope_tables()
    q_scale = LOG2E / math.sqrt(HEAD_DIM)
    avg = _chunk_avg()
    reps = PROJ_TN // HEAD_DIM
    gq = jnp.tile(q_norm_g, (1, reps))
    gk = jnp.tile(k_norm_g, (1, reps))

    def qk_extras(g, tabs, tm):
        fixed = lambda j, i: (0, 0)
        tab = lambda j, i: (i, 0)
        return ([(g, (1, PROJ_TN), fixed), (avg, (NORM_GROUP, NORM_GROUP), fixed)]
                + [(t, (tm, 2 * HEAD_DIM), tab) for t in tabs])

    q = _proj_call(_qk_kernel, h, [(w, Q0)], qk_extras(gq, (cos * q_scale, sa * q_scale, sb * q_scale), PROJ_TM),
                   rows=SEQ, tm=PROJ_TM, tn=PROJ_TN, n_col_tiles=1, out_dtype=_BF16, name="proj_q")
    k = _proj_call(_qk_kernel, h, [(w, K0)], qk_extras(gk, (cos, sa, sb), PROJ_TM_ALL),
                   rows=L_ALL, tm=PROJ_TM_ALL, tn=PROJ_TN, n_col_tiles=1, out_dtype=_BF16, name="proj_k")
    vt = _proj_call(_vt_kernel, h, [(w, V0)], [], rows=L_ALL, tm=PROJ_TM_ALL, tn=PROJ_TN,
                    n_col_tiles=1, out_dtype=_BF16, name="proj_vt",
                    out_spec=pl.BlockSpec((HEADS, 1, V_DIM, PROJ_TM_ALL), lambda j, i: (0, i, 0, 0)),
                    out_shape=jax.ShapeDtypeStruct((HEADS, N_KB, V_DIM, PROJ_TM_ALL), _BF16))
    ga = _proj_call(_silu_kernel, h, [(w, GA0)], [], rows=SEQ, tm=PROJ_TM, tn=PROJ_TN,
                    n_col_tiles=1, out_dtype=_BF16, name="proj_ga")
    gc = _proj_call(_silu_kernel, h, [(w, GC0)], [], rows=SEQ, tm=PROJ_TM, tn=PROJ_TN,
                    n_col_tiles=1, out_dtype=_BF16, name="proj_gc")
    yglu = _proj_call(_glu_kernel, h, [(w, U0), (w, U0 + CONV_W)], [], rows=SEQ, tm=PROJ_TM,
                      tn=GLU_TN, n_col_tiles=CONV_W // GLU_TN, out_dtype=_BF16, name="proj_glu")

    lamv = jnp.concatenate([lam_q1, lam_k1, lam_q2, lam_k2], axis=0)
    shift = _softmax_shift(q_norm_g, k_norm_g, q_scale)
    att = _attention(shift, lamv, q, k, vt, ga, sub_norm_g)

    out = _conv_out(yglu, conv_w[0], conv_b, conv_ln_g, conv_ln_b, gc, att, w_out_bf, x2, mod)
    return out[None]
```

```python
import math

import jax
import jax.numpy as jnp
import numpy as np
from jax import lax
from jax.experimental import pallas as pl
from jax.experimental.pallas import tpu as pltpu

D_MODEL = 2048
SEQ = 8192
GRID_W = 64
CTX_LEN = 256
L_ALL = SEQ + CTX_LEN
HEADS = 8
HEAD_DIM = 64
V_DIM = 2 * HEAD_DIM
ATT_W = HEADS * V_DIM
CONV_W = D_MODEL - ATT_W
CONV_K = 31
CONV_HALO = 16
ROPE_BASE = 10000.0
EPS = 1e-6
LAM_INIT = 0.8 - 0.6 * math.exp(-0.3 * 0)
LOG2E = math.log2(math.e)

Q0 = 0
K0 = Q0 + HEADS * 2 * HEAD_DIM
V0 = K0 + HEADS * 2 * HEAD_DIM
GA0 = V0 + ATT_W
U0 = GA0 + ATT_W
GC0 = U0 + 2 * CONV_W
IN_W = GC0 + CONV_W

SUBLANES = 8
MOD_ROWS = SUBLANES
ADA_TN = 1024
MOD_TM = 256
PROJ_TM = 512
PROJ_TM_ALL = 768
PROJ_TN = 1024
GLU_TN = 512
ATT_TQ = 512
ATT_KB = PROJ_TM_ALL
N_KB = L_ALL // ATT_KB
OUT_TM = 256
CONV_LANES = 128
CONV_CHUNKS = CONV_W // CONV_LANES
CONV_ROWS = 64
OUT_LANES = D_MODEL // CONV_CHUNKS
CONV_FIRST = CONV_HALO - CONV_K // 2
CONV_SPAN_EXTRA = SUBLANES * ((CONV_FIRST + CONV_K - 1) // SUBLANES)
NORM_GROUP = 256
VMEM_LIMIT = 48 * 1024 * 1024
MAX_FIXED_SHIFT = 40.0

_F32 = jnp.float32
_BF16 = jnp.bfloat16


def _sigmoid(x):
    return 1.0 / (1.0 + jnp.exp(-x))


def _silu(x):
    return x * _sigmoid(x)


def _adaln_kernel(cc_ref, w_ref, b_ref, o_ref):
    s = _silu(cc_ref[...]).astype(_BF16)
    o_ref[...] = jnp.dot(s, w_ref[...].astype(_BF16), preferred_element_type=_F32) + b_ref[...]


def _adaln(cc, w, b):
    n = w.shape[1]
    return pl.pallas_call(
        _adaln_kernel,
        grid=(n // ADA_TN,),
        in_specs=[pl.BlockSpec((MOD_ROWS, D_MODEL), lambda j: (0, 0)),
                  pl.BlockSpec((D_MODEL, ADA_TN), lambda j: (0, j)),
                  pl.BlockSpec((1, ADA_TN), lambda j: (0, j))],
        out_specs=pl.BlockSpec((MOD_ROWS, ADA_TN), lambda j: (0, j)),
        out_shape=jax.ShapeDtypeStruct((MOD_ROWS, n), _F32),
        compiler_params=pltpu.CompilerParams(vmem_limit_bytes=VMEM_LIMIT),
        name="adaln",
    )(cc, w, b)


def _modulate_rows(xin, g, shift, scale):
    r = lax.rsqrt(jnp.mean(xin * xin, axis=-1, keepdims=True) + EPS)
    return ((xin * r) * g) * (1.0 + scale) + shift


def _modulate_kernel(x_ref, ctx_ref, mod_ref, g_ref, h_ref):
    i = pl.program_id(0)
    n_lat = pl.num_programs(0) - 1
    g = g_ref[...]

    @pl.when(i < n_lat)
    def _():
        h_ref[...] = _modulate_rows(x_ref[...], g, mod_ref[0:1, 0:D_MODEL],
                                    mod_ref[0:1, D_MODEL:2 * D_MODEL]).astype(h_ref.dtype)

    @pl.when(i == n_lat)
    def _():
        h_ref[...] = _modulate_rows(ctx_ref[...], g, mod_ref[1:2, 0:D_MODEL],
                                    mod_ref[1:2, D_MODEL:2 * D_MODEL]).astype(h_ref.dtype)


def _modulate(x2, ctx2, mod, g):
    n_lat = SEQ // MOD_TM
    return pl.pallas_call(
        _modulate_kernel,
        grid=(n_lat + CTX_LEN // MOD_TM,),
        in_specs=[pl.BlockSpec((MOD_TM, D_MODEL), lambda i: (jnp.minimum(i, n_lat - 1), 0)),
                  pl.BlockSpec((CTX_LEN, D_MODEL), lambda i: (0, 0)),
                  pl.BlockSpec((MOD_ROWS, 3 * D_MODEL), lambda i: (0, 0)),
                  pl.BlockSpec((1, D_MODEL), lambda i: (0, 0))],
        out_specs=pl.BlockSpec((MOD_TM, D_MODEL), lambda i: (i, 0)),
        out_shape=jax.ShapeDtypeStruct((L_ALL, D_MODEL), _BF16),
        compiler_params=pltpu.CompilerParams(vmem_limit_bytes=VMEM_LIMIT),
        name="modulate",
    )(x2, ctx2, mod, g)


def _proj_call(kernel, h, w_list, extras, *, rows, tm, tn, n_col_tiles, out_dtype, name,
               out_spec=None, out_shape=None):
    in_specs = [pl.BlockSpec((tm, D_MODEL), lambda j, i: (i, 0))]
    args = [h]
    for w, c0 in w_list:
        in_specs.append(pl.BlockSpec((D_MODEL, tn), lambda j, i, c0=c0: (0, c0 // tn + j)))
        args.append(w)
    for a, bs, im in extras:
        in_specs.append(pl.BlockSpec(bs, im))
        args.append(a)
    if out_spec is None:
        out_spec = pl.BlockSpec((tm, tn), lambda j, i: (i, j))
        out_shape = jax.ShapeDtypeStruct((rows, n_col_tiles * tn), out_dtype)
    return pl.pallas_call(
        kernel,
        grid=(n_col_tiles, rows // tm),
        in_specs=in_specs,
        out_specs=out_spec,
        out_shape=out_shape,
        compiler_params=pltpu.CompilerParams(vmem_limit_bytes=VMEM_LIMIT),
        name=name,
    )(*args)


def _vt_kernel(h_ref, w_ref, o_ref):
    p = jnp.dot(h_ref[...], w_ref[...].astype(_BF16), preferred_element_type=_F32)
    pt = p.T.astype(o_ref.dtype)
    for hd in range(HEADS):
        o_ref[hd, 0] = pt[hd * V_DIM:(hd + 1) * V_DIM, :]


def _silu_kernel(h_ref, w_ref, o_ref):
    p = jnp.dot(h_ref[...], w_ref[...].astype(_BF16), preferred_element_type=_F32)
    o_ref[...] = _silu(p).astype(o_ref.dtype)


def _glu_kernel(h_ref, wa_ref, wg_ref, o_ref):
    h = h_ref[...]
    a = jnp.dot(h, wa_ref[...].astype(_BF16), preferred_element_type=_F32)
    g = jnp.dot(h, wg_ref[...].astype(_BF16), preferred_element_type=_F32)
    o_ref[...] = (a * _sigmoid(g)).astype(o_ref.dtype)


def _qk_kernel(h_ref, w_ref, g_ref, avg_ref, cos_ref, sa_ref, sb_ref, o_ref):
    p = jnp.dot(h_ref[...], w_ref[...].astype(_BF16), preferred_element_type=_F32)
    tn = p.shape[1]
    ss = p * p
    hi = ss.astype(_BF16)
    lo = (ss - hi.astype(_F32)).astype(_BF16)
    avg = avg_ref[...]
    ms = []
    for c in range(tn // NORM_GROUP):
        sl = slice(c * NORM_GROUP, (c + 1) * NORM_GROUP)
        ms.append(jnp.dot(hi[:, sl], avg, preferred_element_type=_F32)
                  + jnp.dot(lo[:, sl], avg, preferred_element_type=_F32))
    ms = jnp.concatenate(ms, axis=1)
    y = (p * lax.rsqrt(ms + EPS)) * g_ref[...]
    reps = tn // cos_ref.shape[1]
    cos = jnp.concatenate([cos_ref[...]] * reps, axis=1)
    sa = jnp.concatenate([sa_ref[...]] * reps, axis=1)
    sb = jnp.concatenate([sb_ref[...]] * reps, axis=1)
    quarter = HEAD_DIM // 4
    out = y * cos + pltpu.roll(y, tn - quarter, 1) * sa + pltpu.roll(y, quarter, 1) * sb
    o_ref[...] = out.astype(o_ref.dtype)


def _attn_kernel(shift_ref, lamv_ref, q_ref, k_ref, vt_ref, ga_ref, subg_ref, o_ref):
    tq = q_ref.shape[0]
    qt = q_ref[...].astype(_F32).T
    row = lax.broadcasted_iota(jnp.int32, qt.shape, 0)
    qst = jnp.concatenate([jnp.where(row < HEAD_DIM, qt, 0.0),
                           jnp.where(row >= HEAD_DIM, qt, 0.0)], axis=1).astype(_BF16)
    groups = ATT_KB // SUBLANES

    def scores(kb):
        start = pl.multiple_of(kb * ATT_KB, ATT_KB)
        return jnp.dot(k_ref[pl.ds(start, ATT_KB), :], qst, preferred_element_type=_F32)

    def finish(l, acc):
        o = acc / jnp.sum(l, axis=0, keepdims=True)
        lv = lamv_ref[...]
        lam = (jnp.exp(jnp.sum(lv[0:1] * lv[1:2], axis=1, keepdims=True))
               - jnp.exp(jnp.sum(lv[2:3] * lv[3:4], axis=1, keepdims=True)) + LAM_INIT)
        od = (o[:, :tq] - lam * o[:, tq:]).T
        r = lax.rsqrt(jnp.mean(od * od, axis=-1, keepdims=True) + EPS)
        on = ((od * r) * subg_ref[...]) * (1.0 - LAM_INIT)
        o_ref[...] = (on * ga_ref[...].astype(_F32)).astype(o_ref.dtype)

    l0 = jnp.zeros((SUBLANES, 2 * tq), _F32)
    a0 = jnp.zeros((V_DIM, 2 * tq), _F32)
    fixed_ok = shift_ref[1] > 0.5

    @pl.when(fixed_ok)
    def _():
        shift = shift_ref[0]

        def body(kb, carry):
            l, acc = carry
            p = jnp.exp2(scores(kb) - shift)
            l = l + jnp.sum(p.reshape(groups, SUBLANES, 2 * tq), axis=0)
            acc = acc + jnp.dot(vt_ref[0, kb], p.astype(_BF16), preferred_element_type=_F32)
            return l, acc

        finish(*lax.fori_loop(0, N_KB, body, (l0, a0), unroll=True))

    @pl.when(jnp.logical_not(fixed_ok))
    def _():
        def body(kb, carry):
            m, l, acc = carry
            st = scores(kb)
            mb = jnp.max(jnp.max(st.reshape(groups, SUBLANES, 2 * tq), axis=0), axis=0, keepdims=True)
            m_new = jnp.maximum(m, mb)
            alpha = jnp.exp2(m - m_new)
            p = jnp.exp2(st - m_new)
            l = alpha * l + jnp.sum(p.reshape(groups, SUBLANES, 2 * tq), axis=0)
            acc = alpha * acc + jnp.dot(vt_ref[0, kb], p.astype(_BF16), preferred_element_type=_F32)
            return m_new, l, acc

        m0 = jnp.full((1, 2 * tq), -1e30, _F32)
        _, l, acc = lax.fori_loop(0, N_KB, body, (m0, l0, a0))
        finish(l, acc)


def _attention(shift, lamv, q, k, vt, ga, subg):
    return pl.pallas_call(
        _attn_kernel,
        grid=(HEADS, SEQ // ATT_TQ),
        in_specs=[pl.BlockSpec(memory_space=pltpu.SMEM),
                  pl.BlockSpec(lamv.shape, lambda h, i: (0, 0)),
                  pl.BlockSpec((ATT_TQ, V_DIM), lambda h, i: (i, h)),
                  pl.BlockSpec((L_ALL, V_DIM), lambda h, i: (0, h)),
                  pl.BlockSpec((1, N_KB, V_DIM, ATT_KB), lambda h, i: (h, 0, 0, 0)),
                  pl.BlockSpec((ATT_TQ, V_DIM), lambda h, i: (i, h)),
                  pl.BlockSpec((1, V_DIM), lambda h, i: (0, 0))],
        out_specs=pl.BlockSpec((ATT_TQ, V_DIM), lambda h, i: (i, h)),
        out_shape=jax.ShapeDtypeStruct((SEQ, ATT_W), _BF16),
        compiler_params=pltpu.CompilerParams(vmem_limit_bytes=VMEM_LIMIT),
        name="diff_attn",
    )(shift, lamv, q, k, vt, ga, subg)


def _conv_out_kernel(y_ref, yp_ref, yn_ref, cw_ref, cb_ref, lng_ref, lnb_ref, gc_ref, att_ref,
                     wa_ref, wc_ref, x_ref, mod_ref, o_ref, win_ref, acc_ref, shift_ref, conv_ref,
                     ybuf_ref):
    i = pl.program_id(0)
    n_tiles = pl.num_programs(0) - 1
    tile = jnp.minimum(i, n_tiles - 1)
    tm = y_ref.shape[0]
    halo = CONV_HALO

    @pl.when(i == 0)
    def _():
        conv_ref[1] = jnp.zeros(conv_ref.shape[1:], conv_ref.dtype)

    keep_prev = (tile > 0).astype(_F32)
    keep_next = (tile < n_tiles - 1).astype(_F32)
    for c in range(CONV_CHUNKS):
        cols = slice(c * CONV_LANES, (c + 1) * CONV_LANES)
        win_ref[c, 0:halo, :] = yp_ref[:, cols].astype(_F32) * keep_prev
        win_ref[c, halo:halo + tm, :] = y_ref[:, cols].astype(_F32)
        win_ref[c, halo + tm:, :] = yn_ref[:, cols].astype(_F32) * keep_next

    att = att_ref[...]
    conv_prev = conv_ref[(i + 1) % 2]
    span = shift_ref.shape[2]

    def chunk(c, carry):
        yc = (jnp.dot(att, wa_ref[c], preferred_element_type=_F32)
              + jnp.dot(conv_prev, wc_ref[c], preferred_element_type=_F32))
        for res in range(SUBLANES):
            shift_ref[c, res] = win_ref[c, res:res + span, :]
        for r0 in range(0, tm, CONV_ROWS):
            part = jnp.zeros((CONV_ROWS, CONV_LANES), _F32) + cb_ref[c]
            for t in range(CONV_K):
                res = (CONV_FIRST + t) % SUBLANES
                a = r0 + (CONV_FIRST + t) - res
                part = part + shift_ref[c, res, a:a + CONV_ROWS, :] * cw_ref[c, t:t + 1, :]
            acc_ref[c, r0:r0 + CONV_ROWS, :] = part
        ybuf_ref[c] = yc
        return carry

    for c in range(CONV_CHUNKS):
        chunk(c, 0)

    y = jnp.concatenate([ybuf_ref[c] for c in range(CONV_CHUNKS)], axis=1)
    gate = mod_ref[0:1, 2 * D_MODEL:3 * D_MODEL]
    o_ref[...] = x_ref[...] + gate * y

    acc = jnp.concatenate([acc_ref[c] for c in range(CONV_CHUNKS)], axis=1)
    mu = jnp.mean(acc, axis=-1, keepdims=True)
    cen = acc - mu
    var = jnp.mean(cen * cen, axis=-1, keepdims=True)
    ln = (cen * lax.rsqrt(var + EPS)) * lng_ref[...] + lnb_ref[...]
    conv_ref[i % 2] = (_silu(ln) * gc_ref[...].astype(_F32)).astype(conv_ref.dtype)


def _conv_out(yglu, cw, cb, lng, lnb, gc, att, w_out_bf, x2, mod):
    tm = OUT_TM
    n_tiles = SEQ // tm
    nb = tm // CONV_HALO
    last = SEQ // CONV_HALO - 1
    cur = lambda i: (jnp.minimum(i, n_tiles - 1), 0)
    prev = lambda i: (jnp.maximum(i - 1, 0), 0)
    fixed = lambda i: (0, 0)
    fixed3 = lambda i: (0, 0, 0)
    cw3 = cw.reshape(CONV_K, CONV_CHUNKS, CONV_LANES).transpose(1, 0, 2)
    cb3 = cb.reshape(CONV_CHUNKS, 1, CONV_LANES)
    w3 = w_out_bf.reshape(2, ATT_W, CONV_CHUNKS, OUT_LANES).transpose(0, 2, 1, 3)
    wa3, wc3 = w3[0], w3[1]
    return pl.pallas_call(
        _conv_out_kernel,
        grid=(n_tiles + 1,),
        in_specs=[pl.BlockSpec((tm, CONV_W), cur),
                  pl.BlockSpec((CONV_HALO, CONV_W),
                               lambda i: (jnp.maximum(jnp.minimum(i, n_tiles - 1) * nb - 1, 0), 0)),
                  pl.BlockSpec((CONV_HALO, CONV_W),
                               lambda i: (jnp.minimum((jnp.minimum(i, n_tiles - 1) + 1) * nb, last), 0)),
                  pl.BlockSpec((CONV_CHUNKS, CONV_K, CONV_LANES), fixed3),
                  pl.BlockSpec((CONV_CHUNKS, 1, CONV_LANES), fixed3),
                  pl.BlockSpec((1, CONV_W), fixed),
                  pl.BlockSpec((1, CONV_W), fixed),
                  pl.BlockSpec((tm, CONV_W), cur),
                  pl.BlockSpec((tm, ATT_W), prev),
                  pl.BlockSpec((CONV_CHUNKS, ATT_W, OUT_LANES), fixed3),
                  pl.BlockSpec((CONV_CHUNKS, CONV_W, OUT_LANES), fixed3),
                  pl.BlockSpec((tm, D_MODEL), prev),
                  pl.BlockSpec((MOD_ROWS, 3 * D_MODEL), fixed)],
        out_specs=pl.BlockSpec((tm, D_MODEL), prev),
        out_shape=jax.ShapeDtypeStruct((SEQ, D_MODEL), _F32),
        scratch_shapes=[pltpu.VMEM((CONV_CHUNKS, tm + 2 * CONV_HALO, CONV_LANES), _F32),
                        pltpu.VMEM((CONV_CHUNKS, tm, CONV_LANES), _F32),
                        pltpu.VMEM((CONV_CHUNKS, SUBLANES, tm + CONV_SPAN_EXTRA, CONV_LANES), _F32),
                        pltpu.VMEM((2, tm, CONV_W), _BF16),
                        pltpu.VMEM((CONV_CHUNKS, tm, OUT_LANES), _F32)],
        compiler_params=pltpu.CompilerParams(vmem_limit_bytes=VMEM_LIMIT),
        name="conv_out",
    )(yglu, yglu, yglu, cw3, cb3, lng, lnb, gc, att, wa3, wc3, x2, mod)


def _rope_tables():
    rows = SEQ // GRID_W
    row = np.repeat(np.arange(rows), GRID_W).astype(np.float64)
    col = np.tile(np.arange(GRID_W), rows).astype(np.float64)
    nf = HEAD_DIM // 4
    inv = ROPE_BASE ** (-np.arange(nf, dtype=np.float64) / nf)
    ang_r = row[:, None] * inv
    ang_c = col[:, None] * inv
    ang = np.concatenate([ang_r, ang_r, ang_c, ang_c], axis=-1)
    cos, sin = np.cos(ang), np.sin(ang)
    first_half = (np.arange(HEAD_DIM) % (2 * nf)) < nf
    sa = np.where(first_half, -sin, 0.0)
    sb = np.where(first_half, 0.0, sin)
    pad = lambda t, v: np.concatenate([t, np.full((CTX_LEN, HEAD_DIM), v)], axis=0)
    two = lambda t: jnp.asarray(np.concatenate([t, t], axis=1), _F32)
    return two(pad(cos, 1.0)), two(pad(sa, 0.0)), two(pad(sb, 0.0))


def _chunk_avg():
    r = np.arange(NORM_GROUP) // HEAD_DIM
    return jnp.asarray(np.where(r[:, None] == r[None, :], 1.0 / HEAD_DIM, 0.0), _BF16)


def _softmax_shift(q_norm_g, k_norm_g, q_scale):
    bound = (HEAD_DIM * q_scale * 1.01) * jnp.max(jnp.abs(q_norm_g)) * jnp.max(jnp.abs(k_norm_g))
    ok = (bound <= MAX_FIXED_SHIFT).astype(_F32)
    return jnp.stack([jnp.where(ok > 0.5, bound, 0.0), ok]).astype(_F32)


def kernel(x, c, ctx, c_ctx, w_ada, b_ada, norm_g, w_in, q_norm_g, k_norm_g, lam_q1, lam_k1,
           lam_q2, lam_k2, sub_norm_g, conv_w, conv_b, conv_ln_g, conv_ln_b, w_out):
    assert x.shape == (1, SEQ, D_MODEL) and ctx.shape == (1, CTX_LEN, D_MODEL)
    assert w_ada.shape[0] == 1 and w_in.shape == (1, D_MODEL, IN_W)
    x2 = x[0]
    ctx2 = ctx[0]

    cc = jnp.zeros((MOD_ROWS, D_MODEL), _F32).at[0].set(c[0]).at[1].set(c_ctx)
    mod = _adaln(cc, w_ada[0], b_ada)
    h = _modulate(x2, ctx2, mod, norm_g)

    w = w_in[0]
    w_out_bf = w_out[0].astype(_BF16)
    cos, sa, sb = _rope_tables()
    q_scale = LOG2E / math.sqrt(HEAD_DIM)
    avg = _chunk_avg()
    reps = PROJ_TN // HEAD_DIM
    gq = jnp.tile(q_norm_g, (1, reps)) * q_scale
    gk = jnp.tile(k_norm_g, (1, reps))

    def qk_extras(g, tabs, tm):
        fixed = lambda j, i: (0, 0)
        tab = lambda j, i: (i, 0)
        return ([(g, (1, PROJ_TN), fixed), (avg, (NORM_GROUP, NORM_GROUP), fixed)]
                + [(t, (tm, 2 * HEAD_DIM), tab) for t in tabs])

    q = _proj_call(_qk_kernel, h, [(w, Q0)], qk_extras(gq, (cos, sa, sb), PROJ_TM),
                   rows=SEQ, tm=PROJ_TM, tn=PROJ_TN, n_col_tiles=1, out_dtype=_BF16, name="proj_q")
    k = _proj_call(_qk_kernel, h, [(w, K0)], qk_extras(gk, (cos, sa, sb), PROJ_TM_ALL),
                   rows=L_ALL, tm=PROJ_TM_ALL, tn=PROJ_TN, n_col_tiles=1, out_dtype=_BF16, name="proj_k")
    vt = _proj_call(_vt_kernel, h, [(w, V0)], [], rows=L_ALL, tm=PROJ_TM_ALL, tn=PROJ_TN,
                    n_col_tiles=1, out_dtype=_BF16, name="proj_vt",
                    out_spec=pl.BlockSpec((HEADS, 1, V_DIM, PROJ_TM_ALL), lambda j, i: (0, i, 0, 0)),
                    out_shape=jax.ShapeDtypeStruct((HEADS, N_KB, V_DIM, PROJ_TM_ALL), _BF16))
    ga = _proj_call(_silu_kernel, h, [(w, GA0)], [], rows=SEQ, tm=PROJ_TM, tn=PROJ_TN,
                    n_col_tiles=1, out_dtype=_BF16, name="proj_ga")
    gc = _proj_call(_silu_kernel, h, [(w, GC0)], [], rows=SEQ, tm=PROJ_TM, tn=PROJ_TN,
                    n_col_tiles=1, out_dtype=_BF16, name="proj_gc")
    yglu = _proj_call(_glu_kernel, h, [(w, U0), (w, U0 + CONV_W)], [], rows=SEQ, tm=PROJ_TM,
                      tn=GLU_TN, n_col_tiles=CONV_W // GLU_TN, out_dtype=_BF16, name="proj_glu")

    lamv = jnp.concatenate([lam_q1, lam_k1, lam_q2, lam_k2], axis=0)
    shift = _softmax_shift(q_norm_g, k_norm_g, q_scale)
    att = _attention(shift, lamv, q, k, vt, ga, sub_norm_g)

    out = _conv_out(yglu, conv_w[0], conv_b, conv_ln_g, conv_ln_b, gc, att, w_out_bf, x2, mod)
    return out[None]
```

```python
import math

import jax
import jax.numpy as jnp
import numpy as np
from jax import lax
from jax.experimental import pallas as pl
from jax.experimental.pallas import tpu as pltpu

D_MODEL = 2048
SEQ = 8192
GRID_W = 64
CTX_LEN = 256
L_ALL = SEQ + CTX_LEN
HEADS = 8
HEAD_DIM = 64
V_DIM = 2 * HEAD_DIM
ATT_W = HEADS * V_DIM
CONV_W = D_MODEL - ATT_W
CONV_K = 31
CONV_HALO = 16
ROPE_BASE = 10000.0
EPS = 1e-6
LAM_INIT = 0.8 - 0.6 * math.exp(-0.3 * 0)
LOG2E = math.log2(math.e)

Q0 = 0
K0 = Q0 + HEADS * 2 * HEAD_DIM
V0 = K0 + HEADS * 2 * HEAD_DIM
GA0 = V0 + ATT_W
U0 = GA0 + ATT_W
GC0 = U0 + 2 * CONV_W
IN_W = GC0 + CONV_W

SUBLANES = 8
MOD_ROWS = SUBLANES
ADA_TN = 1024
MOD_TM = 256
PROJ_TM = 512
PROJ_TM_ALL = 768
PROJ_TN = 1024
GLU_TN = 512
ATT_TQ = 512
ATT_KB = PROJ_TM_ALL
N_KB = L_ALL // ATT_KB
CONV_TM = SEQ // (HEADS * (SEQ // ATT_TQ))
OUT_TM = 512
CONV_LANES = 128
CONV_CHUNKS = CONV_W // CONV_LANES
CONV_FIRST = CONV_HALO - CONV_K // 2
CONV_SPAN_EXTRA = SUBLANES * ((CONV_FIRST + CONV_K - 1) // SUBLANES)
NORM_GROUP = 256
VMEM_LIMIT = 48 * 1024 * 1024
MAX_FIXED_SHIFT = 40.0

_F32 = jnp.float32
_BF16 = jnp.bfloat16


def _sigmoid(x):
    return 1.0 / (1.0 + jnp.exp(-x))


def _silu(x):
    return x * _sigmoid(x)


def _adaln_kernel(cc_ref, w_ref, b_ref, o_ref):
    s = _silu(cc_ref[...]).astype(_BF16)
    o_ref[...] = jnp.dot(s, w_ref[...].astype(_BF16), preferred_element_type=_F32) + b_ref[...]


def _adaln(cc, w, b):
    n = w.shape[1]
    return pl.pallas_call(
        _adaln_kernel,
        grid=(n // ADA_TN,),
        in_specs=[pl.BlockSpec((MOD_ROWS, D_MODEL), lambda j: (0, 0)),
                  pl.BlockSpec((D_MODEL, ADA_TN), lambda j: (0, j)),
                  pl.BlockSpec((1, ADA_TN), lambda j: (0, j))],
        out_specs=pl.BlockSpec((MOD_ROWS, ADA_TN), lambda j: (0, j)),
        out_shape=jax.ShapeDtypeStruct((MOD_ROWS, n), _F32),
        compiler_params=pltpu.CompilerParams(vmem_limit_bytes=VMEM_LIMIT),
        name="adaln",
    )(cc, w, b)


def _modulate_rows(xin, g, shift, scale):
    r = lax.rsqrt(jnp.mean(xin * xin, axis=-1, keepdims=True) + EPS)
    return ((xin * r) * g) * (1.0 + scale) + shift


def _modulate_kernel(x_ref, ctx_ref, mod_ref, g_ref, h_ref):
    i = pl.program_id(0)
    n_lat = pl.num_programs(0) - 1
    g = g_ref[...]

    @pl.when(i < n_lat)
    def _():
        h_ref[...] = _modulate_rows(x_ref[...], g, mod_ref[0:1, 0:D_MODEL],
                                    mod_ref[0:1, D_MODEL:2 * D_MODEL]).astype(h_ref.dtype)

    @pl.when(i == n_lat)
    def _():
        h_ref[...] = _modulate_rows(ctx_ref[...], g, mod_ref[1:2, 0:D_MODEL],
                                    mod_ref[1:2, D_MODEL:2 * D_MODEL]).astype(h_ref.dtype)


def _modulate(x2, ctx2, mod, g):
    n_lat = SEQ // MOD_TM
    return pl.pallas_call(
        _modulate_kernel,
        grid=(n_lat + CTX_LEN // MOD_TM,),
        in_specs=[pl.BlockSpec((MOD_TM, D_MODEL), lambda i: (jnp.minimum(i, n_lat - 1), 0)),
                  pl.BlockSpec((CTX_LEN, D_MODEL), lambda i: (0, 0)),
                  pl.BlockSpec((MOD_ROWS, 3 * D_MODEL), lambda i: (0, 0)),
                  pl.BlockSpec((1, D_MODEL), lambda i: (0, 0))],
        out_specs=pl.BlockSpec((MOD_TM, D_MODEL), lambda i: (i, 0)),
        out_shape=jax.ShapeDtypeStruct((L_ALL, D_MODEL), _BF16),
        compiler_params=pltpu.CompilerParams(vmem_limit_bytes=VMEM_LIMIT),
        name="modulate",
    )(x2, ctx2, mod, g)


def _proj_call(kernel, h, w_list, extras, *, rows, tm, tn, n_col_tiles, out_dtype, name,
               out_spec=None, out_shape=None):
    in_specs = [pl.BlockSpec((tm, D_MODEL), lambda j, i: (i, 0))]
    args = [h]
    for w, c0 in w_list:
        in_specs.append(pl.BlockSpec((D_MODEL, tn), lambda j, i, c0=c0: (0, c0 // tn + j)))
        args.append(w)
    for a, bs, im in extras:
        in_specs.append(pl.BlockSpec(bs, im))
        args.append(a)
    if out_spec is None:
        out_spec = pl.BlockSpec((tm, tn), lambda j, i: (i, j))
        out_shape = jax.ShapeDtypeStruct((rows, n_col_tiles * tn), out_dtype)
    return pl.pallas_call(
        kernel,
        grid=(n_col_tiles, rows // tm),
        in_specs=in_specs,
        out_specs=out_spec,
        out_shape=out_shape,
        compiler_params=pltpu.CompilerParams(vmem_limit_bytes=VMEM_LIMIT),
        name=name,
    )(*args)


def _vt_kernel(h_ref, w_ref, o_ref):
    p = jnp.dot(h_ref[...], w_ref[...].astype(_BF16), preferred_element_type=_F32)
    pt = p.T.astype(o_ref.dtype)
    for hd in range(HEADS):
        o_ref[hd, 0] = pt[hd * V_DIM:(hd + 1) * V_DIM, :]


def _silu_kernel(h_ref, w_ref, o_ref):
    p = jnp.dot(h_ref[...], w_ref[...].astype(_BF16), preferred_element_type=_F32)
    o_ref[...] = _silu(p).astype(o_ref.dtype)


def _glu_kernel(h_ref, wa_ref, wg_ref, o_ref):
    h = h_ref[...]
    a = jnp.dot(h, wa_ref[...].astype(_BF16), preferred_element_type=_F32)
    g = jnp.dot(h, wg_ref[...].astype(_BF16), preferred_element_type=_F32)
    o_ref[...] = (a * _sigmoid(g)).astype(o_ref.dtype)


def _qk_kernel(h_ref, w_ref, g_ref, avg_ref, cos_ref, sa_ref, sb_ref, o_ref):
    p = jnp.dot(h_ref[...], w_ref[...].astype(_BF16), preferred_element_type=_F32)
    tn = p.shape[1]
    ss = p * p
    hi = ss.astype(_BF16)
    lo = (ss - hi.astype(_F32)).astype(_BF16)
    avg = avg_ref[...]
    ms = []
    for c in range(tn // NORM_GROUP):
        sl = slice(c * NORM_GROUP, (c + 1) * NORM_GROUP)
        ms.append(jnp.dot(hi[:, sl], avg, preferred_element_type=_F32)
                  + jnp.dot(lo[:, sl], avg, preferred_element_type=_F32))
    ms = jnp.concatenate(ms, axis=1)
    y = (p * lax.rsqrt(ms + EPS)) * g_ref[...]
    reps = tn // cos_ref.shape[1]
    cos = jnp.concatenate([cos_ref[...]] * reps, axis=1)
    sa = jnp.concatenate([sa_ref[...]] * reps, axis=1)
    sb = jnp.concatenate([sb_ref[...]] * reps, axis=1)
    quarter = HEAD_DIM // 4
    out = y * cos + pltpu.roll(y, tn - quarter, 1) * sa + pltpu.roll(y, quarter, 1) * sb
    o_ref[...] = out.astype(o_ref.dtype)


def _conv_chunk(c, tile, n_tiles, y_ref, yp_ref, yn_ref, cw_ref, cb_ref, win_ref, shift_ref):
    tm = y_ref.shape[0]
    halo = CONV_HALO
    span = shift_ref.shape[2]
    keep_prev = (tile > 0).astype(_F32)
    keep_next = (tile < n_tiles - 1).astype(_F32)
    cols = slice(c * CONV_LANES, (c + 1) * CONV_LANES)
    win_ref[c, 0:halo, :] = yp_ref[:, cols].astype(_F32) * keep_prev
    win_ref[c, halo:halo + tm, :] = y_ref[:, cols].astype(_F32)
    win_ref[c, halo + tm:, :] = yn_ref[:, cols].astype(_F32) * keep_next
    for res in range(SUBLANES):
        shift_ref[c, res] = win_ref[c, res:res + span, :]
    part = jnp.zeros((tm, CONV_LANES), _F32) + cb_ref[c]
    for t in range(CONV_K):
        res = (CONV_FIRST + t) % SUBLANES
        a = (CONV_FIRST + t) - res
        part = part + shift_ref[c, res, a:a + tm, :] * cw_ref[c, t:t + 1, :]
    return part


def _conv_tail(parts, lng_ref, lnb_ref, gc_ref):
    acc = jnp.concatenate(parts, axis=1)
    mu = jnp.mean(acc, axis=-1, keepdims=True)
    cen = acc - mu
    var = jnp.mean(cen * cen, axis=-1, keepdims=True)
    ln = (cen * lax.rsqrt(var + EPS)) * lng_ref[...] + lnb_ref[...]
    return (_silu(ln) * gc_ref[...].astype(_F32)).astype(_BF16)


def _attn_kernel(shift_ref, lamv_ref, q_ref, k_ref, vt_ref, ga_ref, subg_ref,
                 y_ref, yp_ref, yn_ref, cw_ref, cb_ref, lng_ref, lnb_ref, gc_ref,
                 o_ref, conv_o_ref, win_ref, cshift_ref):
    tq = q_ref.shape[0]
    qt = q_ref[...].astype(_F32).T
    row = lax.broadcasted_iota(jnp.int32, qt.shape, 0)
    qst = jnp.concatenate([jnp.where(row < HEAD_DIM, qt, 0.0),
                           jnp.where(row >= HEAD_DIM, qt, 0.0)], axis=1).astype(_BF16)
    groups = ATT_KB // SUBLANES
    conv_tile = pl.program_id(0) * pl.num_programs(1) + pl.program_id(1)
    n_conv_tiles = pl.num_programs(0) * pl.num_programs(1)

    def conv_chunk(c):
        return _conv_chunk(c, conv_tile, n_conv_tiles, y_ref, yp_ref, yn_ref, cw_ref, cb_ref,
                           win_ref, cshift_ref)

    def conv_tail(parts):
        return _conv_tail(parts, lng_ref, lnb_ref, gc_ref)

    def scores(kb):
        start = kb * ATT_KB if isinstance(kb, int) else pl.multiple_of(kb * ATT_KB, ATT_KB)
        return jnp.dot(k_ref[pl.ds(start, ATT_KB), :], qst, preferred_element_type=_F32)

    def finish(l, acc):
        o = acc / jnp.sum(l, axis=0, keepdims=True)
        lv = lamv_ref[...]
        lam = (jnp.exp(jnp.sum(lv[0:1] * lv[1:2], axis=1, keepdims=True))
               - jnp.exp(jnp.sum(lv[2:3] * lv[3:4], axis=1, keepdims=True)) + LAM_INIT)
        od = (o[:, :tq] - lam * o[:, tq:]).T
        r = lax.rsqrt(jnp.mean(od * od, axis=-1, keepdims=True) + EPS)
        on = ((od * r) * subg_ref[...]) * (1.0 - LAM_INIT)
        o_ref[...] = (on * ga_ref[...].astype(_F32)).astype(o_ref.dtype)

    l0 = jnp.zeros((SUBLANES, 2 * tq), _F32)
    a0 = jnp.zeros((V_DIM, 2 * tq), _F32)
    fixed_ok = shift_ref[1] > 0.5

    @pl.when(fixed_ok)
    def _():
        shift = jnp.full((1, 2 * tq), shift_ref[0], _F32)
        l, acc, parts, anchors = l0, a0, [], {}
        for kb in range(N_KB):
            sh = shift
            if kb in anchors:
                sh = shift + jnp.concatenate([anchors[kb]] * (2 * tq // CONV_LANES), axis=1)
            p = jnp.exp2(scores(kb) - sh)
            l = l + jnp.sum(p.reshape(groups, SUBLANES, 2 * tq), axis=0)
            acc = acc + jnp.dot(vt_ref[0, kb], p.astype(_BF16), preferred_element_type=_F32)
            if kb < CONV_CHUNKS:
                parts.append(conv_chunk(kb))
                anchors[kb + 1] = parts[-1][0:1, :] * 0.0
            elif kb == CONV_CHUNKS:
                conv = conv_tail(parts)
                conv_o_ref[...] = conv
                anchors[N_KB - 1] = conv[0:1, 0:CONV_LANES].astype(_F32) * 0.0
        finish(l, acc)

    @pl.when(jnp.logical_not(fixed_ok))
    def _():
        def body(kb, carry):
            m, l, acc = carry
            st = scores(kb)
            mb = jnp.max(jnp.max(st.reshape(groups, SUBLANES, 2 * tq), axis=0), axis=0, keepdims=True)
            m_new = jnp.maximum(m, mb)
            alpha = jnp.exp2(m - m_new)
            p = jnp.exp2(st - m_new)
            l = alpha * l + jnp.sum(p.reshape(groups, SUBLANES, 2 * tq), axis=0)
            acc = alpha * acc + jnp.dot(vt_ref[0, kb], p.astype(_BF16), preferred_element_type=_F32)
            return m_new, l, acc

        conv_o_ref[...] = conv_tail([conv_chunk(c) for c in range(CONV_CHUNKS)])
        m0 = jnp.full((1, 2 * tq), -1e30, _F32)
        _, l, acc = lax.fori_loop(0, N_KB, body, (m0, l0, a0))
        finish(l, acc)


def _attention(shift, lamv, q, k, vt, ga, subg, yglu, cw, cb, lng, lnb, gc):
    n_q = SEQ // ATT_TQ
    nb = CONV_TM // CONV_HALO
    last = SEQ // CONV_HALO - 1
    fixed = lambda h, i: (0, 0)
    fixed3 = lambda h, i: (0, 0, 0)
    qrow = lambda h, i: (i, h)
    ctile = lambda h, i: (h * n_q + i, 0)
    cw3 = cw.reshape(CONV_K, CONV_CHUNKS, CONV_LANES).transpose(1, 0, 2)
    cb3 = cb.reshape(CONV_CHUNKS, 1, CONV_LANES)
    return pl.pallas_call(
        _attn_kernel,
        grid=(HEADS, n_q),
        in_specs=[pl.BlockSpec(memory_space=pltpu.SMEM),
                  pl.BlockSpec(lamv.shape, fixed),
                  pl.BlockSpec((ATT_TQ, V_DIM), qrow),
                  pl.BlockSpec((L_ALL, V_DIM), lambda h, i: (0, h)),
                  pl.BlockSpec((1, N_KB, V_DIM, ATT_KB), lambda h, i: (h, 0, 0, 0)),
                  pl.BlockSpec((ATT_TQ, V_DIM), qrow),
                  pl.BlockSpec((1, V_DIM), fixed),
                  pl.BlockSpec((CONV_TM, CONV_W), ctile),
                  pl.BlockSpec((CONV_HALO, CONV_W),
                               lambda h, i: (jnp.maximum((h * n_q + i) * nb - 1, 0), 0)),
                  pl.BlockSpec((CONV_HALO, CONV_W),
                               lambda h, i: (jnp.minimum((h * n_q + i + 1) * nb, last), 0)),
                  pl.BlockSpec((CONV_CHUNKS, CONV_K, CONV_LANES), fixed3),
                  pl.BlockSpec((CONV_CHUNKS, 1, CONV_LANES), fixed3),
                  pl.BlockSpec((1, CONV_W), fixed),
                  pl.BlockSpec((1, CONV_W), fixed),
                  pl.BlockSpec((CONV_TM, CONV_W), ctile)],
        out_specs=[pl.BlockSpec((ATT_TQ, V_DIM), qrow),
                   pl.BlockSpec((CONV_TM, CONV_W), ctile)],
        out_shape=[jax.ShapeDtypeStruct((SEQ, ATT_W), _BF16),
                   jax.ShapeDtypeStruct((SEQ, CONV_W), _BF16)],
        scratch_shapes=[pltpu.VMEM((CONV_CHUNKS, CONV_TM + 2 * CONV_HALO, CONV_LANES), _F32),
                        pltpu.VMEM((CONV_CHUNKS, SUBLANES, CONV_TM + CONV_SPAN_EXTRA, CONV_LANES), _F32)],
        compiler_params=pltpu.CompilerParams(vmem_limit_bytes=VMEM_LIMIT),
        name="diff_attn_conv",
    )(shift, lamv, q, k, vt, ga, subg, yglu, yglu, yglu, cw3, cb3, lng, lnb, gc)


def _out_proj_kernel(att_ref, conv_ref, wa_ref, wc_ref, x_ref, mod_ref, o_ref):
    y = (jnp.dot(att_ref[...], wa_ref[...].astype(_BF16), preferred_element_type=_F32)
         + jnp.dot(conv_ref[...], wc_ref[...].astype(_BF16), preferred_element_type=_F32))
    gate = mod_ref[0:1, 2 * D_MODEL:3 * D_MODEL]
    o_ref[...] = x_ref[...] + gate * y


def _out_proj(att, conv, w_out, x2, mod):
    tm = OUT_TM
    row = lambda i: (i, 0)
    once = pl.Buffered(1)
    return pl.pallas_call(
        _out_proj_kernel,
        grid=(SEQ // tm,),
        in_specs=[pl.BlockSpec((tm, ATT_W), row),
                  pl.BlockSpec((tm, CONV_W), row),
                  pl.BlockSpec((ATT_W, D_MODEL), lambda i: (0, 0), pipeline_mode=once),
                  pl.BlockSpec((CONV_W, D_MODEL), lambda i: (1, 0), pipeline_mode=once),
                  pl.BlockSpec((tm, D_MODEL), row),
                  pl.BlockSpec((MOD_ROWS, 3 * D_MODEL), lambda i: (0, 0))],
        out_specs=pl.BlockSpec((tm, D_MODEL), row),
        out_shape=jax.ShapeDtypeStruct((SEQ, D_MODEL), _F32),
        compiler_params=pltpu.CompilerParams(vmem_limit_bytes=VMEM_LIMIT),
        name="out_proj",
    )(att, conv, w_out, w_out, x2, mod)


def _rope_tables():
    rows = SEQ // GRID_W
    row = np.repeat(np.arange(rows), GRID_W).astype(np.float64)
    col = np.tile(np.arange(GRID_W), rows).astype(np.float64)
    nf = HEAD_DIM // 4
    inv = ROPE_BASE ** (-np.arange(nf, dtype=np.float64) / nf)
    ang_r = row[:, None] * inv
    ang_c = col[:, None] * inv
    ang = np.concatenate([ang_r, ang_r, ang_c, ang_c], axis=-1)
    cos, sin = np.cos(ang), np.sin(ang)
    first_half = (np.arange(HEAD_DIM) % (2 * nf)) < nf
    sa = np.where(first_half, -sin, 0.0)
    sb = np.where(first_half, 0.0, sin)
    pad = lambda t, v: np.concatenate([t, np.full((CTX_LEN, HEAD_DIM), v)], axis=0)
    two = lambda t: jnp.asarray(np.concatenate([t, t], axis=1), _F32)
    return two(pad(cos, 1.0)), two(pad(sa, 0.0)), two(pad(sb, 0.0))


def _chunk_avg():
    r = np.arange(NORM_GROUP) // HEAD_DIM
    return jnp.asarray(np.where(r[:, None] == r[None, :], 1.0 / HEAD_DIM, 0.0), _BF16)


def _softmax_shift(q_norm_g, k_norm_g, q_scale):
    bound = (HEAD_DIM * q_scale * 1.01) * jnp.max(jnp.abs(q_norm_g)) * jnp.max(jnp.abs(k_norm_g))
    ok = (bound <= MAX_FIXED_SHIFT).astype(_F32)
    return jnp.stack([jnp.where(ok > 0.5, bound, 0.0), ok]).astype(_F32)


def kernel(x, c, ctx, c_ctx, w_ada, b_ada, norm_g, w_in, q_norm_g, k_norm_g, lam_q1, lam_k1,
           lam_q2, lam_k2, sub_norm_g, conv_w, conv_b, conv_ln_g, conv_ln_b, w_out):
    assert x.shape == (1, SEQ, D_MODEL) and ctx.shape == (1, CTX_LEN, D_MODEL)
    assert w_ada.shape[0] == 1 and w_in.shape == (1, D_MODEL, IN_W)
    x2 = x[0]
    ctx2 = ctx[0]

    cc = jnp.zeros((MOD_ROWS, D_MODEL), _F32).at[0].set(c[0]).at[1].set(c_ctx)
    mod = _adaln(cc, w_ada[0], b_ada)
    h = _modulate(x2, ctx2, mod, norm_g)

    w = w_in[0]
    cos, sa, sb = _rope_tables()
    q_scale = LOG2E / math.sqrt(HEAD_DIM)
    avg = _chunk_avg()
    reps = PROJ_TN // HEAD_DIM
    gq = jnp.tile(q_norm_g, (1, reps)) * q_scale
    gk = jnp.tile(k_norm_g, (1, reps))

    def qk_extras(g, tabs, tm):
        fixed = lambda j, i: (0, 0)
        tab = lambda j, i: (i, 0)
        return ([(g, (1, PROJ_TN), fixed), (avg, (NORM_GROUP, NORM_GROUP), fixed)]
                + [(t, (tm, 2 * HEAD_DIM), tab) for t in tabs])

    q = _proj_call(_qk_kernel, h, [(w, Q0)], qk_extras(gq, (cos, sa, sb), PROJ_TM),
                   rows=SEQ, tm=PROJ_TM, tn=PROJ_TN, n_col_tiles=1, out_dtype=_BF16, name="proj_q")
    k = _proj_call(_qk_kernel, h, [(w, K0)], qk_extras(gk, (cos, sa, sb), PROJ_TM_ALL),
                   rows=L_ALL, tm=PROJ_TM_ALL, tn=PROJ_TN, n_col_tiles=1, out_dtype=_BF16, name="proj_k")
    vt = _proj_call(_vt_kernel, h, [(w, V0)], [], rows=L_ALL, tm=PROJ_TM_ALL, tn=PROJ_TN,
                    n_col_tiles=1, out_dtype=_BF16, name="proj_vt",
                    out_spec=pl.BlockSpec((HEADS, 1, V_DIM, PROJ_TM_ALL), lambda j, i: (0, i, 0, 0)),
                    out_shape=jax.ShapeDtypeStruct((HEADS, N_KB, V_DIM, PROJ_TM_ALL), _BF16))
    ga = _proj_call(_silu_kernel, h, [(w, GA0)], [], rows=SEQ, tm=PROJ_TM, tn=PROJ_TN,
                    n_col_tiles=1, out_dtype=_BF16, name="proj_ga")
    gc = _proj_call(_silu_kernel, h, [(w, GC0)], [], rows=SEQ, tm=PROJ_TM, tn=PROJ_TN,
                    n_col_tiles=1, out_dtype=_BF16, name="proj_gc")
    yglu = _proj_call(_glu_kernel, h, [(w, U0), (w, U0 + CONV_W)], [], rows=SEQ, tm=PROJ_TM,
                      tn=GLU_TN, n_col_tiles=CONV_W // GLU_TN, out_dtype=_BF16, name="proj_glu")

    lamv = jnp.concatenate([lam_q1, lam_k1, lam_q2, lam_k2], axis=0)
    shift = _softmax_shift(q_norm_g, k_norm_g, q_scale)
    att, conv = _attention(shift, lamv, q, k, vt, ga, sub_norm_g,
                           yglu, conv_w[0], conv_b, conv_ln_g, conv_ln_b, gc)

    out = _out_proj(att, conv, w_out[0], x2, mod)
    return out[None]
```

```python
import math

import jax
import jax.numpy as jnp
import numpy as np
from jax import lax
from jax.experimental import pallas as pl
from jax.experimental.pallas import tpu as pltpu

D_MODEL = 2048
SEQ = 8192
GRID_W = 64
CTX_LEN = 256
L_ALL = SEQ + CTX_LEN
HEADS = 8
HEAD_DIM = 64
V_DIM = 2 * HEAD_DIM
ATT_W = HEADS * V_DIM
CONV_W = D_MODEL - ATT_W
CONV_K = 31
CONV_HALO = 16
ROPE_BASE = 10000.0
EPS = 1e-6
LAM_INIT = 0.8 - 0.6 * math.exp(-0.3 * 0)
LOG2E = math.log2(math.e)

Q0 = 0
K0 = Q0 + HEADS * 2 * HEAD_DIM
V0 = K0 + HEADS * 2 * HEAD_DIM
GA0 = V0 + ATT_W
U0 = GA0 + ATT_W
GC0 = U0 + 2 * CONV_W
IN_W = GC0 + CONV_W

SUBLANES = 8
MOD_ROWS = SUBLANES
ADA_TN = 1024
MOD_TM = 256
PROJ_TM = 512
PROJ_TM_ALL = 768
PROJ_TN = 1024
GLU_TN = 512
ATT_TQ = 512
ATT_KB = PROJ_TM_ALL
N_KB = L_ALL // ATT_KB
OUT_TM = 256
CONV_LANES = 128
CONV_CHUNKS = CONV_W // CONV_LANES
OUT_LANES = D_MODEL // CONV_CHUNKS
CONV_FIRST = CONV_HALO - CONV_K // 2
CONV_SPAN_EXTRA = SUBLANES * ((CONV_FIRST + CONV_K - 1) // SUBLANES)
NORM_GROUP = 256
VMEM_LIMIT = 48 * 1024 * 1024
CONV_OUT_VMEM_LIMIT = 58 * 1024 * 1024
MAX_FIXED_SHIFT = 40.0

_F32 = jnp.float32
_BF16 = jnp.bfloat16


def _sigmoid(x):
    return 1.0 / (1.0 + jnp.exp(-x))


def _silu(x):
    return x * _sigmoid(x)


def _adaln_kernel(cc_ref, w_ref, b_ref, o_ref):
    s = _silu(cc_ref[...]).astype(_BF16)
    o_ref[...] = jnp.dot(s, w_ref[...].astype(_BF16), preferred_element_type=_F32) + b_ref[...]


def _adaln(cc, w, b):
    n = w.shape[1]
    return pl.pallas_call(
        _adaln_kernel,
        grid=(n // ADA_TN,),
        in_specs=[pl.BlockSpec((MOD_ROWS, D_MODEL), lambda j: (0, 0)),
                  pl.BlockSpec((D_MODEL, ADA_TN), lambda j: (0, j)),
                  pl.BlockSpec((1, ADA_TN), lambda j: (0, j))],
        out_specs=pl.BlockSpec((MOD_ROWS, ADA_TN), lambda j: (0, j)),
        out_shape=jax.ShapeDtypeStruct((MOD_ROWS, n), _F32),
        compiler_params=pltpu.CompilerParams(vmem_limit_bytes=VMEM_LIMIT),
        name="adaln",
    )(cc, w, b)


def _modulate_rows(xin, g, shift, scale):
    r = lax.rsqrt(jnp.mean(xin * xin, axis=-1, keepdims=True) + EPS)
    return ((xin * r) * g) * (1.0 + scale) + shift


def _modulate_kernel(x_ref, ctx_ref, mod_ref, g_ref, h_ref):
    i = pl.program_id(0)
    n_lat = pl.num_programs(0) - 1
    g = g_ref[...]

    @pl.when(i < n_lat)
    def _():
        h_ref[...] = _modulate_rows(x_ref[...], g, mod_ref[0:1, 0:D_MODEL],
                                    mod_ref[0:1, D_MODEL:2 * D_MODEL]).astype(h_ref.dtype)

    @pl.when(i == n_lat)
    def _():
        h_ref[...] = _modulate_rows(ctx_ref[...], g, mod_ref[1:2, 0:D_MODEL],
                                    mod_ref[1:2, D_MODEL:2 * D_MODEL]).astype(h_ref.dtype)


def _modulate(x2, ctx2, mod, g):
    n_lat = SEQ // MOD_TM
    return pl.pallas_call(
        _modulate_kernel,
        grid=(n_lat + CTX_LEN // MOD_TM,),
        in_specs=[pl.BlockSpec((MOD_TM, D_MODEL), lambda i: (jnp.minimum(i, n_lat - 1), 0)),
                  pl.BlockSpec((CTX_LEN, D_MODEL), lambda i: (0, 0)),
                  pl.BlockSpec((MOD_ROWS, 3 * D_MODEL), lambda i: (0, 0)),
                  pl.BlockSpec((1, D_MODEL), lambda i: (0, 0))],
        out_specs=pl.BlockSpec((MOD_TM, D_MODEL), lambda i: (i, 0)),
        out_shape=jax.ShapeDtypeStruct((L_ALL, D_MODEL), _BF16),
        compiler_params=pltpu.CompilerParams(vmem_limit_bytes=VMEM_LIMIT),
        name="modulate",
    )(x2, ctx2, mod, g)


def _proj_call(kernel, h, w_list, extras, *, rows, tm, tn, n_col_tiles, out_dtype, name,
               out_spec=None, out_shape=None):
    in_specs = [pl.BlockSpec((tm, D_MODEL), lambda j, i: (i, 0))]
    args = [h]
    for w, c0 in w_list:
        in_specs.append(pl.BlockSpec((D_MODEL, tn), lambda j, i, c0=c0: (0, c0 // tn + j)))
        args.append(w)
    for a, bs, im in extras:
        in_specs.append(pl.BlockSpec(bs, im))
        args.append(a)
    if out_spec is None:
        out_spec = pl.BlockSpec((tm, tn), lambda j, i: (i, j))
        out_shape = jax.ShapeDtypeStruct((rows, n_col_tiles * tn), out_dtype)
    return pl.pallas_call(
        kernel,
        grid=(n_col_tiles, rows // tm),
        in_specs=in_specs,
        out_specs=out_spec,
        out_shape=out_shape,
        compiler_params=pltpu.CompilerParams(vmem_limit_bytes=VMEM_LIMIT),
        name=name,
    )(*args)


def _vt_kernel(h_ref, w_ref, o_ref):
    p = jnp.dot(h_ref[...], w_ref[...].astype(_BF16), preferred_element_type=_F32)
    pt = p.T.astype(o_ref.dtype)
    for hd in range(HEADS):
        o_ref[hd, 0] = pt[hd * V_DIM:(hd + 1) * V_DIM, :]


def _silu_kernel(h_ref, w_ref, o_ref):
    p = jnp.dot(h_ref[...], w_ref[...].astype(_BF16), preferred_element_type=_F32)
    o_ref[...] = _silu(p).astype(o_ref.dtype)


def _glu_kernel(h_ref, wa_ref, wg_ref, o_ref):
    h = h_ref[...]
    a = jnp.dot(h, wa_ref[...].astype(_BF16), preferred_element_type=_F32)
    g = jnp.dot(h, wg_ref[...].astype(_BF16), preferred_element_type=_F32)
    o_ref[...] = (a * _sigmoid(g)).astype(o_ref.dtype)


def _qk_kernel(h_ref, w_ref, g_ref, avg_ref, cos_ref, sa_ref, sb_ref, o_ref):
    p = jnp.dot(h_ref[...], w_ref[...].astype(_BF16), preferred_element_type=_F32)
    tn = p.shape[1]
    ss = p * p
    hi = ss.astype(_BF16)
    lo = (ss - hi.astype(_F32)).astype(_BF16)
    avg = avg_ref[...]
    ms = []
    for c in range(tn // NORM_GROUP):
        sl = slice(c * NORM_GROUP, (c + 1) * NORM_GROUP)
        ms.append(jnp.dot(hi[:, sl], avg, preferred_element_type=_F32)
                  + jnp.dot(lo[:, sl], avg, preferred_element_type=_F32))
    ms = jnp.concatenate(ms, axis=1)
    y = (p * lax.rsqrt(ms + EPS)) * g_ref[...]
    reps = tn // cos_ref.shape[1]
    cos = jnp.concatenate([cos_ref[...]] * reps, axis=1)
    sa = jnp.concatenate([sa_ref[...]] * reps, axis=1)
    sb = jnp.concatenate([sb_ref[...]] * reps, axis=1)
    quarter = HEAD_DIM // 4
    out = y * cos + pltpu.roll(y, tn - quarter, 1) * sa + pltpu.roll(y, quarter, 1) * sb
    o_ref[...] = out.astype(o_ref.dtype)


def _conv_chunk(c, tile, n_tiles, y_ref, yp_ref, yn_ref, cw_ref, cb_ref, win_ref, shift_ref):
    tm = y_ref.shape[0]
    halo = CONV_HALO
    span = shift_ref.shape[2]
    keep_prev = (tile > 0).astype(_F32)
    keep_next = (tile < n_tiles - 1).astype(_F32)
    cols = slice(c * CONV_LANES, (c + 1) * CONV_LANES)
    win_ref[c, 0:halo, :] = yp_ref[:, cols].astype(_F32) * keep_prev
    win_ref[c, halo:halo + tm, :] = y_ref[:, cols].astype(_F32)
    win_ref[c, halo + tm:, :] = yn_ref[:, cols].astype(_F32) * keep_next
    for res in range(SUBLANES):
        shift_ref[c, res] = win_ref[c, res:res + span, :]
    part = jnp.zeros((tm, CONV_LANES), _F32) + cb_ref[c]
    for t in range(CONV_K):
        res = (CONV_FIRST + t) % SUBLANES
        a = (CONV_FIRST + t) - res
        part = part + shift_ref[c, res, a:a + tm, :] * cw_ref[c, t:t + 1, :]
    return part


def _conv_tail(parts, lng_ref, lnb_ref, gc_ref):
    acc = jnp.concatenate(parts, axis=1)
    mu = jnp.mean(acc, axis=-1, keepdims=True)
    cen = acc - mu
    var = jnp.mean(cen * cen, axis=-1, keepdims=True)
    ln = (cen * lax.rsqrt(var + EPS)) * lng_ref[...] + lnb_ref[...]
    return (_silu(ln) * gc_ref[...].astype(_F32)).astype(_BF16)


def _attn_kernel(shift_ref, lamv_ref, q_ref, k_ref, vt_ref, ga_ref, subg_ref, o_ref):
    tq = q_ref.shape[0]
    qt = q_ref[...].astype(_F32).T
    row = lax.broadcasted_iota(jnp.int32, qt.shape, 0)
    qst = jnp.concatenate([jnp.where(row < HEAD_DIM, qt, 0.0),
                           jnp.where(row >= HEAD_DIM, qt, 0.0)], axis=1).astype(_BF16)
    groups = ATT_KB // SUBLANES

    def scores(kb):
        start = kb * ATT_KB if isinstance(kb, int) else pl.multiple_of(kb * ATT_KB, ATT_KB)
        return jnp.dot(k_ref[pl.ds(start, ATT_KB), :], qst, preferred_element_type=_F32)

    def finish(l, acc):
        o = acc / jnp.sum(l, axis=0, keepdims=True)
        lv = lamv_ref[...]
        lam = (jnp.exp(jnp.sum(lv[0:1] * lv[1:2], axis=1, keepdims=True))
               - jnp.exp(jnp.sum(lv[2:3] * lv[3:4], axis=1, keepdims=True)) + LAM_INIT)
        od = (o[:, :tq] - lam * o[:, tq:]).T
        r = lax.rsqrt(jnp.mean(od * od, axis=-1, keepdims=True) + EPS)
        on = ((od * r) * subg_ref[...]) * (1.0 - LAM_INIT)
        o_ref[...] = (on * ga_ref[...].astype(_F32)).astype(o_ref.dtype)

    l0 = jnp.zeros((SUBLANES, 2 * tq), _F32)
    a0 = jnp.zeros((V_DIM, 2 * tq), _F32)
    fixed_ok = shift_ref[1] > 0.5

    @pl.when(fixed_ok)
    def _():
        shift = shift_ref[0]
        l, acc = l0, a0
        for kb in range(N_KB):
            p = jnp.exp2(scores(kb) - shift)
            l = l + jnp.sum(p.reshape(groups, SUBLANES, 2 * tq), axis=0)
            acc = acc + jnp.dot(vt_ref[0, kb], p.astype(_BF16), preferred_element_type=_F32)
        finish(l, acc)

    @pl.when(jnp.logical_not(fixed_ok))
    def _():
        def body(kb, carry):
            m, l, acc = carry
            st = scores(kb)
            mb = jnp.max(jnp.max(st.reshape(groups, SUBLANES, 2 * tq), axis=0), axis=0, keepdims=True)
            m_new = jnp.maximum(m, mb)
            alpha = jnp.exp2(m - m_new)
            p = jnp.exp2(st - m_new)
            l = alpha * l + jnp.sum(p.reshape(groups, SUBLANES, 2 * tq), axis=0)
            acc = alpha * acc + jnp.dot(vt_ref[0, kb], p.astype(_BF16), preferred_element_type=_F32)
            return m_new, l, acc

        m0 = jnp.full((1, 2 * tq), -1e30, _F32)
        _, l, acc = lax.fori_loop(0, N_KB, body, (m0, l0, a0))
        finish(l, acc)


def _attention(shift, lamv, q, k, vt, ga, subg):
    qrow = lambda h, i: (i, h)
    return pl.pallas_call(
        _attn_kernel,
        grid=(HEADS, SEQ // ATT_TQ),
        in_specs=[pl.BlockSpec(memory_space=pltpu.SMEM),
                  pl.BlockSpec(lamv.shape, lambda h, i: (0, 0)),
                  pl.BlockSpec((ATT_TQ, V_DIM), qrow),
                  pl.BlockSpec((L_ALL, V_DIM), lambda h, i: (0, h)),
                  pl.BlockSpec((1, N_KB, V_DIM, ATT_KB), lambda h, i: (h, 0, 0, 0)),
                  pl.BlockSpec((ATT_TQ, V_DIM), qrow),
                  pl.BlockSpec((1, V_DIM), lambda h, i: (0, 0))],
        out_specs=pl.BlockSpec((ATT_TQ, V_DIM), qrow),
        out_shape=jax.ShapeDtypeStruct((SEQ, ATT_W), _BF16),
        compiler_params=pltpu.CompilerParams(vmem_limit_bytes=VMEM_LIMIT),
        name="diff_attn",
    )(shift, lamv, q, k, vt, ga, subg)


def _conv_out_kernel(y_ref, yp_ref, yn_ref, cw_ref, cb_ref, lng_ref, lnb_ref, gc_ref, att_ref,
                     w_ref, x_ref, mod_ref, o_ref, win_ref, shift_ref, conv_ref, wbf_ref):
    i = pl.program_id(0)
    n_tiles = pl.num_programs(0) - 1
    tile = jnp.minimum(i, n_tiles - 1)

    @pl.when(i == 0)
    def _():
        conv_ref[1] = jnp.zeros(conv_ref.shape[1:], conv_ref.dtype)
        wbf_ref[...] = w_ref[...].astype(wbf_ref.dtype)

    att = att_ref[...]
    conv_prev = conv_ref[(i + 1) % 2]
    ys, parts = [], []
    for c in range(CONV_CHUNKS):
        cols = slice(c * OUT_LANES, (c + 1) * OUT_LANES)
        ys.append(jnp.dot(att, wbf_ref[0:ATT_W, cols], preferred_element_type=_F32)
                  + jnp.dot(conv_prev, wbf_ref[ATT_W:, cols], preferred_element_type=_F32))
        parts.append(_conv_chunk(c, tile, n_tiles, y_ref, yp_ref, yn_ref, cw_ref, cb_ref,
                                 win_ref, shift_ref))
    gate = mod_ref[0:1, 2 * D_MODEL:3 * D_MODEL]
    o_ref[...] = x_ref[...] + gate * jnp.concatenate(ys, axis=1)
    conv_ref[i % 2] = _conv_tail(parts, lng_ref, lnb_ref, gc_ref)


def _conv_out(yglu, cw, cb, lng, lnb, gc, att, w_out, x2, mod):
    tm = OUT_TM
    n_tiles = SEQ // tm
    nb = tm // CONV_HALO
    last = SEQ // CONV_HALO - 1
    cur = lambda i: (jnp.minimum(i, n_tiles - 1), 0)
    prev = lambda i: (jnp.maximum(i - 1, 0), 0)
    fixed = lambda i: (0, 0)
    fixed3 = lambda i: (0, 0, 0)
    cw3 = cw.reshape(CONV_K, CONV_CHUNKS, CONV_LANES).transpose(1, 0, 2)
    cb3 = cb.reshape(CONV_CHUNKS, 1, CONV_LANES)
    return pl.pallas_call(
        _conv_out_kernel,
        grid=(n_tiles + 1,),
        in_specs=[pl.BlockSpec((tm, CONV_W), cur),
                  pl.BlockSpec((CONV_HALO, CONV_W),
                               lambda i: (jnp.maximum(jnp.minimum(i, n_tiles - 1) * nb - 1, 0), 0)),
                  pl.BlockSpec((CONV_HALO, CONV_W),
                               lambda i: (jnp.minimum((jnp.minimum(i, n_tiles - 1) + 1) * nb, last), 0)),
                  pl.BlockSpec((CONV_CHUNKS, CONV_K, CONV_LANES), fixed3),
                  pl.BlockSpec((CONV_CHUNKS, 1, CONV_LANES), fixed3),
                  pl.BlockSpec((1, CONV_W), fixed),
                  pl.BlockSpec((1, CONV_W), fixed),
                  pl.BlockSpec((tm, CONV_W), cur),
                  pl.BlockSpec((tm, ATT_W), prev),
                  pl.BlockSpec((D_MODEL, D_MODEL), fixed, pipeline_mode=pl.Buffered(1)),
                  pl.BlockSpec((tm, D_MODEL), prev),
                  pl.BlockSpec((MOD_ROWS, 3 * D_MODEL), fixed)],
        out_specs=pl.BlockSpec((tm, D_MODEL), prev),
        out_shape=jax.ShapeDtypeStruct((SEQ, D_MODEL), _F32),
        scratch_shapes=[pltpu.VMEM((CONV_CHUNKS, tm + 2 * CONV_HALO, CONV_LANES), _F32),
                        pltpu.VMEM((CONV_CHUNKS, SUBLANES, tm + CONV_SPAN_EXTRA, CONV_LANES), _F32),
                        pltpu.VMEM((2, tm, CONV_W), _BF16),
                        pltpu.VMEM((D_MODEL, D_MODEL), _BF16)],
        compiler_params=pltpu.CompilerParams(vmem_limit_bytes=CONV_OUT_VMEM_LIMIT),
        name="conv_out",
    )(yglu, yglu, yglu, cw3, cb3, lng, lnb, gc, att, w_out, x2, mod)


def _rope_tables():
    rows = SEQ // GRID_W
    row = np.repeat(np.arange(rows), GRID_W).astype(np.float64)
    col = np.tile(np.arange(GRID_W), rows).astype(np.float64)
    nf = HEAD_DIM // 4
    inv = ROPE_BASE ** (-np.arange(nf, dtype=np.float64) / nf)
    ang_r = row[:, None] * inv
    ang_c = col[:, None] * inv
    ang = np.concatenate([ang_r, ang_r, ang_c, ang_c], axis=-1)
    cos, sin = np.cos(ang), np.sin(ang)
    first_half = (np.arange(HEAD_DIM) % (2 * nf)) < nf
    sa = np.where(first_half, -sin, 0.0)
    sb = np.where(first_half, 0.0, sin)
    pad = lambda t, v: np.concatenate([t, np.full((CTX_LEN, HEAD_DIM), v)], axis=0)
    two = lambda t: jnp.asarray(np.concatenate([t, t], axis=1), _F32)
    return two(pad(cos, 1.0)), two(pad(sa, 0.0)), two(pad(sb, 0.0))


def _chunk_avg():
    r = np.arange(NORM_GROUP) // HEAD_DIM
    return jnp.asarray(np.where(r[:, None] == r[None, :], 1.0 / HEAD_DIM, 0.0), _BF16)


def _softmax_shift(q_norm_g, k_norm_g, q_scale):
    bound = (HEAD_DIM * q_scale * 1.01) * jnp.max(jnp.abs(q_norm_g)) * jnp.max(jnp.abs(k_norm_g))
    ok = (bound <= MAX_FIXED_SHIFT).astype(_F32)
    return jnp.stack([jnp.where(ok > 0.5, bound, 0.0), ok]).astype(_F32)


def kernel(x, c, ctx, c_ctx, w_ada, b_ada, norm_g, w_in, q_norm_g, k_norm_g, lam_q1, lam_k1,
           lam_q2, lam_k2, sub_norm_g, conv_w, conv_b, conv_ln_g, conv_ln_b, w_out):
    assert x.shape == (1, SEQ, D_MODEL) and ctx.shape == (1, CTX_LEN, D_MODEL)
    assert w_ada.shape[0] == 1 and w_in.shape == (1, D_MODEL, IN_W)
    x2 = x[0]
    ctx2 = ctx[0]

    cc = jnp.zeros((MOD_ROWS, D_MODEL), _F32).at[0].set(c[0]).at[1].set(c_ctx)
    mod = _adaln(cc, w_ada[0], b_ada)
    h = _modulate(x2, ctx2, mod, norm_g)

    w = w_in[0]
    cos, sa, sb = _rope_tables()
    q_scale = LOG2E / math.sqrt(HEAD_DIM)
    avg = _chunk_avg()
    reps = PROJ_TN // HEAD_DIM
    gq = jnp.tile(q_norm_g, (1, reps)) * q_scale
    gk = jnp.tile(k_norm_g, (1, reps))

    def qk_extras(g, tabs, tm):
        fixed = lambda j, i: (0, 0)
        tab = lambda j, i: (i, 0)
        return ([(g, (1, PROJ_TN), fixed), (avg, (NORM_GROUP, NORM_GROUP), fixed)]
                + [(t, (tm, 2 * HEAD_DIM), tab) for t in tabs])

    q = _proj_call(_qk_kernel, h, [(w, Q0)], qk_extras(gq, (cos, sa, sb), PROJ_TM),
                   rows=SEQ, tm=PROJ_TM, tn=PROJ_TN, n_col_tiles=1, out_dtype=_BF16, name="proj_q")
    k = _proj_call(_qk_kernel, h, [(w, K0)], qk_extras(gk, (cos, sa, sb), PROJ_TM_ALL),
                   rows=L_ALL, tm=PROJ_TM_ALL, tn=PROJ_TN, n_col_tiles=1, out_dtype=_BF16, name="proj_k")
    vt = _proj_call(_vt_kernel, h, [(w, V0)], [], rows=L_ALL, tm=PROJ_TM_ALL, tn=PROJ_TN,
                    n_col_tiles=1, out_dtype=_BF16, name="proj_vt",
                    out_spec=pl.BlockSpec((HEADS, 1, V_DIM, PROJ_TM_ALL), lambda j, i: (0, i, 0, 0)),
                    out_shape=jax.ShapeDtypeStruct((HEADS, N_KB, V_DIM, PROJ_TM_ALL), _BF16))
    ga = _proj_call(_silu_kernel, h, [(w, GA0)], [], rows=SEQ, tm=PROJ_TM, tn=PROJ_TN,
                    n_col_tiles=1, out_dtype=_BF16, name="proj_ga")
    gc = _proj_call(_silu_kernel, h, [(w, GC0)], [], rows=SEQ, tm=PROJ_TM, tn=PROJ_TN,
                    n_col_tiles=1, out_dtype=_BF16, name="proj_gc")
    yglu = _proj_call(_glu_kernel, h, [(w, U0), (w, U0 + CONV_W)], [], rows=SEQ, tm=PROJ_TM,
                      tn=GLU_TN, n_col_tiles=CONV_W // GLU_TN, out_dtype=_BF16, name="proj_glu")

    lamv = jnp.concatenate([lam_q1, lam_k1, lam_q2, lam_k2], axis=0)
    shift = _softmax_shift(q_norm_g, k_norm_g, q_scale)
    att = _attention(shift, lamv, q, k, vt, ga, sub_norm_g)

    out = _conv_out(yglu, conv_w[0], conv_b, conv_ln_g, conv_ln_b, gc, att, w_out[0], x2, mod)
    return out[None]
```

```python
import math

import jax
import jax.numpy as jnp
import numpy as np
from jax import lax
from jax.experimental import pallas as pl
from jax.experimental.pallas import tpu as pltpu

D_MODEL = 2048
SEQ = 8192
GRID_W = 64
CTX_LEN = 256
L_ALL = SEQ + CTX_LEN
HEADS = 8
HEAD_DIM = 64
V_DIM = 2 * HEAD_DIM
ATT_W = HEADS * V_DIM
CONV_W = D_MODEL - ATT_W
CONV_K = 31
CONV_HALO = 16
ROPE_BASE = 10000.0
EPS = 1e-6
LAM_INIT = 0.8 - 0.6 * math.exp(-0.3 * 0)
LOG2E = math.log2(math.e)

Q0 = 0
K0 = Q0 + HEADS * 2 * HEAD_DIM
V0 = K0 + HEADS * 2 * HEAD_DIM
GA0 = V0 + ATT_W
U0 = GA0 + ATT_W
GC0 = U0 + 2 * CONV_W
IN_W = GC0 + CONV_W

SUBLANES = 8
MOD_ROWS = SUBLANES
ADA_TN = 1024
MOD_TM = 256
MOD_ROWS_PER_PASS = 64
PROJ_TM = 1024
PROJ_TM_ALL = 768
PROJ_TN = 1024
GLU_TN = 512
ATT_TQ = 512
ATT_KB = PROJ_TM_ALL
N_KB = L_ALL // ATT_KB
OUT_TM = 256
CONV_LANES = 128
CONV_CHUNKS = CONV_W // CONV_LANES
OUT_LANES = D_MODEL // CONV_CHUNKS
CONV_FIRST = CONV_HALO - CONV_K // 2
CONV_SPAN_EXTRA = SUBLANES * ((CONV_FIRST + CONV_K - 1) // SUBLANES)
NORM_GROUP = 256
VMEM_LIMIT = 48 * 1024 * 1024
CONV_OUT_VMEM_LIMIT = 58 * 1024 * 1024
MAX_FIXED_SHIFT = 40.0

_F32 = jnp.float32
_BF16 = jnp.bfloat16


def _sigmoid(x):
    return 1.0 / (1.0 + jnp.exp(-x))


def _silu(x):
    return x * _sigmoid(x)


def _adaln_kernel(cc_ref, w_ref, b_ref, o_ref):
    s = _silu(cc_ref[...]).astype(_BF16)
    o_ref[...] = jnp.dot(s, w_ref[...].astype(_BF16), preferred_element_type=_F32) + b_ref[...]


def _adaln(cc, w, b):
    n = w.shape[1]
    return pl.pallas_call(
        _adaln_kernel,
        grid=(n // ADA_TN,),
        in_specs=[pl.BlockSpec((MOD_ROWS, D_MODEL), lambda j: (0, 0)),
                  pl.BlockSpec((D_MODEL, ADA_TN), lambda j: (0, j)),
                  pl.BlockSpec((1, ADA_TN), lambda j: (0, j))],
        out_specs=pl.BlockSpec((MOD_ROWS, ADA_TN), lambda j: (0, j)),
        out_shape=jax.ShapeDtypeStruct((MOD_ROWS, n), _F32),
        compiler_params=pltpu.CompilerParams(vmem_limit_bytes=VMEM_LIMIT),
        name="adaln",
    )(cc, w, b)


def _modulate_rows(xin, g, shift, scale):
    r = lax.rsqrt(jnp.mean(xin * xin, axis=-1, keepdims=True) + EPS)
    return ((xin * r) * g) * (1.0 + scale) + shift


def _modulate_kernel(x_ref, ctx_ref, mod_ref, g_ref, h_ref):
    i = pl.program_id(0)
    n_lat = pl.num_programs(0) - 1
    g = g_ref[...]

    def run(src_ref, mod_row):
        shift = mod_ref[mod_row:mod_row + 1, 0:D_MODEL]
        scale = mod_ref[mod_row:mod_row + 1, D_MODEL:2 * D_MODEL]

        def rows(r, carry):
            sl = pl.ds(pl.multiple_of(r * MOD_ROWS_PER_PASS, MOD_ROWS_PER_PASS), MOD_ROWS_PER_PASS)
            h_ref[sl, :] = _modulate_rows(src_ref[sl, :], g, shift, scale).astype(h_ref.dtype)
            return carry

        lax.fori_loop(0, MOD_TM // MOD_ROWS_PER_PASS, rows, 0)

    @pl.when(i < n_lat)
    def _():
        run(x_ref, 0)

    @pl.when(i == n_lat)
    def _():
        run(ctx_ref, 1)


def _modulate(x2, ctx2, mod, g):
    n_lat = SEQ // MOD_TM
    return pl.pallas_call(
        _modulate_kernel,
        grid=(n_lat + CTX_LEN // MOD_TM,),
        in_specs=[pl.BlockSpec((MOD_TM, D_MODEL), lambda i: (jnp.minimum(i, n_lat - 1), 0)),
                  pl.BlockSpec((CTX_LEN, D_MODEL), lambda i: (0, 0)),
                  pl.BlockSpec((MOD_ROWS, 3 * D_MODEL), lambda i: (0, 0)),
                  pl.BlockSpec((1, D_MODEL), lambda i: (0, 0))],
        out_specs=pl.BlockSpec((MOD_TM, D_MODEL), lambda i: (i, 0)),
        out_shape=jax.ShapeDtypeStruct((L_ALL, D_MODEL), _BF16),
        compiler_params=pltpu.CompilerParams(vmem_limit_bytes=VMEM_LIMIT),
        name="modulate",
    )(x2, ctx2, mod, g)


def _proj_call(kernel, h, w_list, extras, *, rows, tm, tn, n_col_tiles, out_dtype, name,
               out_spec=None, out_shape=None):
    in_specs = [pl.BlockSpec((tm, D_MODEL), lambda j, i: (i, 0))]
    args = [h]
    for w, c0 in w_list:
        in_specs.append(pl.BlockSpec((D_MODEL, tn), lambda j, i, c0=c0: (0, c0 // tn + j)))
        args.append(w)
    for a, bs, im in extras:
        in_specs.append(pl.BlockSpec(bs, im))
        args.append(a)
    if out_spec is None:
        out_spec = pl.BlockSpec((tm, tn), lambda j, i: (i, j))
        out_shape = jax.ShapeDtypeStruct((rows, n_col_tiles * tn), out_dtype)
    return pl.pallas_call(
        kernel,
        grid=(n_col_tiles, rows // tm),
        in_specs=in_specs,
        out_specs=out_spec,
        out_shape=out_shape,
        compiler_params=pltpu.CompilerParams(vmem_limit_bytes=VMEM_LIMIT),
        name=name,
    )(*args)


def _vt_kernel(h_ref, w_ref, o_ref):
    p = jnp.dot(h_ref[...], w_ref[...].astype(_BF16), preferred_element_type=_F32)
    pt = p.T.astype(o_ref.dtype)
    for hd in range(HEADS):
        o_ref[hd, 0] = pt[hd * V_DIM:(hd + 1) * V_DIM, :]


def _silu_kernel(h_ref, w_ref, o_ref):
    p = jnp.dot(h_ref[...], w_ref[...].astype(_BF16), preferred_element_type=_F32)
    o_ref[...] = _silu(p).astype(o_ref.dtype)


def _glu_kernel(h_ref, wa_ref, wg_ref, o_ref):
    h = h_ref[...]
    a = jnp.dot(h, wa_ref[...].astype(_BF16), preferred_element_type=_F32)
    g = jnp.dot(h, wg_ref[...].astype(_BF16), preferred_element_type=_F32)
    o_ref[...] = (a * _sigmoid(g)).astype(o_ref.dtype)


def _qk_kernel(h_ref, w_ref, g_ref, avg_ref, cos_ref, sa_ref, sb_ref, o_ref):
    p = jnp.dot(h_ref[...], w_ref[...].astype(_BF16), preferred_element_type=_F32)
    tn = p.shape[1]
    ss = p * p
    hi = ss.astype(_BF16)
    lo = (ss - hi.astype(_F32)).astype(_BF16)
    avg = avg_ref[...]
    ms = []
    for c in range(tn // NORM_GROUP):
        sl = slice(c * NORM_GROUP, (c + 1) * NORM_GROUP)
        ms.append(jnp.dot(hi[:, sl], avg, preferred_element_type=_F32)
                  + jnp.dot(lo[:, sl], avg, preferred_element_type=_F32))
    ms = jnp.concatenate(ms, axis=1)
    y = (p * lax.rsqrt(ms + EPS)) * g_ref[...]
    reps = tn // cos_ref.shape[1]
    cos = jnp.concatenate([cos_ref[...]] * reps, axis=1)
    sa = jnp.concatenate([sa_ref[...]] * reps, axis=1)
    sb = jnp.concatenate([sb_ref[...]] * reps, axis=1)
    quarter = HEAD_DIM // 4
    out = y * cos + pltpu.roll(y, tn - quarter, 1) * sa + pltpu.roll(y, quarter, 1) * sb
    o_ref[...] = out.astype(o_ref.dtype)


def _conv_chunk(c, tile, n_tiles, y_ref, yp_ref, yn_ref, cw_ref, cb_ref, win_ref, shift_ref):
    tm = y_ref.shape[0]
    halo = CONV_HALO
    span = shift_ref.shape[2]
    keep_prev = (tile > 0).astype(_F32)
    keep_next = (tile < n_tiles - 1).astype(_F32)
    cols = slice(c * CONV_LANES, (c + 1) * CONV_LANES)
    win_ref[c, 0:halo, :] = yp_ref[:, cols].astype(_F32) * keep_prev
    win_ref[c, halo:halo + tm, :] = y_ref[:, cols].astype(_F32)
    win_ref[c, halo + tm:, :] = yn_ref[:, cols].astype(_F32) * keep_next
    for res in range(SUBLANES):
        shift_ref[c, res] = win_ref[c, res:res + span, :]
    part = jnp.zeros((tm, CONV_LANES), _F32) + cb_ref[c]
    for t in range(CONV_K):
        res = (CONV_FIRST + t) % SUBLANES
        a = (CONV_FIRST + t) - res
        part = part + shift_ref[c, res, a:a + tm, :] * cw_ref[c, t:t + 1, :]
    return part


def _conv_tail(parts, lng_ref, lnb_ref, gc_ref):
    acc = jnp.concatenate(parts, axis=1)
    mu = jnp.mean(acc, axis=-1, keepdims=True)
    cen = acc - mu
    var = jnp.mean(cen * cen, axis=-1, keepdims=True)
    ln = (cen * lax.rsqrt(var + EPS)) * lng_ref[...] + lnb_ref[...]
    return (_silu(ln) * gc_ref[...].astype(_F32)).astype(_BF16)


def _attn_kernel(shift_ref, lamv_ref, q_ref, k_ref, vt_ref, ga_ref, subg_ref, o_ref):
    tq = q_ref.shape[0]
    qt = q_ref[...].astype(_F32).T
    row = lax.broadcasted_iota(jnp.int32, qt.shape, 0)
    qst = jnp.concatenate([jnp.where(row < HEAD_DIM, qt, 0.0),
                           jnp.where(row >= HEAD_DIM, qt, 0.0)], axis=1).astype(_BF16)
    groups = ATT_KB // SUBLANES

    def scores(kb):
        start = kb * ATT_KB if isinstance(kb, int) else pl.multiple_of(kb * ATT_KB, ATT_KB)
        return jnp.dot(k_ref[pl.ds(start, ATT_KB), :], qst, preferred_element_type=_F32)

    def finish(l, acc):
        o = acc / jnp.sum(l, axis=0, keepdims=True)
        lv = lamv_ref[...]
        lam = (jnp.exp(jnp.sum(lv[0:1] * lv[1:2], axis=1, keepdims=True))
               - jnp.exp(jnp.sum(lv[2:3] * lv[3:4], axis=1, keepdims=True)) + LAM_INIT)
        od = (o[:, :tq] - lam * o[:, tq:]).T
        r = lax.rsqrt(jnp.mean(od * od, axis=-1, keepdims=True) + EPS)
        on = ((od * r) * subg_ref[...]) * (1.0 - LAM_INIT)
        o_ref[...] = (on * ga_ref[...].astype(_F32)).astype(o_ref.dtype)

    l0 = jnp.zeros((SUBLANES, 2 * tq), _F32)
    a0 = jnp.zeros((V_DIM, 2 * tq), _F32)
    fixed_ok = shift_ref[1] > 0.5

    @pl.when(fixed_ok)
    def _():
        shift = shift_ref[0]
        l, acc = l0, a0
        for kb in range(N_KB):
            p = jnp.exp2(scores(kb) - shift)
            l = l + jnp.sum(p.reshape(groups, SUBLANES, 2 * tq), axis=0)
            acc = acc + jnp.dot(vt_ref[0, kb], p.astype(_BF16), preferred_element_type=_F32)
        finish(l, acc)

    @pl.when(jnp.logical_not(fixed_ok))
    def _():
        def body(kb, carry):
            m, l, acc = carry
            st = scores(kb)
            mb = jnp.max(jnp.max(st.reshape(groups, SUBLANES, 2 * tq), axis=0), axis=0, keepdims=True)
            m_new = jnp.maximum(m, mb)
            alpha = jnp.exp2(m - m_new)
            p = jnp.exp2(st - m_new)
            l = alpha * l + jnp.sum(p.reshape(groups, SUBLANES, 2 * tq), axis=0)
            acc = alpha * acc + jnp.dot(vt_ref[0, kb], p.astype(_BF16), preferred_element_type=_F32)
            return m_new, l, acc

        m0 = jnp.full((1, 2 * tq), -1e30, _F32)
        _, l, acc = lax.fori_loop(0, N_KB, body, (m0, l0, a0))
        finish(l, acc)


def _attention(shift, lamv, q, k, vt, ga, subg):
    qrow = lambda h, i: (i, h)
    return pl.pallas_call(
        _attn_kernel,
        grid=(HEADS, SEQ // ATT_TQ),
        in_specs=[pl.BlockSpec(memory_space=pltpu.SMEM),
                  pl.BlockSpec(lamv.shape, lambda h, i: (0, 0)),
                  pl.BlockSpec((ATT_TQ, V_DIM), qrow),
                  pl.BlockSpec((L_ALL, V_DIM), lambda h, i: (0, h)),
                  pl.BlockSpec((1, N_KB, V_DIM, ATT_KB), lambda h, i: (h, 0, 0, 0)),
                  pl.BlockSpec((ATT_TQ, V_DIM), qrow),
                  pl.BlockSpec((1, V_DIM), lambda h, i: (0, 0))],
        out_specs=pl.BlockSpec((ATT_TQ, V_DIM), qrow),
        out_shape=jax.ShapeDtypeStruct((SEQ, ATT_W), _BF16),
        compiler_params=pltpu.CompilerParams(vmem_limit_bytes=VMEM_LIMIT),
        name="diff_attn",
    )(shift, lamv, q, k, vt, ga, subg)


def _conv_out_kernel(y_ref, yp_ref, yn_ref, cw_ref, cb_ref, lng_ref, lnb_ref, gc_ref, att_ref,
                     w_ref, x_ref, mod_ref, o_ref, win_ref, shift_ref, conv_ref, wbf_ref):
    i = pl.program_id(0)
    n_tiles = pl.num_programs(0) - 1
    tile = jnp.minimum(i, n_tiles - 1)

    @pl.when(i == 0)
    def _():
        conv_ref[1] = jnp.zeros(conv_ref.shape[1:], conv_ref.dtype)
        wbf_ref[...] = w_ref[...].astype(wbf_ref.dtype)

    att = att_ref[...]
    conv_prev = conv_ref[(i + 1) % 2]
    ys, parts = [], []
    for c in range(CONV_CHUNKS):
        cols = slice(c * OUT_LANES, (c + 1) * OUT_LANES)
        ys.append(jnp.dot(att, wbf_ref[0:ATT_W, cols], preferred_element_type=_F32)
                  + jnp.dot(conv_prev, wbf_ref[ATT_W:, cols], preferred_element_type=_F32))
        parts.append(_conv_chunk(c, tile, n_tiles, y_ref, yp_ref, yn_ref, cw_ref, cb_ref,
                                 win_ref, shift_ref))
    gate = mod_ref[0:1, 2 * D_MODEL:3 * D_MODEL]
    o_ref[...] = x_ref[...] + gate * jnp.concatenate(ys, axis=1)
    conv_ref[i % 2] = _conv_tail(parts, lng_ref, lnb_ref, gc_ref)


def _conv_out(yglu, cw, cb, lng, lnb, gc, att, w_out, x2, mod):
    tm = OUT_TM
    n_tiles = SEQ // tm
    nb = tm // CONV_HALO
    last = SEQ // CONV_HALO - 1
    cur = lambda i: (jnp.minimum(i, n_tiles - 1), 0)
    prev = lambda i: (jnp.maximum(i - 1, 0), 0)
    fixed = lambda i: (0, 0)
    fixed3 = lambda i: (0, 0, 0)
    cw3 = cw.reshape(CONV_K, CONV_CHUNKS, CONV_LANES).transpose(1, 0, 2)
    cb3 = cb.reshape(CONV_CHUNKS, 1, CONV_LANES)
    return pl.pallas_call(
        _conv_out_kernel,
        grid=(n_tiles + 1,),
        in_specs=[pl.BlockSpec((tm, CONV_W), cur),
                  pl.BlockSpec((CONV_HALO, CONV_W),
                               lambda i: (jnp.maximum(jnp.minimum(i, n_tiles - 1) * nb - 1, 0), 0)),
                  pl.BlockSpec((CONV_HALO, CONV_W),
                               lambda i: (jnp.minimum((jnp.minimum(i, n_tiles - 1) + 1) * nb, last), 0)),
                  pl.BlockSpec((CONV_CHUNKS, CONV_K, CONV_LANES), fixed3),
                  pl.BlockSpec((CONV_CHUNKS, 1, CONV_LANES), fixed3),
                  pl.BlockSpec((1, CONV_W), fixed),
                  pl.BlockSpec((1, CONV_W), fixed),
                  pl.BlockSpec((tm, CONV_W), cur),
                  pl.BlockSpec((tm, ATT_W), prev),
                  pl.BlockSpec((D_MODEL, D_MODEL), fixed, pipeline_mode=pl.Buffered(1)),
                  pl.BlockSpec((tm, D_MODEL), prev),
                  pl.BlockSpec((MOD_ROWS, 3 * D_MODEL), fixed)],
        out_specs=pl.BlockSpec((tm, D_MODEL), prev),
        out_shape=jax.ShapeDtypeStruct((SEQ, D_MODEL), _F32),
        scratch_shapes=[pltpu.VMEM((CONV_CHUNKS, tm + 2 * CONV_HALO, CONV_LANES), _F32),
                        pltpu.VMEM((CONV_CHUNKS, SUBLANES, tm + CONV_SPAN_EXTRA, CONV_LANES), _F32),
                        pltpu.VMEM((2, tm, CONV_W), _BF16),
                        pltpu.VMEM((D_MODEL, D_MODEL), _BF16)],
        compiler_params=pltpu.CompilerParams(vmem_limit_bytes=CONV_OUT_VMEM_LIMIT),
        name="conv_out",
    )(yglu, yglu, yglu, cw3, cb3, lng, lnb, gc, att, w_out, x2, mod)


def _rope_tables():
    rows = SEQ // GRID_W
    row = np.repeat(np.arange(rows), GRID_W).astype(np.float64)
    col = np.tile(np.arange(GRID_W), rows).astype(np.float64)
    nf = HEAD_DIM // 4
    inv = ROPE_BASE ** (-np.arange(nf, dtype=np.float64) / nf)
    ang_r = row[:, None] * inv
    ang_c = col[:, None] * inv
    ang = np.concatenate([ang_r, ang_r, ang_c, ang_c], axis=-1)
    cos, sin = np.cos(ang), np.sin(ang)
    first_half = (np.arange(HEAD_DIM) % (2 * nf)) < nf
    sa = np.where(first_half, -sin, 0.0)
    sb = np.where(first_half, 0.0, sin)
    pad = lambda t, v: np.concatenate([t, np.full((CTX_LEN, HEAD_DIM), v)], axis=0)
    two = lambda t: jnp.asarray(np.concatenate([t, t], axis=1), _F32)
    return two(pad(cos, 1.0)), two(pad(sa, 0.0)), two(pad(sb, 0.0))


def _chunk_avg():
    r = np.arange(NORM_GROUP) // HEAD_DIM
    return jnp.asarray(np.where(r[:, None] == r[None, :], 1.0 / HEAD_DIM, 0.0), _BF16)


def _softmax_shift(q_norm_g, k_norm_g, q_scale):
    bound = (HEAD_DIM * q_scale * 1.01) * jnp.max(jnp.abs(q_norm_g)) * jnp.max(jnp.abs(k_norm_g))
    ok = (bound <= MAX_FIXED_SHIFT).astype(_F32)
    return jnp.stack([jnp.where(ok > 0.5, bound, 0.0), ok]).astype(_F32)


def kernel(x, c, ctx, c_ctx, w_ada, b_ada, norm_g, w_in, q_norm_g, k_norm_g, lam_q1, lam_k1,
           lam_q2, lam_k2, sub_norm_g, conv_w, conv_b, conv_ln_g, conv_ln_b, w_out):
    assert x.shape == (1, SEQ, D_MODEL) and ctx.shape == (1, CTX_LEN, D_MODEL)
    assert w_ada.shape[0] == 1 and w_in.shape == (1, D_MODEL, IN_W)
    x2 = x[0]
    ctx2 = ctx[0]

    cc = jnp.zeros((MOD_ROWS, D_MODEL), _F32).at[0].set(c[0]).at[1].set(c_ctx)
    mod = _adaln(cc, w_ada[0], b_ada)
    h = _modulate(x2, ctx2, mod, norm_g)

    w = w_in[0]
    cos, sa, sb = _rope_tables()
    q_scale = LOG2E / math.sqrt(HEAD_DIM)
    avg = _chunk_avg()
    reps = PROJ_TN // HEAD_DIM
    gq = jnp.tile(q_norm_g, (1, reps)) * q_scale
    gk = jnp.tile(k_norm_g, (1, reps))

    def qk_extras(g, tabs, tm):
        fixed = lambda j, i: (0, 0)
        tab = lambda j, i: (i, 0)
        return ([(g, (1, PROJ_TN), fixed), (avg, (NORM_GROUP, NORM_GROUP), fixed)]
                + [(t, (tm, 2 * HEAD_DIM), tab) for t in tabs])

    q = _proj_call(_qk_kernel, h, [(w, Q0)], qk_extras(gq, (cos, sa, sb), PROJ_TM),
                   rows=SEQ, tm=PROJ_TM, tn=PROJ_TN, n_col_tiles=1, out_dtype=_BF16, name="proj_q")
    k = _proj_call(_qk_kernel, h, [(w, K0)], qk_extras(gk, (cos, sa, sb), PROJ_TM_ALL),
                   rows=L_ALL, tm=PROJ_TM_ALL, tn=PROJ_TN, n_col_tiles=1, out_dtype=_BF16, name="proj_k")
    vt = _proj_call(_vt_kernel, h, [(w, V0)], [], rows=L_ALL, tm=PROJ_TM_ALL, tn=PROJ_TN,
                    n_col_tiles=1, out_dtype=_BF16, name="proj_vt",
                    out_spec=pl.BlockSpec((HEADS, 1, V_DIM, PROJ_TM_ALL), lambda j, i: (0, i, 0, 0)),
                    out_shape=jax.ShapeDtypeStruct((HEADS, N_KB, V_DIM, PROJ_TM_ALL), _BF16))
    ga = _proj_call(_silu_kernel, h, [(w, GA0)], [], rows=SEQ, tm=PROJ_TM, tn=PROJ_TN,
                    n_col_tiles=1, out_dtype=_BF16, name="proj_ga")
    gc = _proj_call(_silu_kernel, h, [(w, GC0)], [], rows=SEQ, tm=PROJ_TM, tn=PROJ_TN,
                    n_col_tiles=1, out_dtype=_BF16, name="proj_gc")
    yglu = _proj_call(_glu_kernel, h, [(w, U0), (w, U0 + CONV_W)], [], rows=SEQ, tm=PROJ_TM,
                      tn=GLU_TN, n_col_tiles=CONV_W // GLU_TN, out_dtype=_BF16, name="proj_glu")

    lamv = jnp.concatenate([lam_q1, lam_k1, lam_q2, lam_k2], axis=0)
    shift = _softmax_shift(q_norm_g, k_norm_g, q_scale)
    att = _attention(shift, lamv, q, k, vt, ga, sub_norm_g)

    out = _conv_out(yglu, conv_w[0], conv_b, conv_ln_g, conv_ln_b, gc, att, w_out[0], x2, mod)
    return out[None]
```

```python
import math

import jax
import jax.numpy as jnp
import numpy as np
from jax import lax
from jax.experimental import pallas as pl
from jax.experimental.pallas import tpu as pltpu

D_MODEL = 2048
SEQ = 8192
GRID_W = 64
CTX_LEN = 256
L_ALL = SEQ + CTX_LEN
HEADS = 8
HEAD_DIM = 64
V_DIM = 2 * HEAD_DIM
ATT_W = HEADS * V_DIM
CONV_W = D_MODEL - ATT_W
CONV_K = 31
CONV_HALO = 16
ROPE_BASE = 10000.0
EPS = 1e-6
LAM_INIT = 0.8 - 0.6 * math.exp(-0.3 * 0)
LOG2E = math.log2(math.e)

Q0 = 0
K0 = Q0 + HEADS * 2 * HEAD_DIM
V0 = K0 + HEADS * 2 * HEAD_DIM
GA0 = V0 + ATT_W
U0 = GA0 + ATT_W
GC0 = U0 + 2 * CONV_W
IN_W = GC0 + CONV_W

SUBLANES = 8
MOD_ROWS = SUBLANES
ADA_TN = 1024
MOD_TM = 1024
MOD_ROWS_PER_PASS = 64
PROJ_TM = 1024
QK_ROWS = 256
PROJ_TM_ALL = 768
PROJ_TN = 1024
GLU_TN = 512
ATT_TQ = 512
ATT_KB = PROJ_TM_ALL
N_KB = L_ALL // ATT_KB
OUT_TM = 256
CONV_LANES = 128
CONV_CHUNKS = CONV_W // CONV_LANES
OUT_LANES = D_MODEL // CONV_CHUNKS
CONV_FIRST = CONV_HALO - CONV_K // 2
CONV_SPAN_EXTRA = SUBLANES * ((CONV_FIRST + CONV_K - 1) // SUBLANES)
NORM_GROUP = 256
VMEM_LIMIT = 48 * 1024 * 1024
CONV_OUT_VMEM_LIMIT = 58 * 1024 * 1024
MAX_FIXED_SHIFT = 40.0

_F32 = jnp.float32
_BF16 = jnp.bfloat16


def _sigmoid(x):
    return 1.0 / (1.0 + jnp.exp(-x))


def _silu(x):
    return x * _sigmoid(x)


def _adaln_kernel(cc_ref, w_ref, b_ref, o_ref):
    s = _silu(cc_ref[...]).astype(_BF16)
    o_ref[...] = jnp.dot(s, w_ref[...].astype(_BF16), preferred_element_type=_F32) + b_ref[...]


def _adaln(cc, w, b):
    n = w.shape[1]
    return pl.pallas_call(
        _adaln_kernel,
        grid=(n // ADA_TN,),
        in_specs=[pl.BlockSpec((MOD_ROWS, D_MODEL), lambda j: (0, 0)),
                  pl.BlockSpec((D_MODEL, ADA_TN), lambda j: (0, j)),
                  pl.BlockSpec((1, ADA_TN), lambda j: (0, j))],
        out_specs=pl.BlockSpec((MOD_ROWS, ADA_TN), lambda j: (0, j)),
        out_shape=jax.ShapeDtypeStruct((MOD_ROWS, n), _F32),
        compiler_params=pltpu.CompilerParams(vmem_limit_bytes=VMEM_LIMIT),
        name="adaln",
    )(cc, w, b)


def _modulate_rows(xin, g, shift, scale):
    r = lax.rsqrt(jnp.mean(xin * xin, axis=-1, keepdims=True) + EPS)
    return ((xin * r) * g) * (1.0 + scale) + shift


def _modulate_kernel(x_ref, ctx_ref, mod_ref, g_ref, h_ref):
    i = pl.program_id(0)
    n_lat = pl.num_programs(0) - 1
    g = g_ref[...]

    def run(src_ref, mod_row, n_rows):
        shift = mod_ref[mod_row:mod_row + 1, 0:D_MODEL]
        scale = mod_ref[mod_row:mod_row + 1, D_MODEL:2 * D_MODEL]

        def rows(r, carry):
            sl = pl.ds(pl.multiple_of(r * MOD_ROWS_PER_PASS, MOD_ROWS_PER_PASS), MOD_ROWS_PER_PASS)
            h_ref[sl, :] = _modulate_rows(src_ref[sl, :], g, shift, scale).astype(h_ref.dtype)
            return carry

        lax.fori_loop(0, n_rows // MOD_ROWS_PER_PASS, rows, 0)

    @pl.when(i < n_lat)
    def _():
        run(x_ref, 0, MOD_TM)

    @pl.when(i == n_lat)
    def _():
        run(ctx_ref, 1, CTX_LEN)


def _modulate(x2, ctx2, mod, g):
    n_lat = SEQ // MOD_TM
    return pl.pallas_call(
        _modulate_kernel,
        grid=(n_lat + 1,),
        in_specs=[pl.BlockSpec((MOD_TM, D_MODEL), lambda i: (jnp.minimum(i, n_lat - 1), 0)),
                  pl.BlockSpec((CTX_LEN, D_MODEL), lambda i: (0, 0)),
                  pl.BlockSpec((MOD_ROWS, 3 * D_MODEL), lambda i: (0, 0)),
                  pl.BlockSpec((1, D_MODEL), lambda i: (0, 0))],
        out_specs=pl.BlockSpec((MOD_TM, D_MODEL), lambda i: (i, 0)),
        out_shape=jax.ShapeDtypeStruct((L_ALL, D_MODEL), _BF16),
        compiler_params=pltpu.CompilerParams(vmem_limit_bytes=VMEM_LIMIT),
        name="modulate",
    )(x2, ctx2, mod, g)


def _proj_call(kernel, h, w_list, extras, *, rows, tm, tn, n_col_tiles, out_dtype, name,
               out_spec=None, out_shape=None):
    in_specs = [pl.BlockSpec((tm, D_MODEL), lambda j, i: (i, 0))]
    args = [h]
    for w, c0 in w_list:
        in_specs.append(pl.BlockSpec((D_MODEL, tn), lambda j, i, c0=c0: (0, c0 // tn + j)))
        args.append(w)
    for a, bs, im in extras:
        in_specs.append(pl.BlockSpec(bs, im))
        args.append(a)
    if out_spec is None:
        out_spec = pl.BlockSpec((tm, tn), lambda j, i: (i, j))
        out_shape = jax.ShapeDtypeStruct((rows, n_col_tiles * tn), out_dtype)
    return pl.pallas_call(
        kernel,
        grid=(n_col_tiles, rows // tm),
        in_specs=in_specs,
        out_specs=out_spec,
        out_shape=out_shape,
        compiler_params=pltpu.CompilerParams(vmem_limit_bytes=VMEM_LIMIT),
        name=name,
    )(*args)


def _row_parts(h_ref):
    return [slice(r0, r0 + QK_ROWS) for r0 in range(0, h_ref.shape[0], QK_ROWS)]


def _vt_kernel(h_ref, w_ref, o_ref):
    w = w_ref[...].astype(_BF16)
    for rows in _row_parts(h_ref):
        p = jnp.dot(h_ref[rows, :], w, preferred_element_type=_F32)
        pt = p.T.astype(o_ref.dtype)
        for hd in range(HEADS):
            o_ref[hd, 0, :, rows] = pt[hd * V_DIM:(hd + 1) * V_DIM, :]


def _silu_kernel(h_ref, w_ref, o_ref):
    w = w_ref[...].astype(_BF16)
    for rows in _row_parts(h_ref):
        p = jnp.dot(h_ref[rows, :], w, preferred_element_type=_F32)
        o_ref[rows, :] = _silu(p).astype(o_ref.dtype)


def _glu_kernel(h_ref, wa_ref, wg_ref, o_ref):
    wa = wa_ref[...].astype(_BF16)
    wg = wg_ref[...].astype(_BF16)
    for rows in _row_parts(h_ref):
        h = h_ref[rows, :]
        a = jnp.dot(h, wa, preferred_element_type=_F32)
        g = jnp.dot(h, wg, preferred_element_type=_F32)
        o_ref[rows, :] = (a * _sigmoid(g)).astype(o_ref.dtype)


def _qk_kernel(h_ref, w_ref, g_ref, avg_ref, cos_ref, sa_ref, sb_ref, o_ref):
    w = w_ref[...].astype(_BF16)
    avg = avg_ref[...]
    g = g_ref[...]
    tn = w.shape[1]
    reps = tn // cos_ref.shape[1]
    quarter = HEAD_DIM // 4
    for rows in _row_parts(h_ref):
        p = jnp.dot(h_ref[rows, :], w, preferred_element_type=_F32)
        ss = p * p
        hi = ss.astype(_BF16)
        lo = (ss - hi.astype(_F32)).astype(_BF16)
        ms = []
        for c in range(tn // NORM_GROUP):
            sl = slice(c * NORM_GROUP, (c + 1) * NORM_GROUP)
            ms.append(jnp.dot(hi[:, sl], avg, preferred_element_type=_F32)
                      + jnp.dot(lo[:, sl], avg, preferred_element_type=_F32))
        ms = jnp.concatenate(ms, axis=1)
        y = (p * lax.rsqrt(ms + EPS)) * g
        cos = jnp.concatenate([cos_ref[rows, :]] * reps, axis=1)
        sa = jnp.concatenate([sa_ref[rows, :]] * reps, axis=1)
        sb = jnp.concatenate([sb_ref[rows, :]] * reps, axis=1)
        out = y * cos + pltpu.roll(y, tn - quarter, 1) * sa + pltpu.roll(y, quarter, 1) * sb
        o_ref[rows, :] = out.astype(o_ref.dtype)


def _conv_chunk(c, tile, n_tiles, y_ref, yp_ref, yn_ref, cw_ref, cb_ref, win_ref, shift_ref):
    tm = y_ref.shape[0]
    halo = CONV_HALO
    span = shift_ref.shape[2]
    keep_prev = (tile > 0).astype(_F32)
    keep_next = (tile < n_tiles - 1).astype(_F32)
    cols = slice(c * CONV_LANES, (c + 1) * CONV_LANES)
    win_ref[c, 0:halo, :] = yp_ref[:, cols].astype(_F32) * keep_prev
    win_ref[c, halo:halo + tm, :] = y_ref[:, cols].astype(_F32)
    win_ref[c, halo + tm:, :] = yn_ref[:, cols].astype(_F32) * keep_next
    for res in range(SUBLANES):
        shift_ref[c, res] = win_ref[c, res:res + span, :]
    part = jnp.zeros((tm, CONV_LANES), _F32) + cb_ref[c]
    for t in range(CONV_K):
        res = (CONV_FIRST + t) % SUBLANES
        a = (CONV_FIRST + t) - res
        part = part + shift_ref[c, res, a:a + tm, :] * cw_ref[c, t:t + 1, :]
    return part


def _conv_tail(parts, lng_ref, lnb_ref, gc_ref):
    acc = jnp.concatenate(parts, axis=1)
    mu = jnp.mean(acc, axis=-1, keepdims=True)
    cen = acc - mu
    var = jnp.mean(cen * cen, axis=-1, keepdims=True)
    ln = (cen * lax.rsqrt(var + EPS)) * lng_ref[...] + lnb_ref[...]
    return (_silu(ln) * gc_ref[...].astype(_F32)).astype(_BF16)


def _attn_kernel(shift_ref, lamv_ref, q_ref, k_ref, vt_ref, ga_ref, subg_ref, o_ref):
    tq = q_ref.shape[0]
    qt = q_ref[...].astype(_F32).T
    row = lax.broadcasted_iota(jnp.int32, qt.shape, 0)
    qst = jnp.concatenate([jnp.where(row < HEAD_DIM, qt, 0.0),
                           jnp.where(row >= HEAD_DIM, qt, 0.0)], axis=1).astype(_BF16)
    groups = ATT_KB // SUBLANES

    def scores(kb):
        start = kb * ATT_KB if isinstance(kb, int) else pl.multiple_of(kb * ATT_KB, ATT_KB)
        return jnp.dot(k_ref[pl.ds(start, ATT_KB), :], qst, preferred_element_type=_F32)

    def finish(l, acc):
        o = acc / jnp.sum(l, axis=0, keepdims=True)
        lv = lamv_ref[...]
        lam = (jnp.exp(jnp.sum(lv[0:1] * lv[1:2], axis=1, keepdims=True))
               - jnp.exp(jnp.sum(lv[2:3] * lv[3:4], axis=1, keepdims=True)) + LAM_INIT)
        od = (o[:, :tq] - lam * o[:, tq:]).T
        r = lax.rsqrt(jnp.mean(od * od, axis=-1, keepdims=True) + EPS)
        on = ((od * r) * subg_ref[...]) * (1.0 - LAM_INIT)
        o_ref[...] = (on * ga_ref[...].astype(_F32)).astype(o_ref.dtype)

    l0 = jnp.zeros((SUBLANES, 2 * tq), _F32)
    a0 = jnp.zeros((V_DIM, 2 * tq), _F32)
    fixed_ok = shift_ref[1] > 0.5

    @pl.when(fixed_ok)
    def _():
        shift = shift_ref[0]
        l, acc = l0, a0
        for kb in range(N_KB):
            p = jnp.exp2(scores(kb) - shift)
            l = l + jnp.sum(p.reshape(groups, SUBLANES, 2 * tq), axis=0)
            acc = acc + jnp.dot(vt_ref[0, kb], p.astype(_BF16), preferred_element_type=_F32)
        finish(l, acc)

    @pl.when(jnp.logical_not(fixed_ok))
    def _():
        def body(kb, carry):
            m, l, acc = carry
            st = scores(kb)
            mb = jnp.max(jnp.max(st.reshape(groups, SUBLANES, 2 * tq), axis=0), axis=0, keepdims=True)
            m_new = jnp.maximum(m, mb)
            alpha = jnp.exp2(m - m_new)
            p = jnp.exp2(st - m_new)
            l = alpha * l + jnp.sum(p.reshape(groups, SUBLANES, 2 * tq), axis=0)
            acc = alpha * acc + jnp.dot(vt_ref[0, kb], p.astype(_BF16), preferred_element_type=_F32)
            return m_new, l, acc

        m0 = jnp.full((1, 2 * tq), -1e30, _F32)
        _, l, acc = lax.fori_loop(0, N_KB, body, (m0, l0, a0))
        finish(l, acc)


def _attention(shift, lamv, q, k, vt, ga, subg):
    qrow = lambda h, i: (i, h)
    return pl.pallas_call(
        _attn_kernel,
        grid=(HEADS, SEQ // ATT_TQ),
        in_specs=[pl.BlockSpec(memory_space=pltpu.SMEM),
                  pl.BlockSpec(lamv.shape, lambda h, i: (0, 0)),
                  pl.BlockSpec((ATT_TQ, V_DIM), qrow),
                  pl.BlockSpec((L_ALL, V_DIM), lambda h, i: (0, h)),
                  pl.BlockSpec((1, N_KB, V_DIM, ATT_KB), lambda h, i: (h, 0, 0, 0)),
                  pl.BlockSpec((ATT_TQ, V_DIM), qrow),
                  pl.BlockSpec((1, V_DIM), lambda h, i: (0, 0))],
        out_specs=pl.BlockSpec((ATT_TQ, V_DIM), qrow),
        out_shape=jax.ShapeDtypeStruct((SEQ, ATT_W), _BF16),
        compiler_params=pltpu.CompilerParams(vmem_limit_bytes=VMEM_LIMIT),
        name="diff_attn",
    )(shift, lamv, q, k, vt, ga, subg)


def _conv_out_kernel(y_ref, yp_ref, yn_ref, cw_ref, cb_ref, lng_ref, lnb_ref, gc_ref, att_ref,
                     w_ref, x_ref, mod_ref, o_ref, win_ref, shift_ref, conv_ref, wbf_ref):
    i = pl.program_id(0)
    n_tiles = pl.num_programs(0) - 1
    tile = jnp.minimum(i, n_tiles - 1)

    @pl.when(i == 0)
    def _():
        conv_ref[1] = jnp.zeros(conv_ref.shape[1:], conv_ref.dtype)
        wbf_ref[...] = w_ref[...].astype(wbf_ref.dtype)

    att = att_ref[...]
    conv_prev = conv_ref[(i + 1) % 2]
    ys, parts = [], []
    for c in range(CONV_CHUNKS):
        cols = slice(c * OUT_LANES, (c + 1) * OUT_LANES)
        ys.append(jnp.dot(att, wbf_ref[0:ATT_W, cols], preferred_element_type=_F32)
                  + jnp.dot(conv_prev, wbf_ref[ATT_W:, cols], preferred_element_type=_F32))
        parts.append(_conv_chunk(c, tile, n_tiles, y_ref, yp_ref, yn_ref, cw_ref, cb_ref,
                                 win_ref, shift_ref))
    gate = mod_ref[0:1, 2 * D_MODEL:3 * D_MODEL]
    o_ref[...] = x_ref[...] + gate * jnp.concatenate(ys, axis=1)
    conv_ref[i % 2] = _conv_tail(parts, lng_ref, lnb_ref, gc_ref)


def _conv_out(yglu, cw, cb, lng, lnb, gc, att, w_out, x2, mod):
    tm = OUT_TM
    n_tiles = SEQ // tm
    nb = tm // CONV_HALO
    last = SEQ // CONV_HALO - 1
    cur = lambda i: (jnp.minimum(i, n_tiles - 1), 0)
    prev = lambda i: (jnp.maximum(i - 1, 0), 0)
    fixed = lambda i: (0, 0)
    fixed3 = lambda i: (0, 0, 0)
    cw3 = cw.reshape(CONV_K, CONV_CHUNKS, CONV_LANES).transpose(1, 0, 2)
    cb3 = cb.reshape(CONV_CHUNKS, 1, CONV_LANES)
    return pl.pallas_call(
        _conv_out_kernel,
        grid=(n_tiles + 1,),
        in_specs=[pl.BlockSpec((tm, CONV_W), cur),
                  pl.BlockSpec((CONV_HALO, CONV_W),
                               lambda i: (jnp.maximum(jnp.minimum(i, n_tiles - 1) * nb - 1, 0), 0)),
                  pl.BlockSpec((CONV_HALO, CONV_W),
                               lambda i: (jnp.minimum((jnp.minimum(i, n_tiles - 1) + 1) * nb, last), 0)),
                  pl.BlockSpec((CONV_CHUNKS, CONV_K, CONV_LANES), fixed3),
                  pl.BlockSpec((CONV_CHUNKS, 1, CONV_LANES), fixed3),
                  pl.BlockSpec((1, CONV_W), fixed),
                  pl.BlockSpec((1, CONV_W), fixed),
                  pl.BlockSpec((tm, CONV_W), cur),
                  pl.BlockSpec((tm, ATT_W), prev),
                  pl.BlockSpec((D_MODEL, D_MODEL), fixed, pipeline_mode=pl.Buffered(1)),
                  pl.BlockSpec((tm, D_MODEL), prev),
                  pl.BlockSpec((MOD_ROWS, 3 * D_MODEL), fixed)],
        out_specs=pl.BlockSpec((tm, D_MODEL), prev),
        out_shape=jax.ShapeDtypeStruct((SEQ, D_MODEL), _F32),
        scratch_shapes=[pltpu.VMEM((CONV_CHUNKS, tm + 2 * CONV_HALO, CONV_LANES), _F32),
                        pltpu.VMEM((CONV_CHUNKS, SUBLANES, tm + CONV_SPAN_EXTRA, CONV_LANES), _F32),
                        pltpu.VMEM((2, tm, CONV_W), _BF16),
                        pltpu.VMEM((D_MODEL, D_MODEL), _BF16)],
        compiler_params=pltpu.CompilerParams(vmem_limit_bytes=CONV_OUT_VMEM_LIMIT),
        name="conv_out",
    )(yglu, yglu, yglu, cw3, cb3, lng, lnb, gc, att, w_out, x2, mod)


def _rope_tables():
    rows = SEQ // GRID_W
    row = np.repeat(np.arange(rows), GRID_W).astype(np.float64)
    col = np.tile(np.arange(GRID_W), rows).astype(np.float64)
    nf = HEAD_DIM // 4
    inv = ROPE_BASE ** (-np.arange(nf, dtype=np.float64) / nf)
    ang_r = row[:, None] * inv
    ang_c = col[:, None] * inv
    ang = np.concatenate([ang_r, ang_r, ang_c, ang_c], axis=-1)
    cos, sin = np.cos(ang), np.sin(ang)
    first_half = (np.arange(HEAD_DIM) % (2 * nf)) < nf
    sa = np.where(first_half, -sin, 0.0)
    sb = np.where(first_half, 0.0, sin)
    pad = lambda t, v: np.concatenate([t, np.full((CTX_LEN, HEAD_DIM), v)], axis=0)
    two = lambda t: jnp.asarray(np.concatenate([t, t], axis=1), _F32)
    return two(pad(cos, 1.0)), two(pad(sa, 0.0)), two(pad(sb, 0.0))


def _chunk_avg():
    r = np.arange(NORM_GROUP) // HEAD_DIM
    return jnp.asarray(np.where(r[:, None] == r[None, :], 1.0 / HEAD_DIM, 0.0), _BF16)


def _softmax_shift(q_norm_g, k_norm_g, q_scale):
    bound = (HEAD_DIM * q_scale * 1.01) * jnp.max(jnp.abs(q_norm_g)) * jnp.max(jnp.abs(k_norm_g))
    ok = (bound <= MAX_FIXED_SHIFT).astype(_F32)
    return jnp.stack([jnp.where(ok > 0.5, bound, 0.0), ok]).astype(_F32)


def kernel(x, c, ctx, c_ctx, w_ada, b_ada, norm_g, w_in, q_norm_g, k_norm_g, lam_q1, lam_k1,
           lam_q2, lam_k2, sub_norm_g, conv_w, conv_b, conv_ln_g, conv_ln_b, w_out):
    assert x.shape == (1, SEQ, D_MODEL) and ctx.shape == (1, CTX_LEN, D_MODEL)
    assert w_ada.shape[0] == 1 and w_in.shape == (1, D_MODEL, IN_W)
    x2 = x[0]
    ctx2 = ctx[0]

    cc = jnp.zeros((MOD_ROWS, D_MODEL), _F32).at[0].set(c[0]).at[1].set(c_ctx)
    mod = _adaln(cc, w_ada[0], b_ada)
    h = _modulate(x2, ctx2, mod, norm_g)

    w = w_in[0]
    cos, sa, sb = _rope_tables()
    q_scale = LOG2E / math.sqrt(HEAD_DIM)
    avg = _chunk_avg()
    reps = PROJ_TN // HEAD_DIM
    gq = jnp.tile(q_norm_g, (1, reps)) * q_scale
    gk = jnp.tile(k_norm_g, (1, reps))

    def qk_extras(g, tabs, tm):
        fixed = lambda j, i: (0, 0)
        tab = lambda j, i: (i, 0)
        return ([(g, (1, PROJ_TN), fixed), (avg, (NORM_GROUP, NORM_GROUP), fixed)]
                + [(t, (tm, 2 * HEAD_DIM), tab) for t in tabs])

    q = _proj_call(_qk_kernel, h, [(w, Q0)], qk_extras(gq, (cos, sa, sb), PROJ_TM),
                   rows=SEQ, tm=PROJ_TM, tn=PROJ_TN, n_col_tiles=1, out_dtype=_BF16, name="proj_q")
    k = _proj_call(_qk_kernel, h, [(w, K0)], qk_extras(gk, (cos, sa, sb), PROJ_TM_ALL),
                   rows=L_ALL, tm=PROJ_TM_ALL, tn=PROJ_TN, n_col_tiles=1, out_dtype=_BF16, name="proj_k")
    vt = _proj_call(_vt_kernel, h, [(w, V0)], [], rows=L_ALL, tm=PROJ_TM_ALL, tn=PROJ_TN,
                    n_col_tiles=1, out_dtype=_BF16, name="proj_vt",
                    out_spec=pl.BlockSpec((HEADS, 1, V_DIM, PROJ_TM_ALL), lambda j, i: (0, i, 0, 0)),
                    out_shape=jax.ShapeDtypeStruct((HEADS, N_KB, V_DIM, PROJ_TM_ALL), _BF16))
    ga = _proj_call(_silu_kernel, h, [(w, GA0)], [], rows=SEQ, tm=PROJ_TM, tn=PROJ_TN,
                    n_col_tiles=1, out_dtype=_BF16, name="proj_ga")
    gc = _proj_call(_silu_kernel, h, [(w, GC0)], [], rows=SEQ, tm=PROJ_TM, tn=PROJ_TN,
                    n_col_tiles=1, out_dtype=_BF16, name="proj_gc")
    yglu = _proj_call(_glu_kernel, h, [(w, U0), (w, U0 + CONV_W)], [], rows=SEQ, tm=PROJ_TM,
                      tn=GLU_TN, n_col_tiles=CONV_W // GLU_TN, out_dtype=_BF16, name="proj_glu")

    lamv = jnp.concatenate([lam_q1, lam_k1, lam_q2, lam_k2], axis=0)
    shift = _softmax_shift(q_norm_g, k_norm_g, q_scale)
    att = _attention(shift, lamv, q, k, vt, ga, sub_norm_g)

    out = _conv_out(yglu, conv_w[0], conv_b, conv_ln_g, conv_ln_b, gc, att, w_out[0], x2, mod)
    return out[None]
```

```python
import math

import jax
import jax.numpy as jnp
import numpy as np
from jax import lax
from jax.experimental import pallas as pl
from jax.experimental.pallas import tpu as pltpu

D_MODEL = 2048
SEQ = 8192
GRID_W = 64
CTX_LEN = 256
L_ALL = SEQ + CTX_LEN
HEADS = 8
HEAD_DIM = 64
V_DIM = 2 * HEAD_DIM
ATT_W = HEADS * V_DIM
CONV_W = D_MODEL - ATT_W
CONV_K = 31
CONV_HALO = 16
ROPE_BASE = 10000.0
EPS = 1e-6
LAM_INIT = 0.8 - 0.6 * math.exp(-0.3 * 0)
LOG2E = math.log2(math.e)

Q0 = 0
K0 = Q0 + HEADS * 2 * HEAD_DIM
V0 = K0 + HEADS * 2 * HEAD_DIM
GA0 = V0 + ATT_W
U0 = GA0 + ATT_W
GC0 = U0 + 2 * CONV_W
IN_W = GC0 + CONV_W

SUBLANES = 8
MOD_ROWS = SUBLANES
ADA_TN = 1024
MOD_TM = 1024
MOD_ROWS_PER_PASS = 64
PROJ_TM = 1024
QK_ROWS = 256
PROJ_TM_ALL = 768
PROJ_TN = 1024
GLU_TN = 512
ATT_TQ = 512
ATT_KB = PROJ_TM_ALL
N_KB = L_ALL // ATT_KB
OUT_TM = 256
CONV_LANES = 128
CONV_CHUNKS = CONV_W // CONV_LANES
OUT_LANES = D_MODEL // CONV_CHUNKS
CONV_FIRST = CONV_HALO - CONV_K // 2
CONV_SPAN_EXTRA = SUBLANES * ((CONV_FIRST + CONV_K - 1) // SUBLANES)
NORM_GROUP = 256
MIB = 1024 * 1024
V7X_VMEM_BYTES = 64 * MIB
VMEM_LIMIT = 48 * MIB
CONV_OUT_VMEM_LIMIT = V7X_VMEM_BYTES - 6 * MIB
MAX_FIXED_SHIFT = 40.0
SHIFT_MARGIN = 1.01

_F32 = jnp.float32
_BF16 = jnp.bfloat16


def _sigmoid(x):
    return 1.0 / (1.0 + jnp.exp(-x))


def _silu(x):
    return x * _sigmoid(x)


def _adaln_kernel(cc_ref, w_ref, b_ref, o_ref):
    s = _silu(cc_ref[...]).astype(_BF16)
    o_ref[...] = jnp.dot(s, w_ref[...].astype(_BF16), preferred_element_type=_F32) + b_ref[...]


def _adaln(cc, w, b):
    n = w.shape[1]
    return pl.pallas_call(
        _adaln_kernel,
        grid=(n // ADA_TN,),
        in_specs=[pl.BlockSpec((MOD_ROWS, D_MODEL), lambda j: (0, 0)),
                  pl.BlockSpec((D_MODEL, ADA_TN), lambda j: (0, j)),
                  pl.BlockSpec((1, ADA_TN), lambda j: (0, j))],
        out_specs=pl.BlockSpec((MOD_ROWS, ADA_TN), lambda j: (0, j)),
        out_shape=jax.ShapeDtypeStruct((MOD_ROWS, n), _F32),
        compiler_params=pltpu.CompilerParams(vmem_limit_bytes=VMEM_LIMIT),
        name="adaln",
    )(cc, w, b)


def _modulate_rows(xin, g, shift, scale):
    r = lax.rsqrt(jnp.mean(xin * xin, axis=-1, keepdims=True) + EPS)
    return ((xin * r) * g) * (1.0 + scale) + shift


def _modulate_kernel(x_ref, ctx_ref, mod_ref, g_ref, h_ref):
    i = pl.program_id(0)
    n_lat = pl.num_programs(0) - 1
    g = g_ref[...]

    def run(src_ref, mod_row, n_rows):
        shift = mod_ref[mod_row:mod_row + 1, 0:D_MODEL]
        scale = mod_ref[mod_row:mod_row + 1, D_MODEL:2 * D_MODEL]

        def rows(r, carry):
            sl = pl.ds(pl.multiple_of(r * MOD_ROWS_PER_PASS, MOD_ROWS_PER_PASS), MOD_ROWS_PER_PASS)
            h_ref[sl, :] = _modulate_rows(src_ref[sl, :], g, shift, scale).astype(h_ref.dtype)
            return carry

        lax.fori_loop(0, n_rows // MOD_ROWS_PER_PASS, rows, 0)

    @pl.when(i < n_lat)
    def _():
        run(x_ref, 0, MOD_TM)

    @pl.when(i == n_lat)
    def _():
        run(ctx_ref, 1, CTX_LEN)


def _modulate(x2, ctx2, mod, g):
    n_lat = SEQ // MOD_TM
    return pl.pallas_call(
        _modulate_kernel,
        grid=(n_lat + 1,),
        in_specs=[pl.BlockSpec((MOD_TM, D_MODEL), lambda i: (jnp.minimum(i, n_lat - 1), 0)),
                  pl.BlockSpec((CTX_LEN, D_MODEL), lambda i: (0, 0)),
                  pl.BlockSpec((MOD_ROWS, 3 * D_MODEL), lambda i: (0, 0)),
                  pl.BlockSpec((1, D_MODEL), lambda i: (0, 0))],
        out_specs=pl.BlockSpec((MOD_TM, D_MODEL), lambda i: (i, 0)),
        out_shape=jax.ShapeDtypeStruct((L_ALL, D_MODEL), _BF16),
        compiler_params=pltpu.CompilerParams(vmem_limit_bytes=VMEM_LIMIT),
        name="modulate",
    )(x2, ctx2, mod, g)


def _proj_call(kernel, h, w_list, extras, *, rows, tm, tn, n_col_tiles, out_dtype, name,
               out_spec=None, out_shape=None, col_stride=1):
    in_specs = [pl.BlockSpec((tm, D_MODEL), lambda j, i: (i, 0))]
    args = [h]
    for w, c0 in w_list:
        in_specs.append(pl.BlockSpec((D_MODEL, tn), lambda j, i, c0=c0: (0, c0 // tn + j * col_stride)))
        args.append(w)
    for a, bs, im in extras:
        in_specs.append(pl.BlockSpec(bs, im))
        args.append(a)
    if out_spec is None:
        out_spec = pl.BlockSpec((tm, tn), lambda j, i: (i, j))
        out_shape = jax.ShapeDtypeStruct((rows, n_col_tiles * tn), out_dtype)
    return pl.pallas_call(
        kernel,
        grid=(n_col_tiles, rows // tm),
        in_specs=in_specs,
        out_specs=out_spec,
        out_shape=out_shape,
        compiler_params=pltpu.CompilerParams(vmem_limit_bytes=VMEM_LIMIT),
        name=name,
    )(*args)


def _row_parts(h_ref):
    return [slice(r0, r0 + QK_ROWS) for r0 in range(0, h_ref.shape[0], QK_ROWS)]


def _vt_kernel(h_ref, w_ref, o_ref):
    w = w_ref[...].astype(_BF16)
    for rows in _row_parts(h_ref):
        p = jnp.dot(h_ref[rows, :], w, preferred_element_type=_F32)
        pt = p.T.astype(o_ref.dtype)
        for hd in range(HEADS):
            o_ref[hd, 0, :, rows] = pt[hd * V_DIM:(hd + 1) * V_DIM, :]


def _silu_kernel(h_ref, w_ref, o_ref):
    w = w_ref[...].astype(_BF16)
    for rows in _row_parts(h_ref):
        p = jnp.dot(h_ref[rows, :], w, preferred_element_type=_F32)
        o_ref[rows, :] = _silu(p).astype(o_ref.dtype)


def _glu_kernel(h_ref, wa_ref, wg_ref, o_ref):
    wa = wa_ref[...].astype(_BF16)
    wg = wg_ref[...].astype(_BF16)
    for rows in _row_parts(h_ref):
        h = h_ref[rows, :]
        a = jnp.dot(h, wa, preferred_element_type=_F32)
        g = jnp.dot(h, wg, preferred_element_type=_F32)
        o_ref[rows, :] = (a * _sigmoid(g)).astype(o_ref.dtype)


def _qk_kernel(h_ref, w_ref, g_ref, avg_ref, cos_ref, sa_ref, sb_ref, o_ref):
    w = w_ref[...].astype(_BF16)
    avg = avg_ref[...]
    g = g_ref[...]
    tn = w.shape[1]
    reps = tn // cos_ref.shape[1]
    quarter = HEAD_DIM // 4
    for rows in _row_parts(h_ref):
        p = jnp.dot(h_ref[rows, :], w, preferred_element_type=_F32)
        ss = p * p
        hi = ss.astype(_BF16)
        lo = (ss - hi.astype(_F32)).astype(_BF16)
        ms = []
        for c in range(tn // NORM_GROUP):
            sl = slice(c * NORM_GROUP, (c + 1) * NORM_GROUP)
            ms.append(jnp.dot(hi[:, sl], avg, preferred_element_type=_F32)
                      + jnp.dot(lo[:, sl], avg, preferred_element_type=_F32))
        ms = jnp.concatenate(ms, axis=1)
        y = (p * lax.rsqrt(ms + EPS)) * g
        cos = jnp.concatenate([cos_ref[rows, :]] * reps, axis=1)
        sa = jnp.concatenate([sa_ref[rows, :]] * reps, axis=1)
        sb = jnp.concatenate([sb_ref[rows, :]] * reps, axis=1)
        out = y * cos + pltpu.roll(y, tn - quarter, 1) * sa + pltpu.roll(y, quarter, 1) * sb
        o_ref[rows, :] = out.astype(o_ref.dtype)


def _conv_chunk(c, tile, n_tiles, y_ref, yp_ref, yn_ref, cw_ref, cb_ref, win_ref, shift_ref):
    tm = y_ref.shape[0]
    halo = CONV_HALO
    span = shift_ref.shape[2]
    keep_prev = (tile > 0).astype(_F32)
    keep_next = (tile < n_tiles - 1).astype(_F32)
    cols = slice(c * CONV_LANES, (c + 1) * CONV_LANES)
    win_ref[c, 0:halo, :] = yp_ref[:, cols].astype(_F32) * keep_prev
    win_ref[c, halo:halo + tm, :] = y_ref[:, cols].astype(_F32)
    win_ref[c, halo + tm:, :] = yn_ref[:, cols].astype(_F32) * keep_next
    for res in range(SUBLANES):
        shift_ref[c, res] = win_ref[c, res:res + span, :]
    part = jnp.zeros((tm, CONV_LANES), _F32) + cb_ref[c]
    for t in range(CONV_K):
        res = (CONV_FIRST + t) % SUBLANES
        a = (CONV_FIRST + t) - res
        part = part + shift_ref[c, res, a:a + tm, :] * cw_ref[c, t:t + 1, :]
    return part


def _conv_tail(parts, lng_ref, lnb_ref, gc_ref):
    acc = jnp.concatenate(parts, axis=1)
    mu = jnp.mean(acc, axis=-1, keepdims=True)
    cen = acc - mu
    var = jnp.mean(cen * cen, axis=-1, keepdims=True)
    ln = (cen * lax.rsqrt(var + EPS)) * lng_ref[...] + lnb_ref[...]
    return (_silu(ln) * gc_ref[...].astype(_F32)).astype(_BF16)


def _attn_kernel(shift_ref, lamv_ref, q_ref, k_ref, vt_ref, ga_ref, subg_ref, o_ref):
    tq = q_ref.shape[0]
    qt = q_ref[...].T
    row = lax.broadcasted_iota(jnp.int32, qt.shape, 0)
    zero = jnp.zeros_like(qt)
    qst = jnp.concatenate([jnp.where(row < HEAD_DIM, qt, zero),
                           jnp.where(row >= HEAD_DIM, qt, zero)], axis=1)
    groups = ATT_KB // SUBLANES

    def scores(kb):
        start = kb * ATT_KB if isinstance(kb, int) else pl.multiple_of(kb * ATT_KB, ATT_KB)
        return jnp.dot(k_ref[pl.ds(start, ATT_KB), :], qst, preferred_element_type=_F32)

    def finish(l, acc):
        o = acc / jnp.sum(l, axis=0, keepdims=True)
        lv = lamv_ref[...]
        lam = (jnp.exp(jnp.sum(lv[0:1] * lv[1:2], axis=1, keepdims=True))
               - jnp.exp(jnp.sum(lv[2:3] * lv[3:4], axis=1, keepdims=True)) + LAM_INIT)
        od = o[:, :tq] - lam * o[:, tq:]
        r = lax.rsqrt(jnp.mean(od * od, axis=0, keepdims=True) + EPS)
        on = ((od * r).T * subg_ref[...]) * (1.0 - LAM_INIT)
        o_ref[...] = (on * ga_ref[...].astype(_F32)).astype(o_ref.dtype)

    l0 = jnp.zeros((SUBLANES, 2 * tq), _F32)
    a0 = jnp.zeros((V_DIM, 2 * tq), _F32)
    fixed_ok = shift_ref[1] > 0.5

    @pl.when(fixed_ok)
    def _():
        shift = shift_ref[0]
        l, acc = l0, a0
        for kb in range(N_KB):
            p = jnp.exp2(scores(kb) - shift)
            l = l + jnp.sum(p.reshape(groups, SUBLANES, 2 * tq), axis=0)
            acc = acc + jnp.dot(vt_ref[0, kb], p.astype(_BF16), preferred_element_type=_F32)
        finish(l, acc)

    @pl.when(jnp.logical_not(fixed_ok))
    def _():
        def body(kb, carry):
            m, l, acc = carry
            st = scores(kb)
            mb = jnp.max(jnp.max(st.reshape(groups, SUBLANES, 2 * tq), axis=0), axis=0, keepdims=True)
            m_new = jnp.maximum(m, mb)
            alpha = jnp.exp2(m - m_new)
            p = jnp.exp2(st - m_new)
            l = alpha * l + jnp.sum(p.reshape(groups, SUBLANES, 2 * tq), axis=0)
            acc = alpha * acc + jnp.dot(vt_ref[0, kb], p.astype(_BF16), preferred_element_type=_F32)
            return m_new, l, acc

        m0 = jnp.full((1, 2 * tq), -1e30, _F32)
        _, l, acc = lax.fori_loop(0, N_KB, body, (m0, l0, a0))
        finish(l, acc)


def _attention(shift, lamv, q, k, vt, ga, subg):
    qrow = lambda h, i: (i, h)
    return pl.pallas_call(
        _attn_kernel,
        grid=(HEADS, SEQ // ATT_TQ),
        in_specs=[pl.BlockSpec(memory_space=pltpu.SMEM),
                  pl.BlockSpec(lamv.shape, lambda h, i: (0, 0)),
                  pl.BlockSpec((ATT_TQ, V_DIM), qrow),
                  pl.BlockSpec((L_ALL, V_DIM), lambda h, i: (0, h)),
                  pl.BlockSpec((1, N_KB, V_DIM, ATT_KB), lambda h, i: (h, 0, 0, 0)),
                  pl.BlockSpec((ATT_TQ, V_DIM), qrow),
                  pl.BlockSpec((1, V_DIM), lambda h, i: (0, 0))],
        out_specs=pl.BlockSpec((ATT_TQ, V_DIM), qrow),
        out_shape=jax.ShapeDtypeStruct((SEQ, ATT_W), _BF16),
        compiler_params=pltpu.CompilerParams(vmem_limit_bytes=VMEM_LIMIT),
        name="diff_attn",
    )(shift, lamv, q, k, vt, ga, subg)


def _conv_out_kernel(y_ref, yp_ref, yn_ref, cw_ref, cb_ref, lng_ref, lnb_ref, gc_ref, att_ref,
                     w_ref, x_ref, mod_ref, o_ref, win_ref, shift_ref, conv_ref, wbf_ref):
    i = pl.program_id(0)
    n_tiles = pl.num_programs(0) - 1
    tile = jnp.minimum(i, n_tiles - 1)

    @pl.when(i == 0)
    def _():
        conv_ref[1] = jnp.zeros(conv_ref.shape[1:], conv_ref.dtype)
        wbf_ref[...] = w_ref[...].astype(wbf_ref.dtype)

    att = att_ref[...]
    conv_prev = conv_ref[(i + 1) % 2]
    ys, parts = [], []
    for c in range(CONV_CHUNKS):
        cols = slice(c * OUT_LANES, (c + 1) * OUT_LANES)
        ys.append(jnp.dot(att, wbf_ref[0:ATT_W, cols], preferred_element_type=_F32)
                  + jnp.dot(conv_prev, wbf_ref[ATT_W:, cols], preferred_element_type=_F32))
        parts.append(_conv_chunk(c, tile, n_tiles, y_ref, yp_ref, yn_ref, cw_ref, cb_ref,
                                 win_ref, shift_ref))
    gate = mod_ref[0:1, 2 * D_MODEL:3 * D_MODEL]
    o_ref[...] = x_ref[...] + gate * jnp.concatenate(ys, axis=1)
    conv_ref[i % 2] = _conv_tail(parts, lng_ref, lnb_ref, gc_ref)


def _conv_out(yglu, cw, cb, lng, lnb, gc, att, w_out, x2, mod):
    tm = OUT_TM
    n_tiles = SEQ // tm
    nb = tm // CONV_HALO
    last = SEQ // CONV_HALO - 1
    cur = lambda i: (jnp.minimum(i, n_tiles - 1), 0)
    prev = lambda i: (jnp.maximum(i - 1, 0), 0)
    fixed = lambda i: (0, 0)
    fixed3 = lambda i: (0, 0, 0)
    cw3 = cw.reshape(CONV_K, CONV_CHUNKS, CONV_LANES).transpose(1, 0, 2)
    cb3 = cb.reshape(CONV_CHUNKS, 1, CONV_LANES)
    return pl.pallas_call(
        _conv_out_kernel,
        grid=(n_tiles + 1,),
        in_specs=[pl.BlockSpec((tm, CONV_W), cur),
                  pl.BlockSpec((CONV_HALO, CONV_W),
                               lambda i: (jnp.maximum(jnp.minimum(i, n_tiles - 1) * nb - 1, 0), 0)),
                  pl.BlockSpec((CONV_HALO, CONV_W),
                               lambda i: (jnp.minimum((jnp.minimum(i, n_tiles - 1) + 1) * nb, last), 0)),
                  pl.BlockSpec((CONV_CHUNKS, CONV_K, CONV_LANES), fixed3),
                  pl.BlockSpec((CONV_CHUNKS, 1, CONV_LANES), fixed3),
                  pl.BlockSpec((1, CONV_W), fixed),
                  pl.BlockSpec((1, CONV_W), fixed),
                  pl.BlockSpec((tm, CONV_W), lambda i: (jnp.minimum(i, n_tiles - 1), 1)),
                  pl.BlockSpec((tm, ATT_W), prev),
                  pl.BlockSpec((D_MODEL, D_MODEL), fixed, pipeline_mode=pl.Buffered(1)),
                  pl.BlockSpec((tm, D_MODEL), prev),
                  pl.BlockSpec((MOD_ROWS, 3 * D_MODEL), fixed)],
        out_specs=pl.BlockSpec((tm, D_MODEL), prev),
        out_shape=jax.ShapeDtypeStruct((SEQ, D_MODEL), _F32),
        scratch_shapes=[pltpu.VMEM((CONV_CHUNKS, tm + 2 * CONV_HALO, CONV_LANES), _F32),
                        pltpu.VMEM((CONV_CHUNKS, SUBLANES, tm + CONV_SPAN_EXTRA, CONV_LANES), _F32),
                        pltpu.VMEM((2, tm, CONV_W), _BF16),
                        pltpu.VMEM((D_MODEL, D_MODEL), _BF16)],
        compiler_params=pltpu.CompilerParams(vmem_limit_bytes=CONV_OUT_VMEM_LIMIT,
                                             dimension_semantics=("arbitrary",)),
        name="conv_out",
    )(yglu, yglu, yglu, cw3, cb3, lng, lnb, gc, att, w_out, x2, mod)


def _rope_tables():
    rows = SEQ // GRID_W
    row = np.repeat(np.arange(rows), GRID_W).astype(np.float64)
    col = np.tile(np.arange(GRID_W), rows).astype(np.float64)
    nf = HEAD_DIM // 4
    inv = ROPE_BASE ** (-np.arange(nf, dtype=np.float64) / nf)
    ang_r = row[:, None] * inv
    ang_c = col[:, None] * inv
    ang = np.concatenate([ang_r, ang_r, ang_c, ang_c], axis=-1)
    cos, sin = np.cos(ang), np.sin(ang)
    first_half = (np.arange(HEAD_DIM) % (2 * nf)) < nf
    sa = np.where(first_half, -sin, 0.0)
    sb = np.where(first_half, 0.0, sin)
    pad = lambda t, v: np.concatenate([t, np.full((CTX_LEN, HEAD_DIM), v)], axis=0)
    two = lambda t: jnp.asarray(np.concatenate([t, t], axis=1), _F32)
    return two(pad(cos, 1.0)), two(pad(sa, 0.0)), two(pad(sb, 0.0))


def _chunk_avg():
    r = np.arange(NORM_GROUP) // HEAD_DIM
    return jnp.asarray(np.where(r[:, None] == r[None, :], 1.0 / HEAD_DIM, 0.0), _BF16)


def _softmax_shift(q_norm_g, k_norm_g, q_scale):
    bound = (HEAD_DIM * q_scale * SHIFT_MARGIN) * jnp.max(jnp.abs(q_norm_g)) * jnp.max(jnp.abs(k_norm_g))
    ok = (bound <= MAX_FIXED_SHIFT).astype(_F32)
    return jnp.stack([jnp.where(ok > 0.5, bound, 0.0), ok]).astype(_F32)


def kernel(x, c, ctx, c_ctx, w_ada, b_ada, norm_g, w_in, q_norm_g, k_norm_g, lam_q1, lam_k1,
           lam_q2, lam_k2, sub_norm_g, conv_w, conv_b, conv_ln_g, conv_ln_b, w_out):
    assert x.shape == (1, SEQ, D_MODEL) and ctx.shape == (1, CTX_LEN, D_MODEL)
    assert w_ada.shape[0] == 1 and w_in.shape == (1, D_MODEL, IN_W)
    x2 = x[0]
    ctx2 = ctx[0]

    cc = jnp.zeros((MOD_ROWS, D_MODEL), _F32).at[0].set(c[0]).at[1].set(c_ctx)
    mod = _adaln(cc, w_ada[0], b_ada)
    h = _modulate(x2, ctx2, mod, norm_g)

    w = w_in[0]
    cos, sa, sb = _rope_tables()
    q_scale = LOG2E / math.sqrt(HEAD_DIM)
    avg = _chunk_avg()
    reps = PROJ_TN // HEAD_DIM
    gq = jnp.tile(q_norm_g, (1, reps)) * q_scale
    gk = jnp.tile(k_norm_g, (1, reps))

    def qk_extras(g, tabs, tm):
        fixed = lambda j, i: (0, 0)
        tab = lambda j, i: (i, 0)
        return ([(g, (1, PROJ_TN), fixed), (avg, (NORM_GROUP, NORM_GROUP), fixed)]
                + [(t, (tm, 2 * HEAD_DIM), tab) for t in tabs])

    q = _proj_call(_qk_kernel, h, [(w, Q0)], qk_extras(gq, (cos, sa, sb), PROJ_TM),
                   rows=SEQ, tm=PROJ_TM, tn=PROJ_TN, n_col_tiles=1, out_dtype=_BF16, name="proj_q")
    k = _proj_call(_qk_kernel, h, [(w, K0)], qk_extras(gk, (cos, sa, sb), PROJ_TM_ALL),
                   rows=L_ALL, tm=PROJ_TM_ALL, tn=PROJ_TN, n_col_tiles=1, out_dtype=_BF16, name="proj_k")
    vt = _proj_call(_vt_kernel, h, [(w, V0)], [], rows=L_ALL, tm=PROJ_TM_ALL, tn=PROJ_TN,
                    n_col_tiles=1, out_dtype=_BF16, name="proj_vt",
                    out_spec=pl.BlockSpec((HEADS, 1, V_DIM, PROJ_TM_ALL), lambda j, i: (0, i, 0, 0)),
                    out_shape=jax.ShapeDtypeStruct((HEADS, N_KB, V_DIM, PROJ_TM_ALL), _BF16))
    gates = _proj_call(_silu_kernel, h, [(w, GA0)], [], rows=SEQ, tm=PROJ_TM, tn=PROJ_TN,
                       n_col_tiles=2, col_stride=(GC0 - GA0) // PROJ_TN, out_dtype=_BF16, name="proj_gates")
    yglu = _proj_call(_glu_kernel, h, [(w, U0), (w, U0 + CONV_W)], [], rows=SEQ, tm=PROJ_TM,
                      tn=GLU_TN, n_col_tiles=CONV_W // GLU_TN, out_dtype=_BF16, name="proj_glu")

    lamv = jnp.concatenate([lam_q1, lam_k1, lam_q2, lam_k2], axis=0)
    shift = _softmax_shift(q_norm_g, k_norm_g, q_scale)
    att = _attention(shift, lamv, q, k, vt, gates, sub_norm_g)

    out = _conv_out(yglu, conv_w[0], conv_b, conv_ln_g, conv_ln_b, gates, att, w_out[0], x2, mod)
    return out[None]
```

```python
import math

import jax
import jax.numpy as jnp
import numpy as np
from jax import lax
from jax.experimental import pallas as pl
from jax.experimental.pallas import tpu as pltpu

D_MODEL = 2048
SEQ = 8192
GRID_W = 64
CTX_LEN = 256
L_ALL = SEQ + CTX_LEN
HEADS = 8
HEAD_DIM = 64
V_DIM = 2 * HEAD_DIM
ATT_W = HEADS * V_DIM
CONV_W = D_MODEL - ATT_W
CONV_K = 31
CONV_HALO = 16
ROPE_BASE = 10000.0
EPS = 1e-6
LAM_INIT = 0.8 - 0.6 * math.exp(-0.3 * 0)
LOG2E = math.log2(math.e)

Q0 = 0
K0 = Q0 + HEADS * 2 * HEAD_DIM
V0 = K0 + HEADS * 2 * HEAD_DIM
GA0 = V0 + ATT_W
U0 = GA0 + ATT_W
GC0 = U0 + 2 * CONV_W
IN_W = GC0 + CONV_W

SUBLANES = 8
MOD_ROWS = SUBLANES
ADA_TN = 1024
MOD_TM = 1024
MOD_ROWS_PER_PASS = 64
PROJ_TM = 1024
QK_ROWS = 256
PROJ_TM_ALL = 768
PROJ_TN = 1024
GLU_TN = 512
ATT_TQ = 512
ATT_KB = PROJ_TM_ALL
N_KB = L_ALL // ATT_KB
FAST_KB = 6 * ATT_KB
OUT_TM = 256
CONV_LANES = 128
CONV_CHUNKS = CONV_W // CONV_LANES
OUT_LANES = D_MODEL // CONV_CHUNKS
CONV_FIRST = CONV_HALO - CONV_K // 2
CONV_SPAN_EXTRA = SUBLANES * ((CONV_FIRST + CONV_K - 1) // SUBLANES)
NORM_GROUP = 256
MIB = 1024 * 1024
V7X_VMEM_BYTES = 64 * MIB
VMEM_LIMIT = 48 * MIB
CONV_OUT_VMEM_LIMIT = V7X_VMEM_BYTES - 6 * MIB
MAX_FIXED_SHIFT = 40.0
SHIFT_MARGIN = 1.01

_F32 = jnp.float32
_BF16 = jnp.bfloat16


def _sigmoid(x):
    return 1.0 / (1.0 + jnp.exp(-x))


def _silu(x):
    return x * _sigmoid(x)


def _adaln_kernel(cc_ref, w_ref, b_ref, o_ref):
    s = _silu(cc_ref[...]).astype(_BF16)
    o_ref[...] = jnp.dot(s, w_ref[...].astype(_BF16), preferred_element_type=_F32) + b_ref[...]


def _adaln(cc, w, b):
    n = w.shape[1]
    return pl.pallas_call(
        _adaln_kernel,
        grid=(n // ADA_TN,),
        in_specs=[pl.BlockSpec((MOD_ROWS, D_MODEL), lambda j: (0, 0)),
                  pl.BlockSpec((D_MODEL, ADA_TN), lambda j: (0, j)),
                  pl.BlockSpec((1, ADA_TN), lambda j: (0, j))],
        out_specs=pl.BlockSpec((MOD_ROWS, ADA_TN), lambda j: (0, j)),
        out_shape=jax.ShapeDtypeStruct((MOD_ROWS, n), _F32),
        compiler_params=pltpu.CompilerParams(vmem_limit_bytes=VMEM_LIMIT),
        name="adaln",
    )(cc, w, b)


def _modulate_rows(xin, g, shift, scale):
    r = lax.rsqrt(jnp.mean(xin * xin, axis=-1, keepdims=True) + EPS)
    return ((xin * r) * g) * (1.0 + scale) + shift


def _modulate_kernel(x_ref, ctx_ref, mod_ref, g_ref, h_ref):
    i = pl.program_id(0)
    n_lat = pl.num_programs(0) - 1
    g = g_ref[...]

    def run(src_ref, mod_row, n_rows):
        shift = mod_ref[mod_row:mod_row + 1, 0:D_MODEL]
        scale = mod_ref[mod_row:mod_row + 1, D_MODEL:2 * D_MODEL]

        def rows(r, carry):
            sl = pl.ds(pl.multiple_of(r * MOD_ROWS_PER_PASS, MOD_ROWS_PER_PASS), MOD_ROWS_PER_PASS)
            h_ref[sl, :] = _modulate_rows(src_ref[sl, :], g, shift, scale).astype(h_ref.dtype)
            return carry

        lax.fori_loop(0, n_rows // MOD_ROWS_PER_PASS, rows, 0)

    @pl.when(i < n_lat)
    def _():
        run(x_ref, 0, MOD_TM)

    @pl.when(i == n_lat)
    def _():
        run(ctx_ref, 1, CTX_LEN)


def _modulate(x2, ctx2, mod, g):
    n_lat = SEQ // MOD_TM
    return pl.pallas_call(
        _modulate_kernel,
        grid=(n_lat + 1,),
        in_specs=[pl.BlockSpec((MOD_TM, D_MODEL), lambda i: (jnp.minimum(i, n_lat - 1), 0)),
                  pl.BlockSpec((CTX_LEN, D_MODEL), lambda i: (0, 0)),
                  pl.BlockSpec((MOD_ROWS, 3 * D_MODEL), lambda i: (0, 0)),
                  pl.BlockSpec((1, D_MODEL), lambda i: (0, 0))],
        out_specs=pl.BlockSpec((MOD_TM, D_MODEL), lambda i: (i, 0)),
        out_shape=jax.ShapeDtypeStruct((L_ALL, D_MODEL), _BF16),
        compiler_params=pltpu.CompilerParams(vmem_limit_bytes=VMEM_LIMIT),
        name="modulate",
    )(x2, ctx2, mod, g)


def _proj_call(kernel, h, w_list, extras, *, rows, tm, tn, n_col_tiles, out_dtype, name,
               out_spec=None, out_shape=None, col_stride=1):
    in_specs = [pl.BlockSpec((tm, D_MODEL), lambda j, i: (i, 0))]
    args = [h]
    for w, c0 in w_list:
        in_specs.append(pl.BlockSpec((D_MODEL, tn), lambda j, i, c0=c0: (0, c0 // tn + j * col_stride)))
        args.append(w)
    for a, bs, im in extras:
        in_specs.append(pl.BlockSpec(bs, im))
        args.append(a)
    if out_spec is None:
        out_spec = pl.BlockSpec((tm, tn), lambda j, i: (i, j))
        out_shape = jax.ShapeDtypeStruct((rows, n_col_tiles * tn), out_dtype)
    return pl.pallas_call(
        kernel,
        grid=(n_col_tiles, rows // tm),
        in_specs=in_specs,
        out_specs=out_spec,
        out_shape=out_shape,
        compiler_params=pltpu.CompilerParams(vmem_limit_bytes=VMEM_LIMIT),
        name=name,
    )(*args)


def _row_parts(h_ref):
    return [slice(r0, r0 + QK_ROWS) for r0 in range(0, h_ref.shape[0], QK_ROWS)]


def _vt_kernel(h_ref, w_ref, o_ref):
    w = w_ref[...].astype(_BF16)
    for rows in _row_parts(h_ref):
        p = jnp.dot(h_ref[rows, :], w, preferred_element_type=_F32)
        pt = p.T.astype(o_ref.dtype)
        for hd in range(HEADS):
            o_ref[hd, 0, :, rows] = pt[hd * V_DIM:(hd + 1) * V_DIM, :]


def _silu_kernel(h_ref, w_ref, o_ref):
    w = w_ref[...].astype(_BF16)
    for rows in _row_parts(h_ref):
        p = jnp.dot(h_ref[rows, :], w, preferred_element_type=_F32)
        o_ref[rows, :] = _silu(p).astype(o_ref.dtype)


def _glu_kernel(h_ref, wa_ref, wg_ref, o_ref):
    wa = wa_ref[...].astype(_BF16)
    wg = wg_ref[...].astype(_BF16)
    for rows in _row_parts(h_ref):
        h = h_ref[rows, :]
        a = jnp.dot(h, wa, preferred_element_type=_F32)
        g = jnp.dot(h, wg, preferred_element_type=_F32)
        o_ref[rows, :] = (a * _sigmoid(g)).astype(o_ref.dtype)


def _qk_kernel(h_ref, w_ref, g_ref, avg_ref, cos_ref, sa_ref, sb_ref, o_ref):
    w = w_ref[...].astype(_BF16)
    avg = avg_ref[...]
    g = g_ref[...]
    tn = w.shape[1]
    reps = tn // cos_ref.shape[1]
    quarter = HEAD_DIM // 4
    for rows in _row_parts(h_ref):
        p = jnp.dot(h_ref[rows, :], w, preferred_element_type=_F32)
        ss = p * p
        hi = ss.astype(_BF16)
        lo = (ss - hi.astype(_F32)).astype(_BF16)
        ms = []
        for c in range(tn // NORM_GROUP):
            sl = slice(c * NORM_GROUP, (c + 1) * NORM_GROUP)
            ms.append(jnp.dot(hi[:, sl], avg, preferred_element_type=_F32)
                      + jnp.dot(lo[:, sl], avg, preferred_element_type=_F32))
        ms = jnp.concatenate(ms, axis=1)
        y = (p * lax.rsqrt(ms + EPS)) * g
        cos = jnp.concatenate([cos_ref[rows, :]] * reps, axis=1)
        sa = jnp.concatenate([sa_ref[rows, :]] * reps, axis=1)
        sb = jnp.concatenate([sb_ref[rows, :]] * reps, axis=1)
        out = y * cos + pltpu.roll(y, tn - quarter, 1) * sa + pltpu.roll(y, quarter, 1) * sb
        o_ref[rows, :] = out.astype(o_ref.dtype)


def _conv_chunk(c, tile, n_tiles, y_ref, yp_ref, yn_ref, cw_ref, cb_ref, win_ref, shift_ref):
    tm = y_ref.shape[0]
    halo = CONV_HALO
    span = shift_ref.shape[2]
    keep_prev = (tile > 0).astype(_F32)
    keep_next = (tile < n_tiles - 1).astype(_F32)
    cols = slice(c * CONV_LANES, (c + 1) * CONV_LANES)
    win_ref[c, 0:halo, :] = yp_ref[:, cols].astype(_F32) * keep_prev
    win_ref[c, halo:halo + tm, :] = y_ref[:, cols].astype(_F32)
    win_ref[c, halo + tm:, :] = yn_ref[:, cols].astype(_F32) * keep_next
    for res in range(SUBLANES):
        shift_ref[c, res] = win_ref[c, res:res + span, :]
    part = jnp.zeros((tm, CONV_LANES), _F32) + cb_ref[c]
    for t in range(CONV_K):
        res = (CONV_FIRST + t) % SUBLANES
        a = (CONV_FIRST + t) - res
        part = part + shift_ref[c, res, a:a + tm, :] * cw_ref[c, t:t + 1, :]
    return part


def _conv_tail(parts, lng_ref, lnb_ref, gc_ref):
    acc = jnp.concatenate(parts, axis=1)
    mu = jnp.mean(acc, axis=-1, keepdims=True)
    cen = acc - mu
    var = jnp.mean(cen * cen, axis=-1, keepdims=True)
    ln = (cen * lax.rsqrt(var + EPS)) * lng_ref[...] + lnb_ref[...]
    return (_silu(ln) * gc_ref[...].astype(_F32)).astype(_BF16)


def _attn_kernel(shift_ref, lamv_ref, q_ref, k_ref, vt_ref, ga_ref, subg_ref, o_ref):
    tq = q_ref.shape[0]
    qt = q_ref[...].T
    row = lax.broadcasted_iota(jnp.int32, qt.shape, 0)
    zero = jnp.zeros_like(qt)
    qst = jnp.concatenate([jnp.where(row < HEAD_DIM, qt, zero),
                           jnp.where(row >= HEAD_DIM, qt, zero)], axis=1)
    groups = ATT_KB // SUBLANES

    def scores(kb):
        start = kb * ATT_KB if isinstance(kb, int) else pl.multiple_of(kb * ATT_KB, ATT_KB)
        return jnp.dot(k_ref[pl.ds(start, ATT_KB), :], qst, preferred_element_type=_F32)

    def finish(l, acc):
        o = acc / jnp.sum(l, axis=0, keepdims=True)
        lv = lamv_ref[...]
        lam = (jnp.exp(jnp.sum(lv[0:1] * lv[1:2], axis=1, keepdims=True))
               - jnp.exp(jnp.sum(lv[2:3] * lv[3:4], axis=1, keepdims=True)) + LAM_INIT)
        od = o[:, :tq] - lam * o[:, tq:]
        r = lax.rsqrt(jnp.mean(od * od, axis=0, keepdims=True) + EPS)
        on = ((od * r).T * subg_ref[...]) * (1.0 - LAM_INIT)
        o_ref[...] = (on * ga_ref[...].astype(_F32)).astype(o_ref.dtype)

    l0 = jnp.zeros((SUBLANES, 2 * tq), _F32)
    a0 = jnp.zeros((V_DIM, 2 * tq), _F32)
    fixed_ok = shift_ref[1] > 0.5

    @pl.when(fixed_ok)
    def _():
        shift = shift_ref[0]
        l, acc = l0, a0
        for start in range(0, L_ALL, FAST_KB):
            size = min(FAST_KB, L_ALL - start)
            st = jnp.dot(k_ref[start:start + size, :], qst, preferred_element_type=_F32)
            p = jnp.exp2(st - shift)
            l = l + jnp.sum(p.reshape(size // SUBLANES, SUBLANES, 2 * tq), axis=0)
            pb = p.astype(_BF16)
            for off in range(0, size, ATT_KB):
                acc = acc + jnp.dot(vt_ref[0, (start + off) // ATT_KB], pb[off:off + ATT_KB, :],
                                    preferred_element_type=_F32)
        finish(l, acc)

    @pl.when(jnp.logical_not(fixed_ok))
    def _():
        def body(kb, carry):
            m, l, acc = carry
            st = scores(kb)
            mb = jnp.max(jnp.max(st.reshape(groups, SUBLANES, 2 * tq), axis=0), axis=0, keepdims=True)
            m_new = jnp.maximum(m, mb)
            alpha = jnp.exp2(m - m_new)
            p = jnp.exp2(st - m_new)
            l = alpha * l + jnp.sum(p.reshape(groups, SUBLANES, 2 * tq), axis=0)
            acc = alpha * acc + jnp.dot(vt_ref[0, kb], p.astype(_BF16), preferred_element_type=_F32)
            return m_new, l, acc

        m0 = jnp.full((1, 2 * tq), -1e30, _F32)
        _, l, acc = lax.fori_loop(0, N_KB, body, (m0, l0, a0))
        finish(l, acc)


def _attention(shift, lamv, q, k, vt, ga, subg):
    qrow = lambda h, i: (i, h)
    return pl.pallas_call(
        _attn_kernel,
        grid=(HEADS, SEQ // ATT_TQ),
        in_specs=[pl.BlockSpec(memory_space=pltpu.SMEM),
                  pl.BlockSpec(lamv.shape, lambda h, i: (0, 0)),
                  pl.BlockSpec((ATT_TQ, V_DIM), qrow),
                  pl.BlockSpec((L_ALL, V_DIM), lambda h, i: (0, h)),
                  pl.BlockSpec((1, N_KB, V_DIM, ATT_KB), lambda h, i: (h, 0, 0, 0)),
                  pl.BlockSpec((ATT_TQ, V_DIM), qrow),
                  pl.BlockSpec((1, V_DIM), lambda h, i: (0, 0))],
        out_specs=pl.BlockSpec((ATT_TQ, V_DIM), qrow),
        out_shape=jax.ShapeDtypeStruct((SEQ, ATT_W), _BF16),
        compiler_params=pltpu.CompilerParams(vmem_limit_bytes=VMEM_LIMIT),
        name="diff_attn",
    )(shift, lamv, q, k, vt, ga, subg)


def _conv_out_kernel(y_ref, yp_ref, yn_ref, cw_ref, cb_ref, lng_ref, lnb_ref, gc_ref, att_ref,
                     w_ref, x_ref, mod_ref, o_ref, win_ref, shift_ref, conv_ref, wbf_ref):
    i = pl.program_id(0)
    n_tiles = pl.num_programs(0) - 1
    tile = jnp.minimum(i, n_tiles - 1)

    @pl.when(i == 0)
    def _():
        conv_ref[1] = jnp.zeros(conv_ref.shape[1:], conv_ref.dtype)
        wbf_ref[...] = w_ref[...].astype(wbf_ref.dtype)

    att = att_ref[...]
    conv_prev = conv_ref[(i + 1) % 2]
    ys, parts = [], []
    for c in range(CONV_CHUNKS):
        cols = slice(c * OUT_LANES, (c + 1) * OUT_LANES)
        ys.append(jnp.dot(att, wbf_ref[0:ATT_W, cols], preferred_element_type=_F32)
                  + jnp.dot(conv_prev, wbf_ref[ATT_W:, cols], preferred_element_type=_F32))
        parts.append(_conv_chunk(c, tile, n_tiles, y_ref, yp_ref, yn_ref, cw_ref, cb_ref,
                                 win_ref, shift_ref))
    gate = mod_ref[0:1, 2 * D_MODEL:3 * D_MODEL]
    o_ref[...] = x_ref[...] + gate * jnp.concatenate(ys, axis=1)
    conv_ref[i % 2] = _conv_tail(parts, lng_ref, lnb_ref, gc_ref)


def _conv_out(yglu, cw, cb, lng, lnb, gc, att, w_out, x2, mod):
    tm = OUT_TM
    n_tiles = SEQ // tm
    nb = tm // CONV_HALO
    last = SEQ // CONV_HALO - 1
    cur = lambda i: (jnp.minimum(i, n_tiles - 1), 0)
    prev = lambda i: (jnp.maximum(i - 1, 0), 0)
    fixed = lambda i: (0, 0)
    fixed3 = lambda i: (0, 0, 0)
    cw3 = cw.reshape(CONV_K, CONV_CHUNKS, CONV_LANES).transpose(1, 0, 2)
    cb3 = cb.reshape(CONV_CHUNKS, 1, CONV_LANES)
    return pl.pallas_call(
        _conv_out_kernel,
        grid=(n_tiles + 1,),
        in_specs=[pl.BlockSpec((tm, CONV_W), cur),
                  pl.BlockSpec((CONV_HALO, CONV_W),
                               lambda i: (jnp.maximum(jnp.minimum(i, n_tiles - 1) * nb - 1, 0), 0)),
                  pl.BlockSpec((CONV_HALO, CONV_W),
                               lambda i: (jnp.minimum((jnp.minimum(i, n_tiles - 1) + 1) * nb, last), 0)),
                  pl.BlockSpec((CONV_CHUNKS, CONV_K, CONV_LANES), fixed3),
                  pl.BlockSpec((CONV_CHUNKS, 1, CONV_LANES), fixed3),
                  pl.BlockSpec((1, CONV_W), fixed),
                  pl.BlockSpec((1, CONV_W), fixed),
                  pl.BlockSpec((tm, CONV_W), lambda i: (jnp.minimum(i, n_tiles - 1), 1)),
                  pl.BlockSpec((tm, ATT_W), prev),
                  pl.BlockSpec((D_MODEL, D_MODEL), fixed, pipeline_mode=pl.Buffered(1)),
                  pl.BlockSpec((tm, D_MODEL), prev),
                  pl.BlockSpec((MOD_ROWS, 3 * D_MODEL), fixed)],
        out_specs=pl.BlockSpec((tm, D_MODEL), prev),
        out_shape=jax.ShapeDtypeStruct((SEQ, D_MODEL), _F32),
        scratch_shapes=[pltpu.VMEM((CONV_CHUNKS, tm + 2 * CONV_HALO, CONV_LANES), _F32),
                        pltpu.VMEM((CONV_CHUNKS, SUBLANES, tm + CONV_SPAN_EXTRA, CONV_LANES), _F32),
                        pltpu.VMEM((2, tm, CONV_W), _BF16),
                        pltpu.VMEM((D_MODEL, D_MODEL), _BF16)],
        compiler_params=pltpu.CompilerParams(vmem_limit_bytes=CONV_OUT_VMEM_LIMIT,
                                             dimension_semantics=("arbitrary",)),
        name="conv_out",
    )(yglu, yglu, yglu, cw3, cb3, lng, lnb, gc, att, w_out, x2, mod)


def _rope_tables():
    rows = SEQ // GRID_W
    row = np.repeat(np.arange(rows), GRID_W).astype(np.float64)
    col = np.tile(np.arange(GRID_W), rows).astype(np.float64)
    nf = HEAD_DIM // 4
    inv = ROPE_BASE ** (-np.arange(nf, dtype=np.float64) / nf)
    ang_r = row[:, None] * inv
    ang_c = col[:, None] * inv
    ang = np.concatenate([ang_r, ang_r, ang_c, ang_c], axis=-1)
    cos, sin = np.cos(ang), np.sin(ang)
    first_half = (np.arange(HEAD_DIM) % (2 * nf)) < nf
    sa = np.where(first_half, -sin, 0.0)
    sb = np.where(first_half, 0.0, sin)
    pad = lambda t, v: np.concatenate([t, np.full((CTX_LEN, HEAD_DIM), v)], axis=0)
    two = lambda t: jnp.asarray(np.concatenate([t, t], axis=1), _F32)
    return two(pad(cos, 1.0)), two(pad(sa, 0.0)), two(pad(sb, 0.0))


def _chunk_avg():
    r = np.arange(NORM_GROUP) // HEAD_DIM
    return jnp.asarray(np.where(r[:, None] == r[None, :], 1.0 / HEAD_DIM, 0.0), _BF16)


def _softmax_shift(q_norm_g, k_norm_g, q_scale):
    bound = (HEAD_DIM * q_scale * SHIFT_MARGIN) * jnp.max(jnp.abs(q_norm_g)) * jnp.max(jnp.abs(k_norm_g))
    ok = (bound <= MAX_FIXED_SHIFT).astype(_F32)
    return jnp.stack([jnp.where(ok > 0.5, bound, 0.0), ok]).astype(_F32)


def kernel(x, c, ctx, c_ctx, w_ada, b_ada, norm_g, w_in, q_norm_g, k_norm_g, lam_q1, lam_k1,
           lam_q2, lam_k2, sub_norm_g, conv_w, conv_b, conv_ln_g, conv_ln_b, w_out):
    assert x.shape == (1, SEQ, D_MODEL) and ctx.shape == (1, CTX_LEN, D_MODEL)
    assert w_ada.shape[0] == 1 and w_in.shape == (1, D_MODEL, IN_W)
    x2 = x[0]
    ctx2 = ctx[0]

    cc = jnp.zeros((MOD_ROWS, D_MODEL), _F32).at[0].set(c[0]).at[1].set(c_ctx)
    mod = _adaln(cc, w_ada[0], b_ada)
    h = _modulate(x2, ctx2, mod, norm_g)

    w = w_in[0]
    cos, sa, sb = _rope_tables()
    q_scale = LOG2E / math.sqrt(HEAD_DIM)
    avg = _chunk_avg()
    reps = PROJ_TN // HEAD_DIM
    gq = jnp.tile(q_norm_g, (1, reps)) * q_scale
    gk = jnp.tile(k_norm_g, (1, reps))

    def qk_extras(g, tabs, tm):
        fixed = lambda j, i: (0, 0)
        tab = lambda j, i: (i, 0)
        return ([(g, (1, PROJ_TN), fixed), (avg, (NORM_GROUP, NORM_GROUP), fixed)]
                + [(t, (tm, 2 * HEAD_DIM), tab) for t in tabs])

    q = _proj_call(_qk_kernel, h, [(w, Q0)], qk_extras(gq, (cos, sa, sb), PROJ_TM),
                   rows=SEQ, tm=PROJ_TM, tn=PROJ_TN, n_col_tiles=1, out_dtype=_BF16, name="proj_q")
    k = _proj_call(_qk_kernel, h, [(w, K0)], qk_extras(gk, (cos, sa, sb), PROJ_TM_ALL),
                   rows=L_ALL, tm=PROJ_TM_ALL, tn=PROJ_TN, n_col_tiles=1, out_dtype=_BF16, name="proj_k")
    vt = _proj_call(_vt_kernel, h, [(w, V0)], [], rows=L_ALL, tm=PROJ_TM_ALL, tn=PROJ_TN,
                    n_col_tiles=1, out_dtype=_BF16, name="proj_vt",
                    out_spec=pl.BlockSpec((HEADS, 1, V_DIM, PROJ_TM_ALL), lambda j, i: (0, i, 0, 0)),
                    out_shape=jax.ShapeDtypeStruct((HEADS, N_KB, V_DIM, PROJ_TM_ALL), _BF16))
    gates = _proj_call(_silu_kernel, h, [(w, GA0)], [], rows=SEQ, tm=PROJ_TM, tn=PROJ_TN,
                       n_col_tiles=2, col_stride=(GC0 - GA0) // PROJ_TN, out_dtype=_BF16, name="proj_gates")
    yglu = _proj_call(_glu_kernel, h, [(w, U0), (w, U0 + CONV_W)], [], rows=SEQ, tm=PROJ_TM,
                      tn=GLU_TN, n_col_tiles=CONV_W // GLU_TN, out_dtype=_BF16, name="proj_glu")

    lamv = jnp.concatenate([lam_q1, lam_k1, lam_q2, lam_k2], axis=0)
    shift = _softmax_shift(q_norm_g, k_norm_g, q_scale)
    att = _attention(shift, lamv, q, k, vt, gates, sub_norm_g)

    out = _conv_out(yglu, conv_w[0], conv_b, conv_ln_g, conv_ln_b, gates, att, w_out[0], x2, mod)
    return out[None]
```

```python
import math

import jax
import jax.numpy as jnp
import numpy as np
from jax import lax
from jax.experimental import pallas as pl
from jax.experimental.pallas import tpu as pltpu

D_MODEL = 2048
SEQ = 8192
GRID_W = 64
CTX_LEN = 256
L_ALL = SEQ + CTX_LEN
HEADS = 8
HEAD_DIM = 64
V_DIM = 2 * HEAD_DIM
ATT_W = HEADS * V_DIM
CONV_W = D_MODEL - ATT_W
CONV_K = 31
CONV_HALO = 16
ROPE_BASE = 10000.0
EPS = 1e-6
LAM_INIT = 0.8 - 0.6 * math.exp(-0.3 * 0)
LOG2E = math.log2(math.e)

Q0 = 0
K0 = Q0 + HEADS * 2 * HEAD_DIM
V0 = K0 + HEADS * 2 * HEAD_DIM
GA0 = V0 + ATT_W
U0 = GA0 + ATT_W
GC0 = U0 + 2 * CONV_W
IN_W = GC0 + CONV_W

SUBLANES = 8
MOD_ROWS = SUBLANES
ADA_TN = 1024
MOD_TM = 1024
MOD_ROWS_PER_PASS = 64
PROJ_TM = 1024
QK_ROWS = 256
PROJ_TM_ALL = 768
PROJ_TN = 1024
GLU_TN = 512
ATT_TQ = 512
ATT_KB = PROJ_TM_ALL
N_KB = L_ALL // ATT_KB
FAST_KB = 11 * ATT_KB
OUT_TM = 256
CONV_LANES = 128
CONV_CHUNKS = CONV_W // CONV_LANES
OUT_LANES = D_MODEL // CONV_CHUNKS
CONV_FIRST = CONV_HALO - CONV_K // 2
CONV_SPAN_EXTRA = SUBLANES * ((CONV_FIRST + CONV_K - 1) // SUBLANES)
NORM_GROUP = 256
MIB = 1024 * 1024
V7X_VMEM_BYTES = 64 * MIB
VMEM_LIMIT = 48 * MIB
CONV_OUT_VMEM_LIMIT = V7X_VMEM_BYTES - 6 * MIB
MAX_FIXED_SHIFT = 40.0
SHIFT_MARGIN = 1.01

_F32 = jnp.float32
_BF16 = jnp.bfloat16


def _sigmoid(x):
    return 1.0 / (1.0 + jnp.exp(-x))


def _silu(x):
    return x * _sigmoid(x)


def _adaln_kernel(cc_ref, w_ref, b_ref, o_ref):
    s = _silu(cc_ref[...]).astype(_BF16)
    o_ref[...] = jnp.dot(s, w_ref[...].astype(_BF16), preferred_element_type=_F32) + b_ref[...]


def _adaln(cc, w, b):
    n = w.shape[1]
    return pl.pallas_call(
        _adaln_kernel,
        grid=(n // ADA_TN,),
        in_specs=[pl.BlockSpec((MOD_ROWS, D_MODEL), lambda j: (0, 0)),
                  pl.BlockSpec((D_MODEL, ADA_TN), lambda j: (0, j)),
                  pl.BlockSpec((1, ADA_TN), lambda j: (0, j))],
        out_specs=pl.BlockSpec((MOD_ROWS, ADA_TN), lambda j: (0, j)),
        out_shape=jax.ShapeDtypeStruct((MOD_ROWS, n), _F32),
        compiler_params=pltpu.CompilerParams(vmem_limit_bytes=VMEM_LIMIT),
        name="adaln",
    )(cc, w, b)


def _modulate_rows(xin, g, shift, scale):
    r = lax.rsqrt(jnp.mean(xin * xin, axis=-1, keepdims=True) + EPS)
    return ((xin * r) * g) * (1.0 + scale) + shift


def _modulate_kernel(x_ref, ctx_ref, mod_ref, g_ref, h_ref):
    i = pl.program_id(0)
    n_lat = pl.num_programs(0) - 1
    g = g_ref[...]

    def run(src_ref, mod_row, n_rows):
        shift = mod_ref[mod_row:mod_row + 1, 0:D_MODEL]
        scale = mod_ref[mod_row:mod_row + 1, D_MODEL:2 * D_MODEL]

        def rows(r, carry):
            sl = pl.ds(pl.multiple_of(r * MOD_ROWS_PER_PASS, MOD_ROWS_PER_PASS), MOD_ROWS_PER_PASS)
            h_ref[sl, :] = _modulate_rows(src_ref[sl, :], g, shift, scale).astype(h_ref.dtype)
            return carry

        lax.fori_loop(0, n_rows // MOD_ROWS_PER_PASS, rows, 0)

    @pl.when(i < n_lat)
    def _():
        run(x_ref, 0, MOD_TM)

    @pl.when(i == n_lat)
    def _():
        run(ctx_ref, 1, CTX_LEN)


def _modulate(x2, ctx2, mod, g):
    n_lat = SEQ // MOD_TM
    return pl.pallas_call(
        _modulate_kernel,
        grid=(n_lat + 1,),
        in_specs=[pl.BlockSpec((MOD_TM, D_MODEL), lambda i: (jnp.minimum(i, n_lat - 1), 0)),
                  pl.BlockSpec((CTX_LEN, D_MODEL), lambda i: (0, 0)),
                  pl.BlockSpec((MOD_ROWS, 3 * D_MODEL), lambda i: (0, 0)),
                  pl.BlockSpec((1, D_MODEL), lambda i: (0, 0))],
        out_specs=pl.BlockSpec((MOD_TM, D_MODEL), lambda i: (i, 0)),
        out_shape=jax.ShapeDtypeStruct((L_ALL, D_MODEL), _BF16),
        compiler_params=pltpu.CompilerParams(vmem_limit_bytes=VMEM_LIMIT),
        name="modulate",
    )(x2, ctx2, mod, g)


def _proj_call(kernel, h, w_list, extras, *, rows, tm, tn, n_col_tiles, out_dtype, name,
               out_spec=None, out_shape=None, col_stride=1):
    in_specs = [pl.BlockSpec((tm, D_MODEL), lambda j, i: (i, 0))]
    args = [h]
    for w, c0 in w_list:
        in_specs.append(pl.BlockSpec((D_MODEL, tn), lambda j, i, c0=c0: (0, c0 // tn + j * col_stride)))
        args.append(w)
    for a, bs, im in extras:
        in_specs.append(pl.BlockSpec(bs, im))
        args.append(a)
    if out_spec is None:
        out_spec = pl.BlockSpec((tm, tn), lambda j, i: (i, j))
        out_shape = jax.ShapeDtypeStruct((rows, n_col_tiles * tn), out_dtype)
    return pl.pallas_call(
        kernel,
        grid=(n_col_tiles, rows // tm),
        in_specs=in_specs,
        out_specs=out_spec,
        out_shape=out_shape,
        compiler_params=pltpu.CompilerParams(vmem_limit_bytes=VMEM_LIMIT),
        name=name,
    )(*args)


def _row_parts(h_ref):
    return [slice(r0, r0 + QK_ROWS) for r0 in range(0, h_ref.shape[0], QK_ROWS)]


def _vt_kernel(h_ref, w_ref, o_ref):
    w = w_ref[...].astype(_BF16)
    for rows in _row_parts(h_ref):
        p = jnp.dot(h_ref[rows, :], w, preferred_element_type=_F32)
        pt = p.T.astype(o_ref.dtype)
        for hd in range(HEADS):
            o_ref[hd, 0, :, rows] = pt[hd * V_DIM:(hd + 1) * V_DIM, :]


def _silu_kernel(h_ref, w_ref, o_ref):
    w = w_ref[...].astype(_BF16)
    for rows in _row_parts(h_ref):
        p = jnp.dot(h_ref[rows, :], w, preferred_element_type=_F32)
        o_ref[rows, :] = _silu(p).astype(o_ref.dtype)


def _glu_kernel(h_ref, wa_ref, wg_ref, o_ref):
    wa = wa_ref[...].astype(_BF16)
    wg = wg_ref[...].astype(_BF16)
    for rows in _row_parts(h_ref):
        h = h_ref[rows, :]
        a = jnp.dot(h, wa, preferred_element_type=_F32)
        g = jnp.dot(h, wg, preferred_element_type=_F32)
        o_ref[rows, :] = (a * _sigmoid(g)).astype(o_ref.dtype)


def _qk_kernel(h_ref, w_ref, g_ref, avg_ref, cos_ref, sa_ref, sb_ref, o_ref):
    w = w_ref[...].astype(_BF16)
    avg = avg_ref[...]
    g = g_ref[...]
    tn = w.shape[1]
    reps = tn // cos_ref.shape[1]
    quarter = HEAD_DIM // 4
    for rows in _row_parts(h_ref):
        p = jnp.dot(h_ref[rows, :], w, preferred_element_type=_F32)
        ss = p * p
        hi = ss.astype(_BF16)
        lo = (ss - hi.astype(_F32)).astype(_BF16)
        ms = []
        for c in range(tn // NORM_GROUP):
            sl = slice(c * NORM_GROUP, (c + 1) * NORM_GROUP)
            ms.append(jnp.dot(hi[:, sl], avg, preferred_element_type=_F32)
                      + jnp.dot(lo[:, sl], avg, preferred_element_type=_F32))
        ms = jnp.concatenate(ms, axis=1)
        y = (p * lax.rsqrt(ms + EPS)) * g
        cos = jnp.concatenate([cos_ref[rows, :]] * reps, axis=1)
        sa = jnp.concatenate([sa_ref[rows, :]] * reps, axis=1)
        sb = jnp.concatenate([sb_ref[rows, :]] * reps, axis=1)
        out = y * cos + pltpu.roll(y, tn - quarter, 1) * sa + pltpu.roll(y, quarter, 1) * sb
        o_ref[rows, :] = out.astype(o_ref.dtype)


def _conv_chunk(c, tile, n_tiles, y_ref, yp_ref, yn_ref, cw_ref, cb_ref, win_ref, shift_ref):
    tm = y_ref.shape[0]
    halo = CONV_HALO
    span = shift_ref.shape[2]
    keep_prev = (tile > 0).astype(_F32)
    keep_next = (tile < n_tiles - 1).astype(_F32)
    cols = slice(c * CONV_LANES, (c + 1) * CONV_LANES)
    win_ref[c, 0:halo, :] = yp_ref[:, cols].astype(_F32) * keep_prev
    win_ref[c, halo:halo + tm, :] = y_ref[:, cols].astype(_F32)
    win_ref[c, halo + tm:, :] = yn_ref[:, cols].astype(_F32) * keep_next
    for res in range(SUBLANES):
        shift_ref[c, res] = win_ref[c, res:res + span, :]
    part = jnp.zeros((tm, CONV_LANES), _F32) + cb_ref[c]
    for t in range(CONV_K):
        res = (CONV_FIRST + t) % SUBLANES
        a = (CONV_FIRST + t) - res
        part = part + shift_ref[c, res, a:a + tm, :] * cw_ref[c, t:t + 1, :]
    return part


def _conv_tail(parts, lng_ref, lnb_ref, gc_ref):
    acc = jnp.concatenate(parts, axis=1)
    mu = jnp.mean(acc, axis=-1, keepdims=True)
    cen = acc - mu
    var = jnp.mean(cen * cen, axis=-1, keepdims=True)
    ln = (cen * lax.rsqrt(var + EPS)) * lng_ref[...] + lnb_ref[...]
    return (_silu(ln) * gc_ref[...].astype(_F32)).astype(_BF16)


def _attn_kernel(shift_ref, lamv_ref, q_ref, k_ref, vt_ref, ga_ref, subg_ref, o_ref):
    tq = q_ref.shape[0]
    qt = q_ref[...].T
    row = lax.broadcasted_iota(jnp.int32, qt.shape, 0)
    zero = jnp.zeros_like(qt)
    qst = jnp.concatenate([jnp.where(row < HEAD_DIM, qt, zero),
                           jnp.where(row >= HEAD_DIM, qt, zero)], axis=1)
    groups = ATT_KB // SUBLANES

    def scores(kb):
        start = kb * ATT_KB if isinstance(kb, int) else pl.multiple_of(kb * ATT_KB, ATT_KB)
        return jnp.dot(k_ref[pl.ds(start, ATT_KB), :], qst, preferred_element_type=_F32)

    def finish(l, acc):
        o = acc / jnp.sum(l, axis=0, keepdims=True)
        lv = lamv_ref[...]
        lam = (jnp.exp(jnp.sum(lv[0:1] * lv[1:2], axis=1, keepdims=True))
               - jnp.exp(jnp.sum(lv[2:3] * lv[3:4], axis=1, keepdims=True)) + LAM_INIT)
        od = o[:, :tq] - lam * o[:, tq:]
        r = lax.rsqrt(jnp.mean(od * od, axis=0, keepdims=True) + EPS)
        on = ((od * r).T * subg_ref[...]) * (1.0 - LAM_INIT)
        o_ref[...] = (on * ga_ref[...].astype(_F32)).astype(o_ref.dtype)

    l0 = jnp.zeros((SUBLANES, 2 * tq), _F32)
    a0 = jnp.zeros((V_DIM, 2 * tq), _F32)
    fixed_ok = shift_ref[1] > 0.5

    @pl.when(fixed_ok)
    def _():
        shift = shift_ref[0]
        l, acc = l0, a0
        for start in range(0, L_ALL, FAST_KB):
            size = min(FAST_KB, L_ALL - start)
            st = jnp.dot(k_ref[start:start + size, :], qst, preferred_element_type=_F32)
            p = jnp.exp2(st - shift)
            l = l + jnp.sum(p.reshape(size // SUBLANES, SUBLANES, 2 * tq), axis=0)
            pb = p.astype(_BF16)
            for off in range(0, size, ATT_KB):
                acc = acc + jnp.dot(vt_ref[0, (start + off) // ATT_KB], pb[off:off + ATT_KB, :],
                                    preferred_element_type=_F32)
        finish(l, acc)

    @pl.when(jnp.logical_not(fixed_ok))
    def _():
        def body(kb, carry):
            m, l, acc = carry
            st = scores(kb)
            mb = jnp.max(jnp.max(st.reshape(groups, SUBLANES, 2 * tq), axis=0), axis=0, keepdims=True)
            m_new = jnp.maximum(m, mb)
            alpha = jnp.exp2(m - m_new)
            p = jnp.exp2(st - m_new)
            l = alpha * l + jnp.sum(p.reshape(groups, SUBLANES, 2 * tq), axis=0)
            acc = alpha * acc + jnp.dot(vt_ref[0, kb], p.astype(_BF16), preferred_element_type=_F32)
            return m_new, l, acc

        m0 = jnp.full((1, 2 * tq), -1e30, _F32)
        _, l, acc = lax.fori_loop(0, N_KB, body, (m0, l0, a0))
        finish(l, acc)


def _attention(shift, lamv, q, k, vt, ga, subg):
    qrow = lambda h, i: (i, h)
    return pl.pallas_call(
        _attn_kernel,
        grid=(HEADS, SEQ // ATT_TQ),
        in_specs=[pl.BlockSpec(memory_space=pltpu.SMEM),
                  pl.BlockSpec(lamv.shape, lambda h, i: (0, 0)),
                  pl.BlockSpec((ATT_TQ, V_DIM), qrow),
                  pl.BlockSpec((L_ALL, V_DIM), lambda h, i: (0, h)),
                  pl.BlockSpec((1, N_KB, V_DIM, ATT_KB), lambda h, i: (h, 0, 0, 0)),
                  pl.BlockSpec((ATT_TQ, V_DIM), qrow),
                  pl.BlockSpec((1, V_DIM), lambda h, i: (0, 0))],
        out_specs=pl.BlockSpec((ATT_TQ, V_DIM), qrow),
        out_shape=jax.ShapeDtypeStruct((SEQ, ATT_W), _BF16),
        compiler_params=pltpu.CompilerParams(vmem_limit_bytes=VMEM_LIMIT),
        name="diff_attn",
    )(shift, lamv, q, k, vt, ga, subg)


def _conv_out_kernel(y_ref, yp_ref, yn_ref, cw_ref, cb_ref, lng_ref, lnb_ref, gc_ref, att_ref,
                     w_ref, x_ref, mod_ref, o_ref, win_ref, shift_ref, conv_ref, wbf_ref):
    i = pl.program_id(0)
    n_tiles = pl.num_programs(0) - 1
    tile = jnp.minimum(i, n_tiles - 1)

    @pl.when(i == 0)
    def _():
        conv_ref[1] = jnp.zeros(conv_ref.shape[1:], conv_ref.dtype)
        wbf_ref[...] = w_ref[...].astype(wbf_ref.dtype)

    att = att_ref[...]
    conv_prev = conv_ref[(i + 1) % 2]
    ys, parts = [], []
    for c in range(CONV_CHUNKS):
        cols = slice(c * OUT_LANES, (c + 1) * OUT_LANES)
        ys.append(jnp.dot(att, wbf_ref[0:ATT_W, cols], preferred_element_type=_F32)
                  + jnp.dot(conv_prev, wbf_ref[ATT_W:, cols], preferred_element_type=_F32))
        parts.append(_conv_chunk(c, tile, n_tiles, y_ref, yp_ref, yn_ref, cw_ref, cb_ref,
                                 win_ref, shift_ref))
    gate = mod_ref[0:1, 2 * D_MODEL:3 * D_MODEL]
    o_ref[...] = x_ref[...] + gate * jnp.concatenate(ys, axis=1)
    conv_ref[i % 2] = _conv_tail(parts, lng_ref, lnb_ref, gc_ref)


def _conv_out(yglu, cw, cb, lng, lnb, gc, att, w_out, x2, mod):
    tm = OUT_TM
    n_tiles = SEQ // tm
    nb = tm // CONV_HALO
    last = SEQ // CONV_HALO - 1
    cur = lambda i: (jnp.minimum(i, n_tiles - 1), 0)
    prev = lambda i: (jnp.maximum(i - 1, 0), 0)
    fixed = lambda i: (0, 0)
    fixed3 = lambda i: (0, 0, 0)
    cw3 = cw.reshape(CONV_K, CONV_CHUNKS, CONV_LANES).transpose(1, 0, 2)
    cb3 = cb.reshape(CONV_CHUNKS, 1, CONV_LANES)
    return pl.pallas_call(
        _conv_out_kernel,
        grid=(n_tiles + 1,),
        in_specs=[pl.BlockSpec((tm, CONV_W), cur),
                  pl.BlockSpec((CONV_HALO, CONV_W),
                               lambda i: (jnp.maximum(jnp.minimum(i, n_tiles - 1) * nb - 1, 0), 0)),
                  pl.BlockSpec((CONV_HALO, CONV_W),
                               lambda i: (jnp.minimum((jnp.minimum(i, n_tiles - 1) + 1) * nb, last), 0)),
                  pl.BlockSpec((CONV_CHUNKS, CONV_K, CONV_LANES), fixed3),
                  pl.BlockSpec((CONV_CHUNKS, 1, CONV_LANES), fixed3),
                  pl.BlockSpec((1, CONV_W), fixed),
                  pl.BlockSpec((1, CONV_W), fixed),
                  pl.BlockSpec((tm, CONV_W), lambda i: (jnp.minimum(i, n_tiles - 1), 1)),
                  pl.BlockSpec((tm, ATT_W), prev),
                  pl.BlockSpec((D_MODEL, D_MODEL), fixed, pipeline_mode=pl.Buffered(1)),
                  pl.BlockSpec((tm, D_MODEL), prev),
                  pl.BlockSpec((MOD_ROWS, 3 * D_MODEL), fixed)],
        out_specs=pl.BlockSpec((tm, D_MODEL), prev),
        out_shape=jax.ShapeDtypeStruct((SEQ, D_MODEL), _F32),
        scratch_shapes=[pltpu.VMEM((CONV_CHUNKS, tm + 2 * CONV_HALO, CONV_LANES), _F32),
                        pltpu.VMEM((CONV_CHUNKS, SUBLANES, tm + CONV_SPAN_EXTRA, CONV_LANES), _F32),
                        pltpu.VMEM((2, tm, CONV_W), _BF16),
                        pltpu.VMEM((D_MODEL, D_MODEL), _BF16)],
        compiler_params=pltpu.CompilerParams(vmem_limit_bytes=CONV_OUT_VMEM_LIMIT,
                                             dimension_semantics=("arbitrary",)),
        name="conv_out",
    )(yglu, yglu, yglu, cw3, cb3, lng, lnb, gc, att, w_out, x2, mod)


def _rope_tables():
    rows = SEQ // GRID_W
    row = np.repeat(np.arange(rows), GRID_W).astype(np.float64)
    col = np.tile(np.arange(GRID_W), rows).astype(np.float64)
    nf = HEAD_DIM // 4
    inv = ROPE_BASE ** (-np.arange(nf, dtype=np.float64) / nf)
    ang_r = row[:, None] * inv
    ang_c = col[:, None] * inv
    ang = np.concatenate([ang_r, ang_r, ang_c, ang_c], axis=-1)
    cos, sin = np.cos(ang), np.sin(ang)
    first_half = (np.arange(HEAD_DIM) % (2 * nf)) < nf
    sa = np.where(first_half, -sin, 0.0)
    sb = np.where(first_half, 0.0, sin)
    pad = lambda t, v: np.concatenate([t, np.full((CTX_LEN, HEAD_DIM), v)], axis=0)
    two = lambda t: jnp.asarray(np.concatenate([t, t], axis=1), _F32)
    return two(pad(cos, 1.0)), two(pad(sa, 0.0)), two(pad(sb, 0.0))


def _chunk_avg():
    r = np.arange(NORM_GROUP) // HEAD_DIM
    return jnp.asarray(np.where(r[:, None] == r[None, :], 1.0 / HEAD_DIM, 0.0), _BF16)


def _softmax_shift(q_norm_g, k_norm_g, q_scale):
    bound = (HEAD_DIM * q_scale * SHIFT_MARGIN) * jnp.max(jnp.abs(q_norm_g)) * jnp.max(jnp.abs(k_norm_g))
    ok = (bound <= MAX_FIXED_SHIFT).astype(_F32)
    return jnp.stack([jnp.where(ok > 0.5, bound, 0.0), ok]).astype(_F32)


def kernel(x, c, ctx, c_ctx, w_ada, b_ada, norm_g, w_in, q_norm_g, k_norm_g, lam_q1, lam_k1,
           lam_q2, lam_k2, sub_norm_g, conv_w, conv_b, conv_ln_g, conv_ln_b, w_out):
    assert x.shape == (1, SEQ, D_MODEL) and ctx.shape == (1, CTX_LEN, D_MODEL)
    assert w_ada.shape[0] == 1 and w_in.shape == (1, D_MODEL, IN_W)
    x2 = x[0]
    ctx2 = ctx[0]

    cc = jnp.zeros((MOD_ROWS, D_MODEL), _F32).at[0].set(c[0]).at[1].set(c_ctx)
    mod = _adaln(cc, w_ada[0], b_ada)
    h = _modulate(x2, ctx2, mod, norm_g)

    w = w_in[0]
    cos, sa, sb = _rope_tables()
    q_scale = LOG2E / math.sqrt(HEAD_DIM)
    avg = _chunk_avg()
    reps = PROJ_TN // HEAD_DIM
    gq = jnp.tile(q_norm_g, (1, reps)) * q_scale
    gk = jnp.tile(k_norm_g, (1, reps))

    def qk_extras(g, tabs, tm):
        fixed = lambda j, i: (0, 0)
        tab = lambda j, i: (i, 0)
        return ([(g, (1, PROJ_TN), fixed), (avg, (NORM_GROUP, NORM_GROUP), fixed)]
                + [(t, (tm, 2 * HEAD_DIM), tab) for t in tabs])

    q = _proj_call(_qk_kernel, h, [(w, Q0)], qk_extras(gq, (cos, sa, sb), PROJ_TM),
                   rows=SEQ, tm=PROJ_TM, tn=PROJ_TN, n_col_tiles=1, out_dtype=_BF16, name="proj_q")
    k = _proj_call(_qk_kernel, h, [(w, K0)], qk_extras(gk, (cos, sa, sb), PROJ_TM_ALL),
                   rows=L_ALL, tm=PROJ_TM_ALL, tn=PROJ_TN, n_col_tiles=1, out_dtype=_BF16, name="proj_k")
    vt = _proj_call(_vt_kernel, h, [(w, V0)], [], rows=L_ALL, tm=PROJ_TM_ALL, tn=PROJ_TN,
                    n_col_tiles=1, out_dtype=_BF16, name="proj_vt",
                    out_spec=pl.BlockSpec((HEADS, 1, V_DIM, PROJ_TM_ALL), lambda j, i: (0, i, 0, 0)),
                    out_shape=jax.ShapeDtypeStruct((HEADS, N_KB, V_DIM, PROJ_TM_ALL), _BF16))
    gates = _proj_call(_silu_kernel, h, [(w, GA0)], [], rows=SEQ, tm=PROJ_TM, tn=PROJ_TN,
                       n_col_tiles=2, col_stride=(GC0 - GA0) // PROJ_TN, out_dtype=_BF16, name="proj_gates")
    yglu = _proj_call(_glu_kernel, h, [(w, U0), (w, U0 + CONV_W)], [], rows=SEQ, tm=PROJ_TM,
                      tn=GLU_TN, n_col_tiles=CONV_W // GLU_TN, out_dtype=_BF16, name="proj_glu")

    lamv = jnp.concatenate([lam_q1, lam_k1, lam_q2, lam_k2], axis=0)
    shift = _softmax_shift(q_norm_g, k_norm_g, q_scale)
    att = _attention(shift, lamv, q, k, vt, gates, sub_norm_g)

    out = _conv_out(yglu, conv_w[0], conv_b, conv_ln_g, conv_ln_b, gates, att, w_out[0], x2, mod)
    return out[None]
```

```python
import math

import jax
import jax.numpy as jnp
import numpy as np
from jax import lax
from jax.experimental import pallas as pl
from jax.experimental.pallas import tpu as pltpu

D_MODEL = 2048
SEQ = 8192
GRID_W = 64
CTX_LEN = 256
L_ALL = SEQ + CTX_LEN
HEADS = 8
HEAD_DIM = 64
V_DIM = 2 * HEAD_DIM
ATT_W = HEADS * V_DIM
CONV_W = D_MODEL - ATT_W
CONV_K = 31
CONV_HALO = 16
ROPE_BASE = 10000.0
EPS = 1e-6
LAM_INIT = 0.8 - 0.6 * math.exp(-0.3 * 0)
LOG2E = math.log2(math.e)

Q0 = 0
K0 = Q0 + HEADS * 2 * HEAD_DIM
V0 = K0 + HEADS * 2 * HEAD_DIM
GA0 = V0 + ATT_W
U0 = GA0 + ATT_W
GC0 = U0 + 2 * CONV_W
IN_W = GC0 + CONV_W

SUBLANES = 8
MOD_ROWS = SUBLANES
ADA_TN = 1024
MOD_TM = 1024
MOD_ROWS_PER_PASS = 64
PROJ_TM = 1024
QK_ROWS = 128
PROJ_TM_ALL = 768
PROJ_TN = 1024
GLU_TN = 512
ATT_TQ = 512
ATT_KB = PROJ_TM_ALL
N_KB = L_ALL // ATT_KB
FAST_KB = 11 * ATT_KB
OUT_TM = 256
CONV_LANES = 128
CONV_CHUNKS = CONV_W // CONV_LANES
OUT_LANES = D_MODEL // CONV_CHUNKS
CONV_FIRST = CONV_HALO - CONV_K // 2
CONV_SPAN_EXTRA = SUBLANES * ((CONV_FIRST + CONV_K - 1) // SUBLANES)
NORM_GROUP = 256
MIB = 1024 * 1024
V7X_VMEM_BYTES = 64 * MIB
VMEM_LIMIT = 48 * MIB
CONV_OUT_VMEM_LIMIT = V7X_VMEM_BYTES - 6 * MIB
MAX_FIXED_SHIFT = 40.0
SHIFT_MARGIN = 1.01

_F32 = jnp.float32
_BF16 = jnp.bfloat16


def _sigmoid(x):
    return 1.0 / (1.0 + jnp.exp(-x))


def _silu(x):
    return x * _sigmoid(x)


def _adaln_kernel(cc_ref, w_ref, b_ref, o_ref):
    s = _silu(cc_ref[...]).astype(_BF16)
    o_ref[...] = jnp.dot(s, w_ref[...].astype(_BF16), preferred_element_type=_F32) + b_ref[...]


def _adaln(cc, w, b):
    n = w.shape[1]
    return pl.pallas_call(
        _adaln_kernel,
        grid=(n // ADA_TN,),
        in_specs=[pl.BlockSpec((MOD_ROWS, D_MODEL), lambda j: (0, 0)),
                  pl.BlockSpec((D_MODEL, ADA_TN), lambda j: (0, j)),
                  pl.BlockSpec((1, ADA_TN), lambda j: (0, j))],
        out_specs=pl.BlockSpec((MOD_ROWS, ADA_TN), lambda j: (0, j)),
        out_shape=jax.ShapeDtypeStruct((MOD_ROWS, n), _F32),
        compiler_params=pltpu.CompilerParams(vmem_limit_bytes=VMEM_LIMIT),
        name="adaln",
    )(cc, w, b)


def _modulate_rows(xin, g, shift, scale):
    r = lax.rsqrt(jnp.mean(xin * xin, axis=-1, keepdims=True) + EPS)
    return ((xin * r) * g) * (1.0 + scale) + shift


def _modulate_kernel(x_ref, ctx_ref, mod_ref, g_ref, h_ref):
    i = pl.program_id(0)
    n_lat = pl.num_programs(0) - 1
    g = g_ref[...]

    def run(src_ref, mod_row, n_rows):
        shift = mod_ref[mod_row:mod_row + 1, 0:D_MODEL]
        scale = mod_ref[mod_row:mod_row + 1, D_MODEL:2 * D_MODEL]

        def rows(r, carry):
            sl = pl.ds(pl.multiple_of(r * MOD_ROWS_PER_PASS, MOD_ROWS_PER_PASS), MOD_ROWS_PER_PASS)
            h_ref[sl, :] = _modulate_rows(src_ref[sl, :], g, shift, scale).astype(h_ref.dtype)
            return carry

        lax.fori_loop(0, n_rows // MOD_ROWS_PER_PASS, rows, 0)

    @pl.when(i < n_lat)
    def _():
        run(x_ref, 0, MOD_TM)

    @pl.when(i == n_lat)
    def _():
        run(ctx_ref, 1, CTX_LEN)


def _modulate(x2, ctx2, mod, g):
    n_lat = SEQ // MOD_TM
    return pl.pallas_call(
        _modulate_kernel,
        grid=(n_lat + 1,),
        in_specs=[pl.BlockSpec((MOD_TM, D_MODEL), lambda i: (jnp.minimum(i, n_lat - 1), 0)),
                  pl.BlockSpec((CTX_LEN, D_MODEL), lambda i: (0, 0)),
                  pl.BlockSpec((MOD_ROWS, 3 * D_MODEL), lambda i: (0, 0)),
                  pl.BlockSpec((1, D_MODEL), lambda i: (0, 0))],
        out_specs=pl.BlockSpec((MOD_TM, D_MODEL), lambda i: (i, 0)),
        out_shape=jax.ShapeDtypeStruct((L_ALL, D_MODEL), _BF16),
        compiler_params=pltpu.CompilerParams(vmem_limit_bytes=VMEM_LIMIT),
        name="modulate",
    )(x2, ctx2, mod, g)


def _proj_call(kernel, h, w_list, extras, *, rows, tm, tn, n_col_tiles, out_dtype, name,
               out_spec=None, out_shape=None, col_stride=1):
    in_specs = [pl.BlockSpec((tm, D_MODEL), lambda j, i: (i, 0))]
    args = [h]
    for w, c0 in w_list:
        in_specs.append(pl.BlockSpec((D_MODEL, tn), lambda j, i, c0=c0: (0, c0 // tn + j * col_stride)))
        args.append(w)
    for a, bs, im in extras:
        in_specs.append(pl.BlockSpec(bs, im))
        args.append(a)
    if out_spec is None:
        out_spec = pl.BlockSpec((tm, tn), lambda j, i: (i, j))
        out_shape = jax.ShapeDtypeStruct((rows, n_col_tiles * tn), out_dtype)
    return pl.pallas_call(
        kernel,
        grid=(n_col_tiles, rows // tm),
        in_specs=in_specs,
        out_specs=out_spec,
        out_shape=out_shape,
        compiler_params=pltpu.CompilerParams(vmem_limit_bytes=VMEM_LIMIT),
        name=name,
    )(*args)


def _row_parts(h_ref):
    return [slice(r0, r0 + QK_ROWS) for r0 in range(0, h_ref.shape[0], QK_ROWS)]


def _vt_kernel(h_ref, w_ref, o_ref):
    w = w_ref[...].astype(_BF16)
    for rows in _row_parts(h_ref):
        p = jnp.dot(h_ref[rows, :], w, preferred_element_type=_F32)
        pt = p.T.astype(o_ref.dtype)
        for hd in range(HEADS):
            o_ref[hd, 0, :, rows] = pt[hd * V_DIM:(hd + 1) * V_DIM, :]


def _silu_kernel(h_ref, w_ref, o_ref):
    w = w_ref[...].astype(_BF16)
    for rows in _row_parts(h_ref):
        p = jnp.dot(h_ref[rows, :], w, preferred_element_type=_F32)
        o_ref[rows, :] = _silu(p).astype(o_ref.dtype)


def _glu_kernel(h_ref, wa_ref, wg_ref, o_ref):
    wa = wa_ref[...].astype(_BF16)
    wg = wg_ref[...].astype(_BF16)
    for rows in _row_parts(h_ref):
        h = h_ref[rows, :]
        a = jnp.dot(h, wa, preferred_element_type=_F32)
        g = jnp.dot(h, wg, preferred_element_type=_F32)
        o_ref[rows, :] = (a * _sigmoid(g)).astype(o_ref.dtype)


def _qk_kernel(h_ref, w_ref, g_ref, avg_ref, cos_ref, sa_ref, sb_ref, o_ref):
    w = w_ref[...].astype(_BF16)
    avg = avg_ref[...]
    g = g_ref[...]
    tn = w.shape[1]
    reps = tn // cos_ref.shape[1]
    quarter = HEAD_DIM // 4
    for rows in _row_parts(h_ref):
        p = jnp.dot(h_ref[rows, :], w, preferred_element_type=_F32)
        ss = p * p
        hi = ss.astype(_BF16)
        lo = (ss - hi.astype(_F32)).astype(_BF16)
        ms = []
        for c in range(tn // NORM_GROUP):
            sl = slice(c * NORM_GROUP, (c + 1) * NORM_GROUP)
            ms.append(jnp.dot(hi[:, sl], avg, preferred_element_type=_F32)
                      + jnp.dot(lo[:, sl], avg, preferred_element_type=_F32))
        ms = jnp.concatenate(ms, axis=1)
        y = (p * lax.rsqrt(ms + EPS)) * g
        cos = jnp.concatenate([cos_ref[rows, :]] * reps, axis=1)
        sa = jnp.concatenate([sa_ref[rows, :]] * reps, axis=1)
        sb = jnp.concatenate([sb_ref[rows, :]] * reps, axis=1)
        out = y * cos + pltpu.roll(y, tn - quarter, 1) * sa + pltpu.roll(y, quarter, 1) * sb
        o_ref[rows, :] = out.astype(o_ref.dtype)


def _conv_chunk(c, tile, n_tiles, y_ref, yp_ref, yn_ref, cw_ref, cb_ref, win_ref, shift_ref):
    tm = y_ref.shape[0]
    halo = CONV_HALO
    span = shift_ref.shape[2]
    keep_prev = (tile > 0).astype(_F32)
    keep_next = (tile < n_tiles - 1).astype(_F32)
    cols = slice(c * CONV_LANES, (c + 1) * CONV_LANES)
    win_ref[c, 0:halo, :] = yp_ref[:, cols].astype(_F32) * keep_prev
    win_ref[c, halo:halo + tm, :] = y_ref[:, cols].astype(_F32)
    win_ref[c, halo + tm:, :] = yn_ref[:, cols].astype(_F32) * keep_next
    for res in range(SUBLANES):
        shift_ref[c, res] = win_ref[c, res:res + span, :]
    part = jnp.zeros((tm, CONV_LANES), _F32) + cb_ref[c]
    for t in range(CONV_K):
        res = (CONV_FIRST + t) % SUBLANES
        a = (CONV_FIRST + t) - res
        part = part + shift_ref[c, res, a:a + tm, :] * cw_ref[c, t:t + 1, :]
    return part


def _conv_tail(parts, lng_ref, lnb_ref, gc_ref):
    acc = jnp.concatenate(parts, axis=1)
    mu = jnp.mean(acc, axis=-1, keepdims=True)
    cen = acc - mu
    var = jnp.mean(cen * cen, axis=-1, keepdims=True)
    ln = (cen * lax.rsqrt(var + EPS)) * lng_ref[...] + lnb_ref[...]
    return (_silu(ln) * gc_ref[...].astype(_F32)).astype(_BF16)


def _attn_kernel(shift_ref, lamv_ref, q_ref, k_ref, vt_ref, ga_ref, subg_ref, o_ref):
    tq = q_ref.shape[0]
    qt = q_ref[...].T
    row = lax.broadcasted_iota(jnp.int32, qt.shape, 0)
    zero = jnp.zeros_like(qt)
    qst = jnp.concatenate([jnp.where(row < HEAD_DIM, qt, zero),
                           jnp.where(row >= HEAD_DIM, qt, zero)], axis=1)
    groups = ATT_KB // SUBLANES

    def scores(kb):
        start = kb * ATT_KB if isinstance(kb, int) else pl.multiple_of(kb * ATT_KB, ATT_KB)
        return jnp.dot(k_ref[pl.ds(start, ATT_KB), :], qst, preferred_element_type=_F32)

    def finish(l, acc):
        o = acc / jnp.sum(l, axis=0, keepdims=True)
        lv = lamv_ref[...]
        lam = (jnp.exp(jnp.sum(lv[0:1] * lv[1:2], axis=1, keepdims=True))
               - jnp.exp(jnp.sum(lv[2:3] * lv[3:4], axis=1, keepdims=True)) + LAM_INIT)
        od = o[:, :tq] - lam * o[:, tq:]
        r = lax.rsqrt(jnp.mean(od * od, axis=0, keepdims=True) + EPS)
        on = ((od * r).T * subg_ref[...]) * (1.0 - LAM_INIT)
        o_ref[...] = (on * ga_ref[...].astype(_F32)).astype(o_ref.dtype)

    l0 = jnp.zeros((SUBLANES, 2 * tq), _F32)
    a0 = jnp.zeros((V_DIM, 2 * tq), _F32)
    fixed_ok = shift_ref[1] > 0.5

    @pl.when(fixed_ok)
    def _():
        shift = shift_ref[0]
        l, acc = l0, a0
        for start in range(0, L_ALL, FAST_KB):
            size = min(FAST_KB, L_ALL - start)
            st = jnp.dot(k_ref[start:start + size, :], qst, preferred_element_type=_F32)
            p = jnp.exp2(st - shift)
            l = l + jnp.sum(p.reshape(size // SUBLANES, SUBLANES, 2 * tq), axis=0)
            pb = p.astype(_BF16)
            for off in range(0, size, ATT_KB):
                acc = acc + jnp.dot(vt_ref[0, (start + off) // ATT_KB], pb[off:off + ATT_KB, :],
                                    preferred_element_type=_F32)
        finish(l, acc)

    @pl.when(jnp.logical_not(fixed_ok))
    def _():
        def body(kb, carry):
            m, l, acc = carry
            st = scores(kb)
            mb = jnp.max(jnp.max(st.reshape(groups, SUBLANES, 2 * tq), axis=0), axis=0, keepdims=True)
            m_new = jnp.maximum(m, mb)
            alpha = jnp.exp2(m - m_new)
            p = jnp.exp2(st - m_new)
            l = alpha * l + jnp.sum(p.reshape(groups, SUBLANES, 2 * tq), axis=0)
            acc = alpha * acc + jnp.dot(vt_ref[0, kb], p.astype(_BF16), preferred_element_type=_F32)
            return m_new, l, acc

        m0 = jnp.full((1, 2 * tq), -1e30, _F32)
        _, l, acc = lax.fori_loop(0, N_KB, body, (m0, l0, a0))
        finish(l, acc)


def _attention(shift, lamv, q, k, vt, ga, subg):
    qrow = lambda h, i: (i, h)
    return pl.pallas_call(
        _attn_kernel,
        grid=(HEADS, SEQ // ATT_TQ),
        in_specs=[pl.BlockSpec(memory_space=pltpu.SMEM),
                  pl.BlockSpec(lamv.shape, lambda h, i: (0, 0)),
                  pl.BlockSpec((ATT_TQ, V_DIM), qrow),
                  pl.BlockSpec((L_ALL, V_DIM), lambda h, i: (0, h)),
                  pl.BlockSpec((1, N_KB, V_DIM, ATT_KB), lambda h, i: (h, 0, 0, 0)),
                  pl.BlockSpec((ATT_TQ, V_DIM), qrow),
                  pl.BlockSpec((1, V_DIM), lambda h, i: (0, 0))],
        out_specs=pl.BlockSpec((ATT_TQ, V_DIM), qrow),
        out_shape=jax.ShapeDtypeStruct((SEQ, ATT_W), _BF16),
        compiler_params=pltpu.CompilerParams(vmem_limit_bytes=VMEM_LIMIT),
        name="diff_attn",
    )(shift, lamv, q, k, vt, ga, subg)


def _conv_out_kernel(y_ref, yp_ref, yn_ref, cw_ref, cb_ref, lng_ref, lnb_ref, gc_ref, att_ref,
                     w_ref, x_ref, mod_ref, o_ref, win_ref, shift_ref, conv_ref, wbf_ref):
    i = pl.program_id(0)
    n_tiles = pl.num_programs(0) - 1
    tile = jnp.minimum(i, n_tiles - 1)

    @pl.when(i == 0)
    def _():
        conv_ref[1] = jnp.zeros(conv_ref.shape[1:], conv_ref.dtype)
        wbf_ref[...] = w_ref[...].astype(wbf_ref.dtype)

    att = att_ref[...]
    conv_prev = conv_ref[(i + 1) % 2]
    ys, parts = [], []
    for c in range(CONV_CHUNKS):
        cols = slice(c * OUT_LANES, (c + 1) * OUT_LANES)
        ys.append(jnp.dot(att, wbf_ref[0:ATT_W, cols], preferred_element_type=_F32)
                  + jnp.dot(conv_prev, wbf_ref[ATT_W:, cols], preferred_element_type=_F32))
        parts.append(_conv_chunk(c, tile, n_tiles, y_ref, yp_ref, yn_ref, cw_ref, cb_ref,
                                 win_ref, shift_ref))
    gate = mod_ref[0:1, 2 * D_MODEL:3 * D_MODEL]
    o_ref[...] = x_ref[...] + gate * jnp.concatenate(ys, axis=1)
    conv_ref[i % 2] = _conv_tail(parts, lng_ref, lnb_ref, gc_ref)


def _conv_out(yglu, cw, cb, lng, lnb, gc, att, w_out, x2, mod):
    tm = OUT_TM
    n_tiles = SEQ // tm
    nb = tm // CONV_HALO
    last = SEQ // CONV_HALO - 1
    cur = lambda i: (jnp.minimum(i, n_tiles - 1), 0)
    prev = lambda i: (jnp.maximum(i - 1, 0), 0)
    fixed = lambda i: (0, 0)
    fixed3 = lambda i: (0, 0, 0)
    cw3 = cw.reshape(CONV_K, CONV_CHUNKS, CONV_LANES).transpose(1, 0, 2)
    cb3 = cb.reshape(CONV_CHUNKS, 1, CONV_LANES)
    return pl.pallas_call(
        _conv_out_kernel,
        grid=(n_tiles + 1,),
        in_specs=[pl.BlockSpec((tm, CONV_W), cur),
                  pl.BlockSpec((CONV_HALO, CONV_W),
                               lambda i: (jnp.maximum(jnp.minimum(i, n_tiles - 1) * nb - 1, 0), 0)),
                  pl.BlockSpec((CONV_HALO, CONV_W),
                               lambda i: (jnp.minimum((jnp.minimum(i, n_tiles - 1) + 1) * nb, last), 0)),
                  pl.BlockSpec((CONV_CHUNKS, CONV_K, CONV_LANES), fixed3),
                  pl.BlockSpec((CONV_CHUNKS, 1, CONV_LANES), fixed3),
                  pl.BlockSpec((1, CONV_W), fixed),
                  pl.BlockSpec((1, CONV_W), fixed),
                  pl.BlockSpec((tm, CONV_W), lambda i: (jnp.minimum(i, n_tiles - 1), 1)),
                  pl.BlockSpec((tm, ATT_W), prev),
                  pl.BlockSpec((D_MODEL, D_MODEL), fixed, pipeline_mode=pl.Buffered(1)),
                  pl.BlockSpec((tm, D_MODEL), prev),
                  pl.BlockSpec((MOD_ROWS, 3 * D_MODEL), fixed)],
        out_specs=pl.BlockSpec((tm, D_MODEL), prev),
        out_shape=jax.ShapeDtypeStruct((SEQ, D_MODEL), _F32),
        scratch_shapes=[pltpu.VMEM((CONV_CHUNKS, tm + 2 * CONV_HALO, CONV_LANES), _F32),
                        pltpu.VMEM((CONV_CHUNKS, SUBLANES, tm + CONV_SPAN_EXTRA, CONV_LANES), _F32),
                        pltpu.VMEM((2, tm, CONV_W), _BF16),
                        pltpu.VMEM((D_MODEL, D_MODEL), _BF16)],
        compiler_params=pltpu.CompilerParams(vmem_limit_bytes=CONV_OUT_VMEM_LIMIT,
                                             dimension_semantics=("arbitrary",)),
        name="conv_out",
    )(yglu, yglu, yglu, cw3, cb3, lng, lnb, gc, att, w_out, x2, mod)


def _rope_tables():
    rows = SEQ // GRID_W
    row = np.repeat(np.arange(rows), GRID_W).astype(np.float64)
    col = np.tile(np.arange(GRID_W), rows).astype(np.float64)
    nf = HEAD_DIM // 4
    inv = ROPE_BASE ** (-np.arange(nf, dtype=np.float64) / nf)
    ang_r = row[:, None] * inv
    ang_c = col[:, None] * inv
    ang = np.concatenate([ang_r, ang_r, ang_c, ang_c], axis=-1)
    cos, sin = np.cos(ang), np.sin(ang)
    first_half = (np.arange(HEAD_DIM) % (2 * nf)) < nf
    sa = np.where(first_half, -sin, 0.0)
    sb = np.where(first_half, 0.0, sin)
    pad = lambda t, v: np.concatenate([t, np.full((CTX_LEN, HEAD_DIM), v)], axis=0)
    two = lambda t: jnp.asarray(np.concatenate([t, t], axis=1), _F32)
    return two(pad(cos, 1.0)), two(pad(sa, 0.0)), two(pad(sb, 0.0))


def _chunk_avg():
    r = np.arange(NORM_GROUP) // HEAD_DIM
    return jnp.asarray(np.where(r[:, None] == r[None, :], 1.0 / HEAD_DIM, 0.0), _BF16)


def _softmax_shift(q_norm_g, k_norm_g, q_scale):
    bound = (HEAD_DIM * q_scale * SHIFT_MARGIN) * jnp.max(jnp.abs(q_norm_g)) * jnp.max(jnp.abs(k_norm_g))
    ok = (bound <= MAX_FIXED_SHIFT).astype(_F32)
    return jnp.stack([jnp.where(ok > 0.5, bound, 0.0), ok]).astype(_F32)


def kernel(x, c, ctx, c_ctx, w_ada, b_ada, norm_g, w_in, q_norm_g, k_norm_g, lam_q1, lam_k1,
           lam_q2, lam_k2, sub_norm_g, conv_w, conv_b, conv_ln_g, conv_ln_b, w_out):
    assert x.shape == (1, SEQ, D_MODEL) and ctx.shape == (1, CTX_LEN, D_MODEL)
    assert w_ada.shape[0] == 1 and w_in.shape == (1, D_MODEL, IN_W)
    x2 = x[0]
    ctx2 = ctx[0]

    cc = jnp.zeros((MOD_ROWS, D_MODEL), _F32).at[0].set(c[0]).at[1].set(c_ctx)
    mod = _adaln(cc, w_ada[0], b_ada)
    h = _modulate(x2, ctx2, mod, norm_g)

    w = w_in[0]
    cos, sa, sb = _rope_tables()
    q_scale = LOG2E / math.sqrt(HEAD_DIM)
    avg = _chunk_avg()
    reps = PROJ_TN // HEAD_DIM
    gq = jnp.tile(q_norm_g, (1, reps)) * q_scale
    gk = jnp.tile(k_norm_g, (1, reps))

    def qk_extras(g, tabs, tm):
        fixed = lambda j, i: (0, 0)
        tab = lambda j, i: (i, 0)
        return ([(g, (1, PROJ_TN), fixed), (avg, (NORM_GROUP, NORM_GROUP), fixed)]
                + [(t, (tm, 2 * HEAD_DIM), tab) for t in tabs])

    q = _proj_call(_qk_kernel, h, [(w, Q0)], qk_extras(gq, (cos, sa, sb), PROJ_TM),
                   rows=SEQ, tm=PROJ_TM, tn=PROJ_TN, n_col_tiles=1, out_dtype=_BF16, name="proj_q")
    k = _proj_call(_qk_kernel, h, [(w, K0)], qk_extras(gk, (cos, sa, sb), PROJ_TM_ALL),
                   rows=L_ALL, tm=PROJ_TM_ALL, tn=PROJ_TN, n_col_tiles=1, out_dtype=_BF16, name="proj_k")
    vt = _proj_call(_vt_kernel, h, [(w, V0)], [], rows=L_ALL, tm=PROJ_TM_ALL, tn=PROJ_TN,
                    n_col_tiles=1, out_dtype=_BF16, name="proj_vt",
                    out_spec=pl.BlockSpec((HEADS, 1, V_DIM, PROJ_TM_ALL), lambda j, i: (0, i, 0, 0)),
                    out_shape=jax.ShapeDtypeStruct((HEADS, N_KB, V_DIM, PROJ_TM_ALL), _BF16))
    gates = _proj_call(_silu_kernel, h, [(w, GA0)], [], rows=SEQ, tm=PROJ_TM, tn=PROJ_TN,
                       n_col_tiles=2, col_stride=(GC0 - GA0) // PROJ_TN, out_dtype=_BF16, name="proj_gates")
    yglu = _proj_call(_glu_kernel, h, [(w, U0), (w, U0 + CONV_W)], [], rows=SEQ, tm=PROJ_TM,
                      tn=GLU_TN, n_col_tiles=CONV_W // GLU_TN, out_dtype=_BF16, name="proj_glu")

    lamv = jnp.concatenate([lam_q1, lam_k1, lam_q2, lam_k2], axis=0)
    shift = _softmax_shift(q_norm_g, k_norm_g, q_scale)
    att = _attention(shift, lamv, q, k, vt, gates, sub_norm_g)

    out = _conv_out(yglu, conv_w[0], conv_b, conv_ln_g, conv_ln_b, gates, att, w_out[0], x2, mod)
    return out[None]
```

```python
import math

import jax
import jax.numpy as jnp
import numpy as np
from jax import lax
from jax.experimental import pallas as pl
from jax.experimental.pallas import tpu as pltpu

D_MODEL = 2048
SEQ = 8192
GRID_W = 64
CTX_LEN = 256
L_ALL = SEQ + CTX_LEN
HEADS = 8
HEAD_DIM = 64
V_DIM = 2 * HEAD_DIM
ATT_W = HEADS * V_DIM
CONV_W = D_MODEL - ATT_W
CONV_K = 31
CONV_HALO = 16
ROPE_BASE = 10000.0
EPS = 1e-6
LAM_INIT = 0.8 - 0.6 * math.exp(-0.3 * 0)
LOG2E = math.log2(math.e)

Q0 = 0
K0 = Q0 + HEADS * 2 * HEAD_DIM
V0 = K0 + HEADS * 2 * HEAD_DIM
GA0 = V0 + ATT_W
U0 = GA0 + ATT_W
GC0 = U0 + 2 * CONV_W
IN_W = GC0 + CONV_W

SUBLANES = 8
MOD_ROWS = SUBLANES
ADA_TN = 1024
MOD_TM = 1024
MOD_ROWS_PER_PASS = 64
PROJ_TM = 1024
QK_ROWS = 256
QG_TM = 512
PROJ_TM_ALL = 768
PROJ_TN = 1024
GLU_TN = 512
ATT_TQ = 512
ATT_KB = PROJ_TM_ALL
N_KB = L_ALL // ATT_KB
FAST_KB = 11 * ATT_KB
OUT_TM = 256
CONV_LANES = 128
CONV_CHUNKS = CONV_W // CONV_LANES
OUT_LANES = D_MODEL // CONV_CHUNKS
CONV_FIRST = CONV_HALO - CONV_K // 2
CONV_SPAN_EXTRA = SUBLANES * ((CONV_FIRST + CONV_K - 1) // SUBLANES)
NORM_GROUP = 256
MIB = 1024 * 1024
V7X_VMEM_BYTES = 64 * MIB
VMEM_LIMIT = 48 * MIB
CONV_OUT_VMEM_LIMIT = V7X_VMEM_BYTES - 6 * MIB
MAX_FIXED_SHIFT = 40.0
SHIFT_MARGIN = 1.01

_F32 = jnp.float32
_BF16 = jnp.bfloat16


def _sigmoid(x):
    return 1.0 / (1.0 + jnp.exp(-x))


def _silu(x):
    return x * _sigmoid(x)


def _adaln_kernel(cc_ref, w_ref, b_ref, o_ref):
    s = _silu(cc_ref[...]).astype(_BF16)
    o_ref[...] = jnp.dot(s, w_ref[...].astype(_BF16), preferred_element_type=_F32) + b_ref[...]


def _adaln(cc, w, b):
    n = w.shape[1]
    return pl.pallas_call(
        _adaln_kernel,
        grid=(n // ADA_TN,),
        in_specs=[pl.BlockSpec((MOD_ROWS, D_MODEL), lambda j: (0, 0)),
                  pl.BlockSpec((D_MODEL, ADA_TN), lambda j: (0, j)),
                  pl.BlockSpec((1, ADA_TN), lambda j: (0, j))],
        out_specs=pl.BlockSpec((MOD_ROWS, ADA_TN), lambda j: (0, j)),
        out_shape=jax.ShapeDtypeStruct((MOD_ROWS, n), _F32),
        compiler_params=pltpu.CompilerParams(vmem_limit_bytes=VMEM_LIMIT),
        name="adaln",
    )(cc, w, b)


def _modulate_rows(xin, g, shift, scale):
    r = lax.rsqrt(jnp.mean(xin * xin, axis=-1, keepdims=True) + EPS)
    return ((xin * r) * g) * (1.0 + scale) + shift


def _modulate_kernel(x_ref, ctx_ref, mod_ref, g_ref, h_ref):
    i = pl.program_id(0)
    n_lat = pl.num_programs(0) - 1
    g = g_ref[...]

    def run(src_ref, mod_row, n_rows):
        shift = mod_ref[mod_row:mod_row + 1, 0:D_MODEL]
        scale = mod_ref[mod_row:mod_row + 1, D_MODEL:2 * D_MODEL]

        def rows(r, carry):
            sl = pl.ds(pl.multiple_of(r * MOD_ROWS_PER_PASS, MOD_ROWS_PER_PASS), MOD_ROWS_PER_PASS)
            h_ref[sl, :] = _modulate_rows(src_ref[sl, :], g, shift, scale).astype(h_ref.dtype)
            return carry

        lax.fori_loop(0, n_rows // MOD_ROWS_PER_PASS, rows, 0)

    @pl.when(i < n_lat)
    def _():
        run(x_ref, 0, MOD_TM)

    @pl.when(i == n_lat)
    def _():
        run(ctx_ref, 1, CTX_LEN)


def _modulate(x2, ctx2, mod, g):
    n_lat = SEQ // MOD_TM
    return pl.pallas_call(
        _modulate_kernel,
        grid=(n_lat + 1,),
        in_specs=[pl.BlockSpec((MOD_TM, D_MODEL), lambda i: (jnp.minimum(i, n_lat - 1), 0)),
                  pl.BlockSpec((CTX_LEN, D_MODEL), lambda i: (0, 0)),
                  pl.BlockSpec((MOD_ROWS, 3 * D_MODEL), lambda i: (0, 0)),
                  pl.BlockSpec((1, D_MODEL), lambda i: (0, 0))],
        out_specs=pl.BlockSpec((MOD_TM, D_MODEL), lambda i: (i, 0)),
        out_shape=jax.ShapeDtypeStruct((L_ALL, D_MODEL), _BF16),
        compiler_params=pltpu.CompilerParams(vmem_limit_bytes=VMEM_LIMIT),
        name="modulate",
    )(x2, ctx2, mod, g)


def _row_parts(h_ref):
    return [slice(r0, r0 + QK_ROWS) for r0 in range(0, h_ref.shape[0], QK_ROWS)]


def _glu_kernel(h_ref, wa_ref, wg_ref, o_ref):
    wa = wa_ref[...].astype(_BF16)
    wg = wg_ref[...].astype(_BF16)
    for rows in _row_parts(h_ref):
        h = h_ref[rows, :]
        a = jnp.dot(h, wa, preferred_element_type=_F32)
        g = jnp.dot(h, wg, preferred_element_type=_F32)
        o_ref[rows, :] = (a * _sigmoid(g)).astype(o_ref.dtype)


def _glu_proj(h, w):
    tm, tn = PROJ_TM, GLU_TN
    wspec = lambda c0: pl.BlockSpec((D_MODEL, tn), lambda j, i: (0, c0 // tn + j))
    return pl.pallas_call(
        _glu_kernel,
        grid=(CONV_W // tn, SEQ // tm),
        in_specs=[pl.BlockSpec((tm, D_MODEL), lambda j, i: (i, 0)), wspec(U0), wspec(U0 + CONV_W)],
        out_specs=pl.BlockSpec((tm, tn), lambda j, i: (i, j)),
        out_shape=jax.ShapeDtypeStruct((SEQ, CONV_W), _BF16),
        compiler_params=pltpu.CompilerParams(vmem_limit_bytes=VMEM_LIMIT),
        name="proj_glu",
    )(h, w, w)


def _qk_epilogue(p, g, avg, cos, sa, sb):
    tn = p.shape[1]
    ss = p * p
    hi = ss.astype(_BF16)
    lo = (ss - hi.astype(_F32)).astype(_BF16)
    ms = []
    for c in range(tn // NORM_GROUP):
        sl = slice(c * NORM_GROUP, (c + 1) * NORM_GROUP)
        ms.append(jnp.dot(hi[:, sl], avg, preferred_element_type=_F32)
                  + jnp.dot(lo[:, sl], avg, preferred_element_type=_F32))
    ms = jnp.concatenate(ms, axis=1)
    y = (p * lax.rsqrt(ms + EPS)) * g
    reps = tn // cos.shape[1]
    cos, sa, sb = (jnp.concatenate([t] * reps, axis=1) for t in (cos, sa, sb))
    quarter = HEAD_DIM // 4
    return y * cos + pltpu.roll(y, tn - quarter, 1) * sa + pltpu.roll(y, quarter, 1) * sb


def _kv_kernel(h_ref, wk_ref, wv_ref, g_ref, avg_ref, cos_ref, sa_ref, sb_ref, k_ref, vt_ref):
    wk = wk_ref[...].astype(_BF16)
    wv = wv_ref[...].astype(_BF16)
    avg = avg_ref[...]
    g = g_ref[...]
    for rows in _row_parts(h_ref):
        hp = h_ref[rows, :]
        pk = jnp.dot(hp, wk, preferred_element_type=_F32)
        out = _qk_epilogue(pk, g, avg, cos_ref[rows, :], sa_ref[rows, :], sb_ref[rows, :])
        k_ref[rows, :] = out.astype(k_ref.dtype)
        pt = jnp.dot(hp, wv, preferred_element_type=_F32).T.astype(vt_ref.dtype)
        for hd in range(HEADS):
            vt_ref[hd, 0, :, rows] = pt[hd * V_DIM:(hd + 1) * V_DIM, :]


def _qg_kernel(h_ref, wq_ref, wga_ref, wgc_ref, g_ref, avg_ref, cos_ref, sa_ref, sb_ref, q_ref, gates_ref):
    wq = wq_ref[...].astype(_BF16)
    wga = wga_ref[...].astype(_BF16)
    wgc = wgc_ref[...].astype(_BF16)
    avg = avg_ref[...]
    g = g_ref[...]
    tn = wq.shape[1]
    for rows in _row_parts(h_ref):
        hp = h_ref[rows, :]
        pq = jnp.dot(hp, wq, preferred_element_type=_F32)
        out = _qk_epilogue(pq, g, avg, cos_ref[rows, :], sa_ref[rows, :], sb_ref[rows, :])
        q_ref[rows, :] = out.astype(q_ref.dtype)
        gates_ref[rows, 0:tn] = _silu(jnp.dot(hp, wga, preferred_element_type=_F32)).astype(gates_ref.dtype)
        gates_ref[rows, tn:] = _silu(jnp.dot(hp, wgc, preferred_element_type=_F32)).astype(gates_ref.dtype)


def _qg_proj(h, w, gq, avg, cos, sa, sb):
    tm = QG_TM
    fixed = lambda i: (0, 0)
    row = lambda i: (i, 0)
    once = pl.Buffered(1)
    wspec = lambda c0: pl.BlockSpec((D_MODEL, PROJ_TN), lambda i: (0, c0 // PROJ_TN), pipeline_mode=once)
    return pl.pallas_call(
        _qg_kernel,
        grid=(SEQ // tm,),
        in_specs=[pl.BlockSpec((tm, D_MODEL), row), wspec(Q0), wspec(GA0), wspec(GC0),
                  pl.BlockSpec((1, PROJ_TN), fixed),
                  pl.BlockSpec((NORM_GROUP, NORM_GROUP), fixed),
                  pl.BlockSpec((tm, 2 * HEAD_DIM), row),
                  pl.BlockSpec((tm, 2 * HEAD_DIM), row),
                  pl.BlockSpec((tm, 2 * HEAD_DIM), row)],
        out_specs=[pl.BlockSpec((tm, PROJ_TN), row),
                   pl.BlockSpec((tm, 2 * PROJ_TN), row)],
        out_shape=[jax.ShapeDtypeStruct((SEQ, PROJ_TN), _BF16),
                   jax.ShapeDtypeStruct((SEQ, 2 * PROJ_TN), _BF16)],
        compiler_params=pltpu.CompilerParams(vmem_limit_bytes=VMEM_LIMIT),
        name="proj_qg",
    )(h, w, w, w, gq, avg, cos, sa, sb)


def _kv_proj(h, w, gk, avg, cos, sa, sb):
    tm = PROJ_TM_ALL
    fixed = lambda i: (0, 0)
    row = lambda i: (i, 0)
    once = pl.Buffered(1)
    return pl.pallas_call(
        _kv_kernel,
        grid=(L_ALL // tm,),
        in_specs=[pl.BlockSpec((tm, D_MODEL), row),
                  pl.BlockSpec((D_MODEL, PROJ_TN), lambda i: (0, K0 // PROJ_TN), pipeline_mode=once),
                  pl.BlockSpec((D_MODEL, PROJ_TN), lambda i: (0, V0 // PROJ_TN), pipeline_mode=once),
                  pl.BlockSpec((1, PROJ_TN), fixed),
                  pl.BlockSpec((NORM_GROUP, NORM_GROUP), fixed),
                  pl.BlockSpec((tm, 2 * HEAD_DIM), row),
                  pl.BlockSpec((tm, 2 * HEAD_DIM), row),
                  pl.BlockSpec((tm, 2 * HEAD_DIM), row)],
        out_specs=[pl.BlockSpec((tm, PROJ_TN), row),
                   pl.BlockSpec((HEADS, 1, V_DIM, tm), lambda i: (0, i, 0, 0))],
        out_shape=[jax.ShapeDtypeStruct((L_ALL, PROJ_TN), _BF16),
                   jax.ShapeDtypeStruct((HEADS, N_KB, V_DIM, tm), _BF16)],
        compiler_params=pltpu.CompilerParams(vmem_limit_bytes=VMEM_LIMIT),
        name="proj_kv",
    )(h, w, w, gk, avg, cos, sa, sb)


def _conv_chunk(c, tile, n_tiles, y_ref, yp_ref, yn_ref, cw_ref, cb_ref, win_ref, shift_ref):
    tm = y_ref.shape[0]
    halo = CONV_HALO
    span = shift_ref.shape[2]
    keep_prev = (tile > 0).astype(_F32)
    keep_next = (tile < n_tiles - 1).astype(_F32)
    cols = slice(c * CONV_LANES, (c + 1) * CONV_LANES)
    win_ref[c, 0:halo, :] = yp_ref[:, cols].astype(_F32) * keep_prev
    win_ref[c, halo:halo + tm, :] = y_ref[:, cols].astype(_F32)
    win_ref[c, halo + tm:, :] = yn_ref[:, cols].astype(_F32) * keep_next
    for res in range(SUBLANES):
        shift_ref[c, res] = win_ref[c, res:res + span, :]
    part = jnp.zeros((tm, CONV_LANES), _F32) + cb_ref[c]
    for t in range(CONV_K):
        res = (CONV_FIRST + t) % SUBLANES
        a = (CONV_FIRST + t) - res
        part = part + shift_ref[c, res, a:a + tm, :] * cw_ref[c, t:t + 1, :]
    return part


def _conv_tail(parts, lng_ref, lnb_ref, gc_ref):
    acc = jnp.concatenate(parts, axis=1)
    mu = jnp.mean(acc, axis=-1, keepdims=True)
    cen = acc - mu
    var = jnp.mean(cen * cen, axis=-1, keepdims=True)
    ln = (cen * lax.rsqrt(var + EPS)) * lng_ref[...] + lnb_ref[...]
    return (_silu(ln) * gc_ref[...].astype(_F32)).astype(_BF16)


def _attn_kernel(shift_ref, lamv_ref, q_ref, k_ref, vt_ref, ga_ref, subg_ref, o_ref):
    tq = q_ref.shape[0]
    qt = q_ref[...].T
    row = lax.broadcasted_iota(jnp.int32, qt.shape, 0)
    zero = jnp.zeros_like(qt)
    qst = jnp.concatenate([jnp.where(row < HEAD_DIM, qt, zero),
                           jnp.where(row >= HEAD_DIM, qt, zero)], axis=1)
    groups = ATT_KB // SUBLANES

    def scores(kb):
        start = kb * ATT_KB if isinstance(kb, int) else pl.multiple_of(kb * ATT_KB, ATT_KB)
        return jnp.dot(k_ref[pl.ds(start, ATT_KB), :], qst, preferred_element_type=_F32)

    def finish(l, acc):
        o = acc / jnp.sum(l, axis=0, keepdims=True)
        lv = lamv_ref[...]
        lam = (jnp.exp(jnp.sum(lv[0:1] * lv[1:2], axis=1, keepdims=True))
               - jnp.exp(jnp.sum(lv[2:3] * lv[3:4], axis=1, keepdims=True)) + LAM_INIT)
        od = o[:, :tq] - lam * o[:, tq:]
        r = lax.rsqrt(jnp.mean(od * od, axis=0, keepdims=True) + EPS)
        on = ((od * r).T * subg_ref[...]) * (1.0 - LAM_INIT)
        o_ref[...] = (on * ga_ref[...].astype(_F32)).astype(o_ref.dtype)

    l0 = jnp.zeros((SUBLANES, 2 * tq), _F32)
    a0 = jnp.zeros((V_DIM, 2 * tq), _F32)
    fixed_ok = shift_ref[1] > 0.5

    @pl.when(fixed_ok)
    def _():
        shift = shift_ref[0]
        l, acc = l0, a0
        for start in range(0, L_ALL, FAST_KB):
            size = min(FAST_KB, L_ALL - start)
            st = jnp.dot(k_ref[start:start + size, :], qst, preferred_element_type=_F32)
            p = jnp.exp2(st - shift)
            l = l + jnp.sum(p.reshape(size // SUBLANES, SUBLANES, 2 * tq), axis=0)
            pb = p.astype(_BF16)
            for off in range(0, size, ATT_KB):
                acc = acc + jnp.dot(vt_ref[0, (start + off) // ATT_KB], pb[off:off + ATT_KB, :],
                                    preferred_element_type=_F32)
        finish(l, acc)

    @pl.when(jnp.logical_not(fixed_ok))
    def _():
        def body(kb, carry):
            m, l, acc = carry
            st = scores(kb)
            mb = jnp.max(jnp.max(st.reshape(groups, SUBLANES, 2 * tq), axis=0), axis=0, keepdims=True)
            m_new = jnp.maximum(m, mb)
            alpha = jnp.exp2(m - m_new)
            p = jnp.exp2(st - m_new)
            l = alpha * l + jnp.sum(p.reshape(groups, SUBLANES, 2 * tq), axis=0)
            acc = alpha * acc + jnp.dot(vt_ref[0, kb], p.astype(_BF16), preferred_element_type=_F32)
            return m_new, l, acc

        m0 = jnp.full((1, 2 * tq), -1e30, _F32)
        _, l, acc = lax.fori_loop(0, N_KB, body, (m0, l0, a0))
        finish(l, acc)


def _attention(shift, lamv, q, k, vt, ga, subg):
    qrow = lambda h, i: (i, h)
    return pl.pallas_call(
        _attn_kernel,
        grid=(HEADS, SEQ // ATT_TQ),
        in_specs=[pl.BlockSpec(memory_space=pltpu.SMEM),
                  pl.BlockSpec(lamv.shape, lambda h, i: (0, 0)),
                  pl.BlockSpec((ATT_TQ, V_DIM), qrow),
                  pl.BlockSpec((L_ALL, V_DIM), lambda h, i: (0, h)),
                  pl.BlockSpec((1, N_KB, V_DIM, ATT_KB), lambda h, i: (h, 0, 0, 0)),
                  pl.BlockSpec((ATT_TQ, V_DIM), qrow),
                  pl.BlockSpec((1, V_DIM), lambda h, i: (0, 0))],
        out_specs=pl.BlockSpec((ATT_TQ, V_DIM), qrow),
        out_shape=jax.ShapeDtypeStruct((SEQ, ATT_W), _BF16),
        compiler_params=pltpu.CompilerParams(vmem_limit_bytes=VMEM_LIMIT),
        name="diff_attn",
    )(shift, lamv, q, k, vt, ga, subg)


def _conv_out_kernel(y_ref, yp_ref, yn_ref, cw_ref, cb_ref, lng_ref, lnb_ref, gc_ref, att_ref,
                     w_ref, x_ref, mod_ref, o_ref, win_ref, shift_ref, conv_ref, wbf_ref):
    i = pl.program_id(0)
    n_tiles = pl.num_programs(0) - 1
    tile = jnp.minimum(i, n_tiles - 1)

    @pl.when(i == 0)
    def _():
        conv_ref[1] = jnp.zeros(conv_ref.shape[1:], conv_ref.dtype)
        wbf_ref[...] = w_ref[...].astype(wbf_ref.dtype)

    att = att_ref[...]
    conv_prev = conv_ref[(i + 1) % 2]
    ys, parts = [], []
    for c in range(CONV_CHUNKS):
        cols = slice(c * OUT_LANES, (c + 1) * OUT_LANES)
        ys.append(jnp.dot(att, wbf_ref[0:ATT_W, cols], preferred_element_type=_F32)
                  + jnp.dot(conv_prev, wbf_ref[ATT_W:, cols], preferred_element_type=_F32))
        parts.append(_conv_chunk(c, tile, n_tiles, y_ref, yp_ref, yn_ref, cw_ref, cb_ref,
                                 win_ref, shift_ref))
    gate = mod_ref[0:1, 2 * D_MODEL:3 * D_MODEL]
    o_ref[...] = x_ref[...] + gate * jnp.concatenate(ys, axis=1)
    conv_ref[i % 2] = _conv_tail(parts, lng_ref, lnb_ref, gc_ref)


def _conv_out(yglu, cw, cb, lng, lnb, gc, att, w_out, x2, mod):
    tm = OUT_TM
    n_tiles = SEQ // tm
    nb = tm // CONV_HALO
    last = SEQ // CONV_HALO - 1
    cur = lambda i: (jnp.minimum(i, n_tiles - 1), 0)
    prev = lambda i: (jnp.maximum(i - 1, 0), 0)
    fixed = lambda i: (0, 0)
    fixed3 = lambda i: (0, 0, 0)
    cw3 = cw.reshape(CONV_K, CONV_CHUNKS, CONV_LANES).transpose(1, 0, 2)
    cb3 = cb.reshape(CONV_CHUNKS, 1, CONV_LANES)
    return pl.pallas_call(
        _conv_out_kernel,
        grid=(n_tiles + 1,),
        in_specs=[pl.BlockSpec((tm, CONV_W), cur),
                  pl.BlockSpec((CONV_HALO, CONV_W),
                               lambda i: (jnp.maximum(jnp.minimum(i, n_tiles - 1) * nb - 1, 0), 0)),
                  pl.BlockSpec((CONV_HALO, CONV_W),
                               lambda i: (jnp.minimum((jnp.minimum(i, n_tiles - 1) + 1) * nb, last), 0)),
                  pl.BlockSpec((CONV_CHUNKS, CONV_K, CONV_LANES), fixed3),
                  pl.BlockSpec((CONV_CHUNKS, 1, CONV_LANES), fixed3),
                  pl.BlockSpec((1, CONV_W), fixed),
                  pl.BlockSpec((1, CONV_W), fixed),
                  pl.BlockSpec((tm, CONV_W), lambda i: (jnp.minimum(i, n_tiles - 1), 1)),
                  pl.BlockSpec((tm, ATT_W), prev),
                  pl.BlockSpec((D_MODEL, D_MODEL), fixed, pipeline_mode=pl.Buffered(1)),
                  pl.BlockSpec((tm, D_MODEL), prev),
                  pl.BlockSpec((MOD_ROWS, 3 * D_MODEL), fixed)],
        out_specs=pl.BlockSpec((tm, D_MODEL), prev),
        out_shape=jax.ShapeDtypeStruct((SEQ, D_MODEL), _F32),
        scratch_shapes=[pltpu.VMEM((CONV_CHUNKS, tm + 2 * CONV_HALO, CONV_LANES), _F32),
                        pltpu.VMEM((CONV_CHUNKS, SUBLANES, tm + CONV_SPAN_EXTRA, CONV_LANES), _F32),
                        pltpu.VMEM((2, tm, CONV_W), _BF16),
                        pltpu.VMEM((D_MODEL, D_MODEL), _BF16)],
        compiler_params=pltpu.CompilerParams(vmem_limit_bytes=CONV_OUT_VMEM_LIMIT,
                                             dimension_semantics=("arbitrary",)),
        name="conv_out",
    )(yglu, yglu, yglu, cw3, cb3, lng, lnb, gc, att, w_out, x2, mod)


def _rope_tables():
    rows = SEQ // GRID_W
    row = np.repeat(np.arange(rows), GRID_W).astype(np.float64)
    col = np.tile(np.arange(GRID_W), rows).astype(np.float64)
    nf = HEAD_DIM // 4
    inv = ROPE_BASE ** (-np.arange(nf, dtype=np.float64) / nf)
    ang_r = row[:, None] * inv
    ang_c = col[:, None] * inv
    ang = np.concatenate([ang_r, ang_r, ang_c, ang_c], axis=-1)
    cos, sin = np.cos(ang), np.sin(ang)
    first_half = (np.arange(HEAD_DIM) % (2 * nf)) < nf
    sa = np.where(first_half, -sin, 0.0)
    sb = np.where(first_half, 0.0, sin)
    pad = lambda t, v: np.concatenate([t, np.full((CTX_LEN, HEAD_DIM), v)], axis=0)
    two = lambda t: jnp.asarray(np.concatenate([t, t], axis=1), _F32)
    return two(pad(cos, 1.0)), two(pad(sa, 0.0)), two(pad(sb, 0.0))


def _chunk_avg():
    r = np.arange(NORM_GROUP) // HEAD_DIM
    return jnp.asarray(np.where(r[:, None] == r[None, :], 1.0 / HEAD_DIM, 0.0), _BF16)


def _softmax_shift(q_norm_g, k_norm_g, q_scale):
    bound = (HEAD_DIM * q_scale * SHIFT_MARGIN) * jnp.max(jnp.abs(q_norm_g)) * jnp.max(jnp.abs(k_norm_g))
    ok = (bound <= MAX_FIXED_SHIFT).astype(_F32)
    return jnp.stack([jnp.where(ok > 0.5, bound, 0.0), ok]).astype(_F32)


def kernel(x, c, ctx, c_ctx, w_ada, b_ada, norm_g, w_in, q_norm_g, k_norm_g, lam_q1, lam_k1,
           lam_q2, lam_k2, sub_norm_g, conv_w, conv_b, conv_ln_g, conv_ln_b, w_out):
    assert x.shape == (1, SEQ, D_MODEL) and ctx.shape == (1, CTX_LEN, D_MODEL)
    assert w_ada.shape[0] == 1 and w_in.shape == (1, D_MODEL, IN_W)
    x2 = x[0]
    ctx2 = ctx[0]

    cc = jnp.zeros((MOD_ROWS, D_MODEL), _F32).at[0].set(c[0]).at[1].set(c_ctx)
    mod = _adaln(cc, w_ada[0], b_ada)
    h = _modulate(x2, ctx2, mod, norm_g)

    w = w_in[0]
    cos, sa, sb = _rope_tables()
    q_scale = LOG2E / math.sqrt(HEAD_DIM)
    avg = _chunk_avg()
    reps = PROJ_TN // HEAD_DIM
    gq = jnp.tile(q_norm_g, (1, reps)) * q_scale
    gk = jnp.tile(k_norm_g, (1, reps))

    q, gates = _qg_proj(h, w, gq, avg, cos, sa, sb)
    k, vt = _kv_proj(h, w, gk, avg, cos, sa, sb)
    yglu = _glu_proj(h, w)

    lamv = jnp.concatenate([lam_q1, lam_k1, lam_q2, lam_k2], axis=0)
    shift = _softmax_shift(q_norm_g, k_norm_g, q_scale)
    att = _attention(shift, lamv, q, k, vt, gates, sub_norm_g)

    out = _conv_out(yglu, conv_w[0], conv_b, conv_ln_g, conv_ln_b, gates, att, w_out[0], x2, mod)
    return out[None]
```

```python
import math

import jax
import jax.numpy as jnp
import numpy as np
from jax import lax
from jax.experimental import pallas as pl
from jax.experimental.pallas import tpu as pltpu

D_MODEL = 2048
SEQ = 8192
GRID_W = 64
CTX_LEN = 256
L_ALL = SEQ + CTX_LEN
HEADS = 8
HEAD_DIM = 64
V_DIM = 2 * HEAD_DIM
ATT_W = HEADS * V_DIM
CONV_W = D_MODEL - ATT_W
CONV_K = 31
CONV_HALO = 16
ROPE_BASE = 10000.0
EPS = 1e-6
LAM_INIT = 0.8 - 0.6 * math.exp(-0.3 * 0)
LOG2E = math.log2(math.e)

Q0 = 0
K0 = Q0 + HEADS * 2 * HEAD_DIM
V0 = K0 + HEADS * 2 * HEAD_DIM
GA0 = V0 + ATT_W
U0 = GA0 + ATT_W
GC0 = U0 + 2 * CONV_W
IN_W = GC0 + CONV_W

SUBLANES = 8
MOD_ROWS = SUBLANES
ADA_TN = 1024
MOD_TM = 1024
MOD_ROWS_PER_PASS = 64
PROJ_TM = 1024
QK_ROWS = 256
QG_TM = 512
PROJ_TM_ALL = 768
PROJ_TN = 1024
GLU_TN = 512
ATT_TQ = 512
ATT_KB = PROJ_TM_ALL
N_KB = L_ALL // ATT_KB
FAST_KB = N_KB * ATT_KB
OUT_TM = 256
CONV_LANES = 128
CONV_CHUNKS = CONV_W // CONV_LANES
OUT_LANES = D_MODEL // CONV_CHUNKS
CONV_FIRST = CONV_HALO - CONV_K // 2
CONV_SPAN_EXTRA = SUBLANES * ((CONV_FIRST + CONV_K - 1) // SUBLANES)
NORM_GROUP = 256
MIB = 1024 * 1024
V7X_VMEM_BYTES = 64 * MIB
VMEM_LIMIT = 48 * MIB
CONV_OUT_VMEM_LIMIT = V7X_VMEM_BYTES - 6 * MIB
MAX_FIXED_SHIFT = 40.0
SHIFT_MARGIN = 1.01

_F32 = jnp.float32
_BF16 = jnp.bfloat16


def _sigmoid(x):
    return 1.0 / (1.0 + jnp.exp(-x))


def _silu(x):
    return x * _sigmoid(x)


def _adaln_kernel(cc_ref, w_ref, b_ref, o_ref):
    s = _silu(cc_ref[...]).astype(_BF16)
    o_ref[...] = jnp.dot(s, w_ref[...].astype(_BF16), preferred_element_type=_F32) + b_ref[...]


def _adaln(cc, w, b):
    n = w.shape[1]
    return pl.pallas_call(
        _adaln_kernel,
        grid=(n // ADA_TN,),
        in_specs=[pl.BlockSpec((MOD_ROWS, D_MODEL), lambda j: (0, 0)),
                  pl.BlockSpec((D_MODEL, ADA_TN), lambda j: (0, j)),
                  pl.BlockSpec((1, ADA_TN), lambda j: (0, j))],
        out_specs=pl.BlockSpec((MOD_ROWS, ADA_TN), lambda j: (0, j)),
        out_shape=jax.ShapeDtypeStruct((MOD_ROWS, n), _F32),
        compiler_params=pltpu.CompilerParams(vmem_limit_bytes=VMEM_LIMIT),
        name="adaln",
    )(cc, w, b)


def _modulate_rows(xin, g, shift, scale):
    r = lax.rsqrt(jnp.mean(xin * xin, axis=-1, keepdims=True) + EPS)
    return ((xin * r) * g) * (1.0 + scale) + shift


def _modulate_kernel(x_ref, ctx_ref, mod_ref, g_ref, h_ref):
    i = pl.program_id(0)
    n_lat = pl.num_programs(0) - 1
    g = g_ref[...]

    def run(src_ref, mod_row, n_rows):
        shift = mod_ref[mod_row:mod_row + 1, 0:D_MODEL]
        scale = mod_ref[mod_row:mod_row + 1, D_MODEL:2 * D_MODEL]

        def rows(r, carry):
            sl = pl.ds(pl.multiple_of(r * MOD_ROWS_PER_PASS, MOD_ROWS_PER_PASS), MOD_ROWS_PER_PASS)
            h_ref[sl, :] = _modulate_rows(src_ref[sl, :], g, shift, scale).astype(h_ref.dtype)
            return carry

        lax.fori_loop(0, n_rows // MOD_ROWS_PER_PASS, rows, 0)

    @pl.when(i < n_lat)
    def _():
        run(x_ref, 0, MOD_TM)

    @pl.when(i == n_lat)
    def _():
        run(ctx_ref, 1, CTX_LEN)


def _modulate(x2, ctx2, mod, g):
    n_lat = SEQ // MOD_TM
    return pl.pallas_call(
        _modulate_kernel,
        grid=(n_lat + 1,),
        in_specs=[pl.BlockSpec((MOD_TM, D_MODEL), lambda i: (jnp.minimum(i, n_lat - 1), 0)),
                  pl.BlockSpec((CTX_LEN, D_MODEL), lambda i: (0, 0)),
                  pl.BlockSpec((MOD_ROWS, 3 * D_MODEL), lambda i: (0, 0)),
                  pl.BlockSpec((1, D_MODEL), lambda i: (0, 0))],
        out_specs=pl.BlockSpec((MOD_TM, D_MODEL), lambda i: (i, 0)),
        out_shape=jax.ShapeDtypeStruct((L_ALL, D_MODEL), _BF16),
        compiler_params=pltpu.CompilerParams(vmem_limit_bytes=VMEM_LIMIT),
        name="modulate",
    )(x2, ctx2, mod, g)


def _row_parts(h_ref):
    return [slice(r0, r0 + QK_ROWS) for r0 in range(0, h_ref.shape[0], QK_ROWS)]


def _glu_kernel(h_ref, wa_ref, wg_ref, o_ref):
    wa = wa_ref[...].astype(_BF16)
    wg = wg_ref[...].astype(_BF16)
    for rows in _row_parts(h_ref):
        h = h_ref[rows, :]
        a = jnp.dot(h, wa, preferred_element_type=_F32)
        g = jnp.dot(h, wg, preferred_element_type=_F32)
        o_ref[rows, :] = (a * _sigmoid(g)).astype(o_ref.dtype)


def _glu_proj(h, w):
    tm, tn = PROJ_TM, GLU_TN
    wspec = lambda c0: pl.BlockSpec((D_MODEL, tn), lambda j, i: (0, c0 // tn + j))
    return pl.pallas_call(
        _glu_kernel,
        grid=(CONV_W // tn, SEQ // tm),
        in_specs=[pl.BlockSpec((tm, D_MODEL), lambda j, i: (i, 0)), wspec(U0), wspec(U0 + CONV_W)],
        out_specs=pl.BlockSpec((tm, tn), lambda j, i: (i, j)),
        out_shape=jax.ShapeDtypeStruct((SEQ, CONV_W), _BF16),
        compiler_params=pltpu.CompilerParams(vmem_limit_bytes=VMEM_LIMIT),
        name="proj_glu",
    )(h, w, w)


def _qk_epilogue(p, g, avg, cos, sa, sb):
    tn = p.shape[1]
    ss = p * p
    hi = ss.astype(_BF16)
    lo = (ss - hi.astype(_F32)).astype(_BF16)
    ms = []
    for c in range(tn // NORM_GROUP):
        sl = slice(c * NORM_GROUP, (c + 1) * NORM_GROUP)
        ms.append(jnp.dot(hi[:, sl], avg, preferred_element_type=_F32)
                  + jnp.dot(lo[:, sl], avg, preferred_element_type=_F32))
    ms = jnp.concatenate(ms, axis=1)
    y = (p * lax.rsqrt(ms + EPS)) * g
    reps = tn // cos.shape[1]
    cos, sa, sb = (jnp.concatenate([t] * reps, axis=1) for t in (cos, sa, sb))
    quarter = HEAD_DIM // 4
    return y * cos + pltpu.roll(y, tn - quarter, 1) * sa + pltpu.roll(y, quarter, 1) * sb


def _kv_kernel(h_ref, wk_ref, wv_ref, g_ref, avg_ref, cos_ref, sa_ref, sb_ref, k_ref, vt_ref, wbf_ref):
    @pl.when(pl.program_id(0) == 0)
    def _():
        wbf_ref[0] = wk_ref[...].astype(_BF16)
        wbf_ref[1] = wv_ref[...].astype(_BF16)

    wk = wbf_ref[0]
    wv = wbf_ref[1]
    avg = avg_ref[...]
    g = g_ref[...]
    for rows in _row_parts(h_ref):
        hp = h_ref[rows, :]
        pk = jnp.dot(hp, wk, preferred_element_type=_F32)
        out = _qk_epilogue(pk, g, avg, cos_ref[rows, :], sa_ref[rows, :], sb_ref[rows, :])
        k_ref[rows, :] = out.astype(k_ref.dtype)
        pt = jnp.dot(hp, wv, preferred_element_type=_F32).T.astype(vt_ref.dtype)
        for hd in range(HEADS):
            vt_ref[hd, 0, :, rows] = pt[hd * V_DIM:(hd + 1) * V_DIM, :]


def _qg_kernel(h_ref, wq_ref, wga_ref, wgc_ref, g_ref, avg_ref, cos_ref, sa_ref, sb_ref, q_ref, gates_ref):
    wq = wq_ref[...].astype(_BF16)
    wga = wga_ref[...].astype(_BF16)
    wgc = wgc_ref[...].astype(_BF16)
    avg = avg_ref[...]
    g = g_ref[...]
    tn = wq.shape[1]
    for rows in _row_parts(h_ref):
        hp = h_ref[rows, :]
        pq = jnp.dot(hp, wq, preferred_element_type=_F32)
        out = _qk_epilogue(pq, g, avg, cos_ref[rows, :], sa_ref[rows, :], sb_ref[rows, :])
        q_ref[rows, :] = out.astype(q_ref.dtype)
        gates_ref[rows, 0:tn] = _silu(jnp.dot(hp, wga, preferred_element_type=_F32)).astype(gates_ref.dtype)
        gates_ref[rows, tn:] = _silu(jnp.dot(hp, wgc, preferred_element_type=_F32)).astype(gates_ref.dtype)


def _qg_proj(h, w, gq, avg, cos, sa, sb):
    tm = QG_TM
    fixed = lambda i: (0, 0)
    row = lambda i: (i, 0)
    once = pl.Buffered(1)
    wspec = lambda c0: pl.BlockSpec((D_MODEL, PROJ_TN), lambda i: (0, c0 // PROJ_TN), pipeline_mode=once)
    return pl.pallas_call(
        _qg_kernel,
        grid=(SEQ // tm,),
        in_specs=[pl.BlockSpec((tm, D_MODEL), row), wspec(Q0), wspec(GA0), wspec(GC0),
                  pl.BlockSpec((1, PROJ_TN), fixed),
                  pl.BlockSpec((NORM_GROUP, NORM_GROUP), fixed),
                  pl.BlockSpec((tm, 2 * HEAD_DIM), row),
                  pl.BlockSpec((tm, 2 * HEAD_DIM), row),
                  pl.BlockSpec((tm, 2 * HEAD_DIM), row)],
        out_specs=[pl.BlockSpec((tm, PROJ_TN), row),
                   pl.BlockSpec((tm, 2 * PROJ_TN), row)],
        out_shape=[jax.ShapeDtypeStruct((SEQ, PROJ_TN), _BF16),
                   jax.ShapeDtypeStruct((SEQ, 2 * PROJ_TN), _BF16)],
        compiler_params=pltpu.CompilerParams(vmem_limit_bytes=VMEM_LIMIT),
        name="proj_qg",
    )(h, w, w, w, gq, avg, cos, sa, sb)


def _kv_proj(h, w, gk, avg, cos, sa, sb):
    tm = PROJ_TM_ALL
    fixed = lambda i: (0, 0)
    row = lambda i: (i, 0)
    once = pl.Buffered(1)
    return pl.pallas_call(
        _kv_kernel,
        grid=(L_ALL // tm,),
        in_specs=[pl.BlockSpec((tm, D_MODEL), row),
                  pl.BlockSpec((D_MODEL, PROJ_TN), lambda i: (0, K0 // PROJ_TN), pipeline_mode=once),
                  pl.BlockSpec((D_MODEL, PROJ_TN), lambda i: (0, V0 // PROJ_TN), pipeline_mode=once),
                  pl.BlockSpec((1, PROJ_TN), fixed),
                  pl.BlockSpec((NORM_GROUP, NORM_GROUP), fixed),
                  pl.BlockSpec((tm, 2 * HEAD_DIM), row),
                  pl.BlockSpec((tm, 2 * HEAD_DIM), row),
                  pl.BlockSpec((tm, 2 * HEAD_DIM), row)],
        out_specs=[pl.BlockSpec((tm, PROJ_TN), row),
                   pl.BlockSpec((HEADS, 1, V_DIM, tm), lambda i: (0, i, 0, 0))],
        out_shape=[jax.ShapeDtypeStruct((L_ALL, PROJ_TN), _BF16),
                   jax.ShapeDtypeStruct((HEADS, N_KB, V_DIM, tm), _BF16)],
        scratch_shapes=[pltpu.VMEM((2, D_MODEL, PROJ_TN), _BF16)],
        compiler_params=pltpu.CompilerParams(vmem_limit_bytes=CONV_OUT_VMEM_LIMIT,
                                             dimension_semantics=("arbitrary",)),
        name="proj_kv",
    )(h, w, w, gk, avg, cos, sa, sb)


def _conv_chunk(c, tile, n_tiles, y_ref, yp_ref, yn_ref, cw_ref, cb_ref, win_ref, shift_ref):
    tm = y_ref.shape[0]
    halo = CONV_HALO
    span = shift_ref.shape[2]
    keep_prev = (tile > 0).astype(_F32)
    keep_next = (tile < n_tiles - 1).astype(_F32)
    cols = slice(c * CONV_LANES, (c + 1) * CONV_LANES)
    win_ref[c, 0:halo, :] = yp_ref[:, cols].astype(_F32) * keep_prev
    win_ref[c, halo:halo + tm, :] = y_ref[:, cols].astype(_F32)
    win_ref[c, halo + tm:, :] = yn_ref[:, cols].astype(_F32) * keep_next
    for res in range(SUBLANES):
        shift_ref[c, res] = win_ref[c, res:res + span, :]
    part = jnp.zeros((tm, CONV_LANES), _F32) + cb_ref[c]
    for t in range(CONV_K):
        res = (CONV_FIRST + t) % SUBLANES
        a = (CONV_FIRST + t) - res
        part = part + shift_ref[c, res, a:a + tm, :] * cw_ref[c, t:t + 1, :]
    return part


def _conv_tail(parts, lng_ref, lnb_ref, gc_ref):
    acc = jnp.concatenate(parts, axis=1)
    mu = jnp.mean(acc, axis=-1, keepdims=True)
    cen = acc - mu
    var = jnp.mean(cen * cen, axis=-1, keepdims=True)
    ln = (cen * lax.rsqrt(var + EPS)) * lng_ref[...] + lnb_ref[...]
    return (_silu(ln) * gc_ref[...].astype(_F32)).astype(_BF16)


def _attn_kernel(shift_ref, lamv_ref, q_ref, k_ref, vt_ref, ga_ref, subg_ref, o_ref):
    tq = q_ref.shape[0]
    qt = q_ref[...].T
    row = lax.broadcasted_iota(jnp.int32, qt.shape, 0)
    zero = jnp.zeros_like(qt)
    qst = jnp.concatenate([jnp.where(row < HEAD_DIM, qt, zero),
                           jnp.where(row >= HEAD_DIM, qt, zero)], axis=1)
    groups = ATT_KB // SUBLANES

    def scores(kb):
        start = kb * ATT_KB if isinstance(kb, int) else pl.multiple_of(kb * ATT_KB, ATT_KB)
        return jnp.dot(k_ref[pl.ds(start, ATT_KB), :], qst, preferred_element_type=_F32)

    def finish(l, acc):
        o = acc / jnp.sum(l, axis=0, keepdims=True)
        lv = lamv_ref[...]
        lam = (jnp.exp(jnp.sum(lv[0:1] * lv[1:2], axis=1, keepdims=True))
               - jnp.exp(jnp.sum(lv[2:3] * lv[3:4], axis=1, keepdims=True)) + LAM_INIT)
        od = o[:, :tq] - lam * o[:, tq:]
        r = lax.rsqrt(jnp.mean(od * od, axis=0, keepdims=True) + EPS)
        on = ((od * r).T * subg_ref[...]) * (1.0 - LAM_INIT)
        o_ref[...] = (on * ga_ref[...].astype(_F32)).astype(o_ref.dtype)

    l0 = jnp.zeros((SUBLANES, 2 * tq), _F32)
    a0 = jnp.zeros((V_DIM, 2 * tq), _F32)
    fixed_ok = shift_ref[1] > 0.5

    @pl.when(fixed_ok)
    def _():
        shift = shift_ref[0]
        l, acc = l0, a0
        for start in range(0, L_ALL, FAST_KB):
            size = min(FAST_KB, L_ALL - start)
            st = jnp.dot(k_ref[start:start + size, :], qst, preferred_element_type=_F32)
            p = jnp.exp2(st - shift)
            l = l + jnp.sum(p.reshape(size // SUBLANES, SUBLANES, 2 * tq), axis=0)
            pb = p.astype(_BF16)
            for off in range(0, size, ATT_KB):
                acc = acc + jnp.dot(vt_ref[0, (start + off) // ATT_KB], pb[off:off + ATT_KB, :],
                                    preferred_element_type=_F32)
        finish(l, acc)

    @pl.when(jnp.logical_not(fixed_ok))
    def _():
        def body(kb, carry):
            m, l, acc = carry
            st = scores(kb)
            mb = jnp.max(jnp.max(st.reshape(groups, SUBLANES, 2 * tq), axis=0), axis=0, keepdims=True)
            m_new = jnp.maximum(m, mb)
            alpha = jnp.exp2(m - m_new)
            p = jnp.exp2(st - m_new)
            l = alpha * l + jnp.sum(p.reshape(groups, SUBLANES, 2 * tq), axis=0)
            acc = alpha * acc + jnp.dot(vt_ref[0, kb], p.astype(_BF16), preferred_element_type=_F32)
            return m_new, l, acc

        m0 = jnp.full((1, 2 * tq), -1e30, _F32)
        _, l, acc = lax.fori_loop(0, N_KB, body, (m0, l0, a0))
        finish(l, acc)


def _attention(shift, lamv, q, k, vt, ga, subg):
    qrow = lambda h, i: (i, h)
    return pl.pallas_call(
        _attn_kernel,
        grid=(HEADS, SEQ // ATT_TQ),
        in_specs=[pl.BlockSpec(memory_space=pltpu.SMEM),
                  pl.BlockSpec(lamv.shape, lambda h, i: (0, 0)),
                  pl.BlockSpec((ATT_TQ, V_DIM), qrow),
                  pl.BlockSpec((L_ALL, V_DIM), lambda h, i: (0, h)),
                  pl.BlockSpec((1, N_KB, V_DIM, ATT_KB), lambda h, i: (h, 0, 0, 0)),
                  pl.BlockSpec((ATT_TQ, V_DIM), qrow),
                  pl.BlockSpec((1, V_DIM), lambda h, i: (0, 0))],
        out_specs=pl.BlockSpec((ATT_TQ, V_DIM), qrow),
        out_shape=jax.ShapeDtypeStruct((SEQ, ATT_W), _BF16),
        compiler_params=pltpu.CompilerParams(vmem_limit_bytes=VMEM_LIMIT),
        name="diff_attn",
    )(shift, lamv, q, k, vt, ga, subg)


def _conv_out_kernel(y_ref, yp_ref, yn_ref, cw_ref, cb_ref, lng_ref, lnb_ref, gc_ref, att_ref,
                     w_ref, x_ref, mod_ref, o_ref, win_ref, shift_ref, conv_ref, wbf_ref):
    i = pl.program_id(0)
    n_tiles = pl.num_programs(0) - 1
    tile = jnp.minimum(i, n_tiles - 1)

    @pl.when(i == 0)
    def _():
        conv_ref[1] = jnp.zeros(conv_ref.shape[1:], conv_ref.dtype)
        wbf_ref[...] = w_ref[...].astype(wbf_ref.dtype)

    att = att_ref[...]
    conv_prev = conv_ref[(i + 1) % 2]
    ys, parts = [], []
    for c in range(CONV_CHUNKS):
        cols = slice(c * OUT_LANES, (c + 1) * OUT_LANES)
        ys.append(jnp.dot(att, wbf_ref[0:ATT_W, cols], preferred_element_type=_F32)
                  + jnp.dot(conv_prev, wbf_ref[ATT_W:, cols], preferred_element_type=_F32))
        parts.append(_conv_chunk(c, tile, n_tiles, y_ref, yp_ref, yn_ref, cw_ref, cb_ref,
                                 win_ref, shift_ref))
    gate = mod_ref[0:1, 2 * D_MODEL:3 * D_MODEL]
    o_ref[...] = x_ref[...] + gate * jnp.concatenate(ys, axis=1)
    conv_ref[i % 2] = _conv_tail(parts, lng_ref, lnb_ref, gc_ref)


def _conv_out(yglu, cw, cb, lng, lnb, gc, att, w_out, x2, mod):
    tm = OUT_TM
    n_tiles = SEQ // tm
    nb = tm // CONV_HALO
    last = SEQ // CONV_HALO - 1
    cur = lambda i: (jnp.minimum(i, n_tiles - 1), 0)
    prev = lambda i: (jnp.maximum(i - 1, 0), 0)
    fixed = lambda i: (0, 0)
    fixed3 = lambda i: (0, 0, 0)
    cw3 = cw.reshape(CONV_K, CONV_CHUNKS, CONV_LANES).transpose(1, 0, 2)
    cb3 = cb.reshape(CONV_CHUNKS, 1, CONV_LANES)
    return pl.pallas_call(
        _conv_out_kernel,
        grid=(n_tiles + 1,),
        in_specs=[pl.BlockSpec((tm, CONV_W), cur),
                  pl.BlockSpec((CONV_HALO, CONV_W),
                               lambda i: (jnp.maximum(jnp.minimum(i, n_tiles - 1) * nb - 1, 0), 0)),
                  pl.BlockSpec((CONV_HALO, CONV_W),
                               lambda i: (jnp.minimum((jnp.minimum(i, n_tiles - 1) + 1) * nb, last), 0)),
                  pl.BlockSpec((CONV_CHUNKS, CONV_K, CONV_LANES), fixed3),
                  pl.BlockSpec((CONV_CHUNKS, 1, CONV_LANES), fixed3),
                  pl.BlockSpec((1, CONV_W), fixed),
                  pl.BlockSpec((1, CONV_W), fixed),
                  pl.BlockSpec((tm, CONV_W), lambda i: (jnp.minimum(i, n_tiles - 1), 1)),
                  pl.BlockSpec((tm, ATT_W), prev),
                  pl.BlockSpec((D_MODEL, D_MODEL), fixed, pipeline_mode=pl.Buffered(1)),
                  pl.BlockSpec((tm, D_MODEL), prev),
                  pl.BlockSpec((MOD_ROWS, 3 * D_MODEL), fixed)],
        out_specs=pl.BlockSpec((tm, D_MODEL), prev),
        out_shape=jax.ShapeDtypeStruct((SEQ, D_MODEL), _F32),
        scratch_shapes=[pltpu.VMEM((CONV_CHUNKS, tm + 2 * CONV_HALO, CONV_LANES), _F32),
                        pltpu.VMEM((CONV_CHUNKS, SUBLANES, tm + CONV_SPAN_EXTRA, CONV_LANES), _F32),
                        pltpu.VMEM((2, tm, CONV_W), _BF16),
                        pltpu.VMEM((D_MODEL, D_MODEL), _BF16)],
        compiler_params=pltpu.CompilerParams(vmem_limit_bytes=CONV_OUT_VMEM_LIMIT,
                                             dimension_semantics=("arbitrary",)),
        name="conv_out",
    )(yglu, yglu, yglu, cw3, cb3, lng, lnb, gc, att, w_out, x2, mod)


def _rope_tables():
    rows = SEQ // GRID_W
    row = np.repeat(np.arange(rows), GRID_W).astype(np.float64)
    col = np.tile(np.arange(GRID_W), rows).astype(np.float64)
    nf = HEAD_DIM // 4
    inv = ROPE_BASE ** (-np.arange(nf, dtype=np.float64) / nf)
    ang_r = row[:, None] * inv
    ang_c = col[:, None] * inv
    ang = np.concatenate([ang_r, ang_r, ang_c, ang_c], axis=-1)
    cos, sin = np.cos(ang), np.sin(ang)
    first_half = (np.arange(HEAD_DIM) % (2 * nf)) < nf
    sa = np.where(first_half, -sin, 0.0)
    sb = np.where(first_half, 0.0, sin)
    pad = lambda t, v: np.concatenate([t, np.full((CTX_LEN, HEAD_DIM), v)], axis=0)
    two = lambda t: jnp.asarray(np.concatenate([t, t], axis=1), _F32)
    return two(pad(cos, 1.0)), two(pad(sa, 0.0)), two(pad(sb, 0.0))


def _chunk_avg():
    r = np.arange(NORM_GROUP) // HEAD_DIM
    return jnp.asarray(np.where(r[:, None] == r[None, :], 1.0 / HEAD_DIM, 0.0), _BF16)


def _softmax_shift(q_norm_g, k_norm_g, q_scale):
    bound = (HEAD_DIM * q_scale * SHIFT_MARGIN) * jnp.max(jnp.abs(q_norm_g)) * jnp.max(jnp.abs(k_norm_g))
    ok = (bound <= MAX_FIXED_SHIFT).astype(_F32)
    return jnp.stack([jnp.where(ok > 0.5, bound, 0.0), ok]).astype(_F32)


def kernel(x, c, ctx, c_ctx, w_ada, b_ada, norm_g, w_in, q_norm_g, k_norm_g, lam_q1, lam_k1,
           lam_q2, lam_k2, sub_norm_g, conv_w, conv_b, conv_ln_g, conv_ln_b, w_out):
    assert x.shape == (1, SEQ, D_MODEL) and ctx.shape == (1, CTX_LEN, D_MODEL)
    assert w_ada.shape[0] == 1 and w_in.shape == (1, D_MODEL, IN_W)
    x2 = x[0]
    ctx2 = ctx[0]

    cc = jnp.zeros((MOD_ROWS, D_MODEL), _F32).at[0].set(c[0]).at[1].set(c_ctx)
    mod = _adaln(cc, w_ada[0], b_ada)
    h = _modulate(x2, ctx2, mod, norm_g)

    w = w_in[0]
    cos, sa, sb = _rope_tables()
    q_scale = LOG2E / math.sqrt(HEAD_DIM)
    avg = _chunk_avg()
    reps = PROJ_TN // HEAD_DIM
    gq = jnp.tile(q_norm_g, (1, reps)) * q_scale
    gk = jnp.tile(k_norm_g, (1, reps))

    q, gates = _qg_proj(h, w, gq, avg, cos, sa, sb)
    k, vt = _kv_proj(h, w, gk, avg, cos, sa, sb)
    yglu = _glu_proj(h, w)

    lamv = jnp.concatenate([lam_q1, lam_k1, lam_q2, lam_k2], axis=0)
    shift = _softmax_shift(q_norm_g, k_norm_g, q_scale)
    att = _attention(shift, lamv, q, k, vt, gates, sub_norm_g)

    out = _conv_out(yglu, conv_w[0], conv_b, conv_ln_g, conv_ln_b, gates, att, w_out[0], x2, mod)
    return out[None]
```

```python
import math

import jax
import jax.numpy as jnp
import numpy as np
from jax import lax
from jax.experimental import pallas as pl
from jax.experimental.pallas import tpu as pltpu

D_MODEL = 2048
SEQ = 8192
GRID_W = 64
CTX_LEN = 256
L_ALL = SEQ + CTX_LEN
HEADS = 8
HEAD_DIM = 64
V_DIM = 2 * HEAD_DIM
ATT_W = HEADS * V_DIM
CONV_W = D_MODEL - ATT_W
CONV_K = 31
CONV_HALO = 16
ROPE_BASE = 10000.0
EPS = 1e-6
LAM_INIT = 0.8 - 0.6 * math.exp(-0.3 * 0)
LOG2E = math.log2(math.e)

Q0 = 0
K0 = Q0 + HEADS * 2 * HEAD_DIM
V0 = K0 + HEADS * 2 * HEAD_DIM
GA0 = V0 + ATT_W
U0 = GA0 + ATT_W
GC0 = U0 + 2 * CONV_W
IN_W = GC0 + CONV_W

SUBLANES = 8
MOD_ROWS = SUBLANES
ADA_TN = 1024
MOD_TM = 1024
MOD_ROWS_PER_PASS = 64
PROJ_TM = 1024
QK_ROWS = 256
QG_TM = 512
PROJ_TM_ALL = 768
PROJ_TN = 1024
GLU_TN = 512
ATT_TQ = 512
ATT_KB = PROJ_TM_ALL
N_KB = L_ALL // ATT_KB
FAST_KB = N_KB * ATT_KB
OUT_TM = 256
CONV_LANES = 128
CONV_CHUNKS = CONV_W // CONV_LANES
OUT_LANES = D_MODEL // CONV_CHUNKS
CONV_FIRST = CONV_HALO - CONV_K // 2
CONV_SPAN_EXTRA = SUBLANES * ((CONV_FIRST + CONV_K - 1) // SUBLANES)
NORM_GROUP = 256
MIB = 1024 * 1024
V7X_VMEM_BYTES = 64 * MIB
VMEM_LIMIT = 48 * MIB
CONV_OUT_VMEM_LIMIT = V7X_VMEM_BYTES - 6 * MIB
MAX_FIXED_SHIFT = 40.0
SHIFT_MARGIN = 1.01

_F32 = jnp.float32
_BF16 = jnp.bfloat16


def _sigmoid(x):
    return 1.0 / (1.0 + jnp.exp(-x))


def _silu(x):
    return x * _sigmoid(x)


def _adaln_kernel(cc_ref, w_ref, b_ref, o_ref):
    s = _silu(cc_ref[...]).astype(_BF16)
    o_ref[...] = jnp.dot(s, w_ref[...].astype(_BF16), preferred_element_type=_F32) + b_ref[...]


def _adaln(cc, w, b):
    n = w.shape[1]
    return pl.pallas_call(
        _adaln_kernel,
        grid=(n // ADA_TN,),
        in_specs=[pl.BlockSpec((MOD_ROWS, D_MODEL), lambda j: (0, 0)),
                  pl.BlockSpec((D_MODEL, ADA_TN), lambda j: (0, j)),
                  pl.BlockSpec((1, ADA_TN), lambda j: (0, j))],
        out_specs=pl.BlockSpec((MOD_ROWS, ADA_TN), lambda j: (0, j)),
        out_shape=jax.ShapeDtypeStruct((MOD_ROWS, n), _F32),
        compiler_params=pltpu.CompilerParams(vmem_limit_bytes=VMEM_LIMIT),
        name="adaln",
    )(cc, w, b)


def _modulate_rows(xin, g, shift, scale):
    r = lax.rsqrt(jnp.mean(xin * xin, axis=-1, keepdims=True) + EPS)
    return ((xin * r) * g) * (1.0 + scale) + shift


def _modulate_kernel(x_ref, ctx_ref, mod_ref, g_ref, h_ref):
    i = pl.program_id(0)
    n_lat = pl.num_programs(0) - 1
    g = g_ref[...]

    def run(src_ref, mod_row, n_rows):
        shift = mod_ref[mod_row:mod_row + 1, 0:D_MODEL]
        scale = mod_ref[mod_row:mod_row + 1, D_MODEL:2 * D_MODEL]

        def rows(r, carry):
            sl = pl.ds(pl.multiple_of(r * MOD_ROWS_PER_PASS, MOD_ROWS_PER_PASS), MOD_ROWS_PER_PASS)
            h_ref[sl, :] = _modulate_rows(src_ref[sl, :], g, shift, scale).astype(h_ref.dtype)
            return carry

        lax.fori_loop(0, n_rows // MOD_ROWS_PER_PASS, rows, 0)

    @pl.when(i < n_lat)
    def _():
        run(x_ref, 0, MOD_TM)

    @pl.when(i == n_lat)
    def _():
        run(ctx_ref, 1, CTX_LEN)


def _modulate(x2, ctx2, mod, g):
    n_lat = SEQ // MOD_TM
    return pl.pallas_call(
        _modulate_kernel,
        grid=(n_lat + 1,),
        in_specs=[pl.BlockSpec((MOD_TM, D_MODEL), lambda i: (jnp.minimum(i, n_lat - 1), 0)),
                  pl.BlockSpec((CTX_LEN, D_MODEL), lambda i: (0, 0)),
                  pl.BlockSpec((MOD_ROWS, 3 * D_MODEL), lambda i: (0, 0)),
                  pl.BlockSpec((1, D_MODEL), lambda i: (0, 0))],
        out_specs=pl.BlockSpec((MOD_TM, D_MODEL), lambda i: (i, 0)),
        out_shape=jax.ShapeDtypeStruct((L_ALL, D_MODEL), _BF16),
        compiler_params=pltpu.CompilerParams(vmem_limit_bytes=VMEM_LIMIT),
        name="modulate",
    )(x2, ctx2, mod, g)


def _row_parts(h_ref):
    return [slice(r0, r0 + QK_ROWS) for r0 in range(0, h_ref.shape[0], QK_ROWS)]


def _glu_kernel(h_ref, wa_ref, wg_ref, o_ref):
    wa = wa_ref[...].astype(_BF16)
    wg = wg_ref[...].astype(_BF16)
    for rows in _row_parts(h_ref):
        h = h_ref[rows, :]
        a = jnp.dot(h, wa, preferred_element_type=_F32)
        g = jnp.dot(h, wg, preferred_element_type=_F32)
        o_ref[rows, :] = (a * _sigmoid(g)).astype(o_ref.dtype)


def _glu_proj(h, w):
    tm, tn = PROJ_TM, GLU_TN
    wspec = lambda c0: pl.BlockSpec((D_MODEL, tn), lambda j, i: (0, c0 // tn + j))
    return pl.pallas_call(
        _glu_kernel,
        grid=(CONV_W // tn, SEQ // tm),
        in_specs=[pl.BlockSpec((tm, D_MODEL), lambda j, i: (i, 0)), wspec(U0), wspec(U0 + CONV_W)],
        out_specs=pl.BlockSpec((tm, tn), lambda j, i: (i, j)),
        out_shape=jax.ShapeDtypeStruct((SEQ, CONV_W), _BF16),
        compiler_params=pltpu.CompilerParams(vmem_limit_bytes=VMEM_LIMIT),
        name="proj_glu",
    )(h, w, w)


def _qk_epilogue(p, g, avg, cos, sa, sb):
    tn = p.shape[1]
    ss = p * p
    hi = ss.astype(_BF16)
    lo = (ss - hi.astype(_F32)).astype(_BF16)
    ms = []
    for c in range(tn // NORM_GROUP):
        sl = slice(c * NORM_GROUP, (c + 1) * NORM_GROUP)
        ms.append(jnp.dot(hi[:, sl], avg, preferred_element_type=_F32)
                  + jnp.dot(lo[:, sl], avg, preferred_element_type=_F32))
    ms = jnp.concatenate(ms, axis=1)
    y = (p * lax.rsqrt(ms + EPS)) * g
    reps = tn // cos.shape[1]
    cos, sa, sb = (jnp.concatenate([t] * reps, axis=1) for t in (cos, sa, sb))
    quarter = HEAD_DIM // 4
    return y * cos + pltpu.roll(y, tn - quarter, 1) * sa + pltpu.roll(y, quarter, 1) * sb


def _kv_kernel(h_ref, wk_ref, wv_ref, g_ref, avg_ref, cos_ref, sa_ref, sb_ref, k_ref, vt_ref):
    wk = wk_ref[...].astype(_BF16)
    wv = wv_ref[...].astype(_BF16)
    avg = avg_ref[...]
    g = g_ref[...]
    for rows in _row_parts(h_ref):
        hp = h_ref[rows, :]
        pk = jnp.dot(hp, wk, preferred_element_type=_F32)
        out = _qk_epilogue(pk, g, avg, cos_ref[rows, :], sa_ref[rows, :], sb_ref[rows, :])
        k_ref[rows, :] = out.astype(k_ref.dtype)
        pt = jnp.dot(hp, wv, preferred_element_type=_F32).T.astype(vt_ref.dtype)
        for hd in range(HEADS):
            vt_ref[hd, 0, :, rows] = pt[hd * V_DIM:(hd + 1) * V_DIM, :]


def _qg_kernel(h_ref, wq_ref, wga_ref, wgc_ref, g_ref, avg_ref, cos_ref, sa_ref, sb_ref, q_ref, gates_ref):
    wq = wq_ref[...].astype(_BF16)
    wga = wga_ref[...].astype(_BF16)
    wgc = wgc_ref[...].astype(_BF16)
    avg = avg_ref[...]
    g = g_ref[...]
    tn = wq.shape[1]
    for rows in _row_parts(h_ref):
        hp = h_ref[rows, :]
        pq = jnp.dot(hp, wq, preferred_element_type=_F32)
        out = _qk_epilogue(pq, g, avg, cos_ref[rows, :], sa_ref[rows, :], sb_ref[rows, :])
        q_ref[rows, :] = out.astype(q_ref.dtype)
        gates_ref[rows, 0:tn] = _silu(jnp.dot(hp, wga, preferred_element_type=_F32)).astype(gates_ref.dtype)
        gates_ref[rows, tn:] = _silu(jnp.dot(hp, wgc, preferred_element_type=_F32)).astype(gates_ref.dtype)


def _qg_proj(h, w, gq, avg, cos, sa, sb):
    tm = QG_TM
    fixed = lambda i: (0, 0)
    row = lambda i: (i, 0)
    once = pl.Buffered(1)
    wspec = lambda c0: pl.BlockSpec((D_MODEL, PROJ_TN), lambda i: (0, c0 // PROJ_TN), pipeline_mode=once)
    return pl.pallas_call(
        _qg_kernel,
        grid=(SEQ // tm,),
        in_specs=[pl.BlockSpec((tm, D_MODEL), row), wspec(Q0), wspec(GA0), wspec(GC0),
                  pl.BlockSpec((1, PROJ_TN), fixed),
                  pl.BlockSpec((NORM_GROUP, NORM_GROUP), fixed),
                  pl.BlockSpec((tm, 2 * HEAD_DIM), row),
                  pl.BlockSpec((tm, 2 * HEAD_DIM), row),
                  pl.BlockSpec((tm, 2 * HEAD_DIM), row)],
        out_specs=[pl.BlockSpec((tm, PROJ_TN), row),
                   pl.BlockSpec((tm, 2 * PROJ_TN), row)],
        out_shape=[jax.ShapeDtypeStruct((SEQ, PROJ_TN), _BF16),
                   jax.ShapeDtypeStruct((SEQ, 2 * PROJ_TN), _BF16)],
        compiler_params=pltpu.CompilerParams(vmem_limit_bytes=VMEM_LIMIT),
        name="proj_qg",
    )(h, w, w, w, gq, avg, cos, sa, sb)


def _kv_proj(h, w, gk, avg, cos, sa, sb):
    tm = PROJ_TM_ALL
    fixed = lambda i: (0, 0)
    row = lambda i: (i, 0)
    once = pl.Buffered(1)
    return pl.pallas_call(
        _kv_kernel,
        grid=(L_ALL // tm,),
        in_specs=[pl.BlockSpec((tm, D_MODEL), row),
                  pl.BlockSpec((D_MODEL, PROJ_TN), lambda i: (0, K0 // PROJ_TN), pipeline_mode=once),
                  pl.BlockSpec((D_MODEL, PROJ_TN), lambda i: (0, V0 // PROJ_TN), pipeline_mode=once),
                  pl.BlockSpec((1, PROJ_TN), fixed),
                  pl.BlockSpec((NORM_GROUP, NORM_GROUP), fixed),
                  pl.BlockSpec((tm, 2 * HEAD_DIM), row),
                  pl.BlockSpec((tm, 2 * HEAD_DIM), row),
                  pl.BlockSpec((tm, 2 * HEAD_DIM), row)],
        out_specs=[pl.BlockSpec((tm, PROJ_TN), row),
                   pl.BlockSpec((HEADS, 1, V_DIM, tm), lambda i: (0, i, 0, 0))],
        out_shape=[jax.ShapeDtypeStruct((L_ALL, PROJ_TN), _BF16),
                   jax.ShapeDtypeStruct((HEADS, N_KB, V_DIM, tm), _BF16)],
        compiler_params=pltpu.CompilerParams(vmem_limit_bytes=VMEM_LIMIT),
        name="proj_kv",
    )(h, w, w, gk, avg, cos, sa, sb)


def _conv_chunk(c, tile, n_tiles, y_ref, yp_ref, yn_ref, cw_ref, cb_ref, shift_ref):
    tm = y_ref.shape[0]
    span = shift_ref.shape[2]
    keep_prev = (tile > 0).astype(_F32)
    keep_next = (tile < n_tiles - 1).astype(_F32)
    cols = slice(c * CONV_LANES, (c + 1) * CONV_LANES)
    window = jnp.concatenate([yp_ref[:, cols].astype(_F32) * keep_prev,
                              y_ref[:, cols].astype(_F32),
                              yn_ref[:, cols].astype(_F32) * keep_next], axis=0)
    for res in range(SUBLANES):
        shift_ref[c, res] = window[res:res + span, :]
    part = jnp.zeros((tm, CONV_LANES), _F32) + cb_ref[c]
    for t in range(CONV_K):
        res = (CONV_FIRST + t) % SUBLANES
        a = (CONV_FIRST + t) - res
        part = part + shift_ref[c, res, a:a + tm, :] * cw_ref[c, t:t + 1, :]
    return part


def _conv_tail(parts, lng_ref, lnb_ref, gc_ref):
    acc = jnp.concatenate(parts, axis=1)
    mu = jnp.mean(acc, axis=-1, keepdims=True)
    cen = acc - mu
    var = jnp.mean(cen * cen, axis=-1, keepdims=True)
    ln = (cen * lax.rsqrt(var + EPS)) * lng_ref[...] + lnb_ref[...]
    return (_silu(ln) * gc_ref[...].astype(_F32)).astype(_BF16)


def _attn_kernel(shift_ref, lamv_ref, q_ref, k_ref, vt_ref, ga_ref, subg_ref, o_ref):
    tq = q_ref.shape[0]
    qt = q_ref[...].T
    row = lax.broadcasted_iota(jnp.int32, qt.shape, 0)
    zero = jnp.zeros_like(qt)
    qst = jnp.concatenate([jnp.where(row < HEAD_DIM, qt, zero),
                           jnp.where(row >= HEAD_DIM, qt, zero)], axis=1)
    groups = ATT_KB // SUBLANES

    def scores(kb):
        start = kb * ATT_KB if isinstance(kb, int) else pl.multiple_of(kb * ATT_KB, ATT_KB)
        return jnp.dot(k_ref[pl.ds(start, ATT_KB), :], qst, preferred_element_type=_F32)

    def finish(l, acc):
        o = acc / jnp.sum(l, axis=0, keepdims=True)
        lv = lamv_ref[...]
        lam = (jnp.exp(jnp.sum(lv[0:1] * lv[1:2], axis=1, keepdims=True))
               - jnp.exp(jnp.sum(lv[2:3] * lv[3:4], axis=1, keepdims=True)) + LAM_INIT)
        od = o[:, :tq] - lam * o[:, tq:]
        r = lax.rsqrt(jnp.mean(od * od, axis=0, keepdims=True) + EPS)
        on = ((od * r).T * subg_ref[...]) * (1.0 - LAM_INIT)
        o_ref[...] = (on * ga_ref[...].astype(_F32)).astype(o_ref.dtype)

    l0 = jnp.zeros((SUBLANES, 2 * tq), _F32)
    a0 = jnp.zeros((V_DIM, 2 * tq), _F32)
    fixed_ok = shift_ref[1] > 0.5

    @pl.when(fixed_ok)
    def _():
        shift = shift_ref[0]
        l, acc = l0, a0
        for start in range(0, L_ALL, FAST_KB):
            size = min(FAST_KB, L_ALL - start)
            st = jnp.dot(k_ref[start:start + size, :], qst, preferred_element_type=_F32)
            p = jnp.exp2(st - shift)
            l = l + jnp.sum(p.reshape(size // SUBLANES, SUBLANES, 2 * tq), axis=0)
            pb = p.astype(_BF16)
            for off in range(0, size, ATT_KB):
                acc = acc + jnp.dot(vt_ref[0, (start + off) // ATT_KB], pb[off:off + ATT_KB, :],
                                    preferred_element_type=_F32)
        finish(l, acc)

    @pl.when(jnp.logical_not(fixed_ok))
    def _():
        def body(kb, carry):
            m, l, acc = carry
            st = scores(kb)
            mb = jnp.max(jnp.max(st.reshape(groups, SUBLANES, 2 * tq), axis=0), axis=0, keepdims=True)
            m_new = jnp.maximum(m, mb)
            alpha = jnp.exp2(m - m_new)
            p = jnp.exp2(st - m_new)
            l = alpha * l + jnp.sum(p.reshape(groups, SUBLANES, 2 * tq), axis=0)
            acc = alpha * acc + jnp.dot(vt_ref[0, kb], p.astype(_BF16), preferred_element_type=_F32)
            return m_new, l, acc

        m0 = jnp.full((1, 2 * tq), -1e30, _F32)
        _, l, acc = lax.fori_loop(0, N_KB, body, (m0, l0, a0))
        finish(l, acc)


def _attention(shift, lamv, q, k, vt, ga, subg):
    qrow = lambda h, i: (i, h)
    return pl.pallas_call(
        _attn_kernel,
        grid=(HEADS, SEQ // ATT_TQ),
        in_specs=[pl.BlockSpec(memory_space=pltpu.SMEM),
                  pl.BlockSpec(lamv.shape, lambda h, i: (0, 0)),
                  pl.BlockSpec((ATT_TQ, V_DIM), qrow),
                  pl.BlockSpec((L_ALL, V_DIM), lambda h, i: (0, h)),
                  pl.BlockSpec((1, N_KB, V_DIM, ATT_KB), lambda h, i: (h, 0, 0, 0)),
                  pl.BlockSpec((ATT_TQ, V_DIM), qrow),
                  pl.BlockSpec((1, V_DIM), lambda h, i: (0, 0))],
        out_specs=pl.BlockSpec((ATT_TQ, V_DIM), qrow),
        out_shape=jax.ShapeDtypeStruct((SEQ, ATT_W), _BF16),
        compiler_params=pltpu.CompilerParams(vmem_limit_bytes=VMEM_LIMIT),
        name="diff_attn",
    )(shift, lamv, q, k, vt, ga, subg)


def _conv_out_kernel(y_ref, yp_ref, yn_ref, cw_ref, cb_ref, lng_ref, lnb_ref, gc_ref, att_ref,
                     w_ref, x_ref, mod_ref, o_ref, shift_ref, conv_ref, wbf_ref):
    i = pl.program_id(0)
    n_tiles = pl.num_programs(0) - 1
    tile = jnp.minimum(i, n_tiles - 1)

    @pl.when(i == 0)
    def _():
        conv_ref[1] = jnp.zeros(conv_ref.shape[1:], conv_ref.dtype)
        wbf_ref[...] = w_ref[...].astype(wbf_ref.dtype)

    att = att_ref[...]
    conv_prev = conv_ref[(i + 1) % 2]
    ys, parts = [], []
    for c in range(CONV_CHUNKS):
        cols = slice(c * OUT_LANES, (c + 1) * OUT_LANES)
        ys.append(jnp.dot(att, wbf_ref[0:ATT_W, cols], preferred_element_type=_F32)
                  + jnp.dot(conv_prev, wbf_ref[ATT_W:, cols], preferred_element_type=_F32))
        parts.append(_conv_chunk(c, tile, n_tiles, y_ref, yp_ref, yn_ref, cw_ref, cb_ref, shift_ref))
    gate = mod_ref[0:1, 2 * D_MODEL:3 * D_MODEL]
    o_ref[...] = x_ref[...] + gate * jnp.concatenate(ys, axis=1)
    conv_ref[i % 2] = _conv_tail(parts, lng_ref, lnb_ref, gc_ref)


def _conv_out(yglu, cw, cb, lng, lnb, gc, att, w_out, x2, mod):
    tm = OUT_TM
    n_tiles = SEQ // tm
    nb = tm // CONV_HALO
    last = SEQ // CONV_HALO - 1
    cur = lambda i: (jnp.minimum(i, n_tiles - 1), 0)
    prev = lambda i: (jnp.maximum(i - 1, 0), 0)
    fixed = lambda i: (0, 0)
    fixed3 = lambda i: (0, 0, 0)
    cw3 = cw.reshape(CONV_K, CONV_CHUNKS, CONV_LANES).transpose(1, 0, 2)
    cb3 = cb.reshape(CONV_CHUNKS, 1, CONV_LANES)
    return pl.pallas_call(
        _conv_out_kernel,
        grid=(n_tiles + 1,),
        in_specs=[pl.BlockSpec((tm, CONV_W), cur),
                  pl.BlockSpec((CONV_HALO, CONV_W),
                               lambda i: (jnp.maximum(jnp.minimum(i, n_tiles - 1) * nb - 1, 0), 0)),
                  pl.BlockSpec((CONV_HALO, CONV_W),
                               lambda i: (jnp.minimum((jnp.minimum(i, n_tiles - 1) + 1) * nb, last), 0)),
                  pl.BlockSpec((CONV_CHUNKS, CONV_K, CONV_LANES), fixed3),
                  pl.BlockSpec((CONV_CHUNKS, 1, CONV_LANES), fixed3),
                  pl.BlockSpec((1, CONV_W), fixed),
                  pl.BlockSpec((1, CONV_W), fixed),
                  pl.BlockSpec((tm, CONV_W), lambda i: (jnp.minimum(i, n_tiles - 1), 1)),
                  pl.BlockSpec((tm, ATT_W), prev),
                  pl.BlockSpec((D_MODEL, D_MODEL), fixed, pipeline_mode=pl.Buffered(1)),
                  pl.BlockSpec((tm, D_MODEL), prev),
                  pl.BlockSpec((MOD_ROWS, 3 * D_MODEL), fixed)],
        out_specs=pl.BlockSpec((tm, D_MODEL), prev),
        out_shape=jax.ShapeDtypeStruct((SEQ, D_MODEL), _F32),
        scratch_shapes=[pltpu.VMEM((CONV_CHUNKS, SUBLANES, tm + CONV_SPAN_EXTRA, CONV_LANES), _F32),
                        pltpu.VMEM((2, tm, CONV_W), _BF16),
                        pltpu.VMEM((D_MODEL, D_MODEL), _BF16)],
        compiler_params=pltpu.CompilerParams(vmem_limit_bytes=CONV_OUT_VMEM_LIMIT,
                                             dimension_semantics=("arbitrary",)),
        name="conv_out",
    )(yglu, yglu, yglu, cw3, cb3, lng, lnb, gc, att, w_out, x2, mod)


def _rope_tables():
    rows = SEQ // GRID_W
    row = np.repeat(np.arange(rows), GRID_W).astype(np.float64)
    col = np.tile(np.arange(GRID_W), rows).astype(np.float64)
    nf = HEAD_DIM // 4
    inv = ROPE_BASE ** (-np.arange(nf, dtype=np.float64) / nf)
    ang_r = row[:, None] * inv
    ang_c = col[:, None] * inv
    ang = np.concatenate([ang_r, ang_r, ang_c, ang_c], axis=-1)
    cos, sin = np.cos(ang), np.sin(ang)
    first_half = (np.arange(HEAD_DIM) % (2 * nf)) < nf
    sa = np.where(first_half, -sin, 0.0)
    sb = np.where(first_half, 0.0, sin)
    pad = lambda t, v: np.concatenate([t, np.full((CTX_LEN, HEAD_DIM), v)], axis=0)
    two = lambda t: jnp.asarray(np.concatenate([t, t], axis=1), _F32)
    return two(pad(cos, 1.0)), two(pad(sa, 0.0)), two(pad(sb, 0.0))


def _chunk_avg():
    r = np.arange(NORM_GROUP) // HEAD_DIM
    return jnp.asarray(np.where(r[:, None] == r[None, :], 1.0 / HEAD_DIM, 0.0), _BF16)


def _softmax_shift(q_norm_g, k_norm_g, q_scale):
    bound = (HEAD_DIM * q_scale * SHIFT_MARGIN) * jnp.max(jnp.abs(q_norm_g)) * jnp.max(jnp.abs(k_norm_g))
    ok = (bound <= MAX_FIXED_SHIFT).astype(_F32)
    return jnp.stack([jnp.where(ok > 0.5, bound, 0.0), ok]).astype(_F32)


def kernel(x, c, ctx, c_ctx, w_ada, b_ada, norm_g, w_in, q_norm_g, k_norm_g, lam_q1, lam_k1,
           lam_q2, lam_k2, sub_norm_g, conv_w, conv_b, conv_ln_g, conv_ln_b, w_out):
    assert x.shape == (1, SEQ, D_MODEL) and ctx.shape == (1, CTX_LEN, D_MODEL)
    assert w_ada.shape[0] == 1 and w_in.shape == (1, D_MODEL, IN_W)
    x2 = x[0]
    ctx2 = ctx[0]

    cc = jnp.zeros((MOD_ROWS, D_MODEL), _F32).at[0].set(c[0]).at[1].set(c_ctx)
    mod = _adaln(cc, w_ada[0], b_ada)
    h = _modulate(x2, ctx2, mod, norm_g)

    w = w_in[0]
    cos, sa, sb = _rope_tables()
    q_scale = LOG2E / math.sqrt(HEAD_DIM)
    avg = _chunk_avg()
    reps = PROJ_TN // HEAD_DIM
    gq = jnp.tile(q_norm_g, (1, reps)) * q_scale
    gk = jnp.tile(k_norm_g, (1, reps))

    q, gates = _qg_proj(h, w, gq, avg, cos, sa, sb)
    k, vt = _kv_proj(h, w, gk, avg, cos, sa, sb)
    yglu = _glu_proj(h, w)

    lamv = jnp.concatenate([lam_q1, lam_k1, lam_q2, lam_k2], axis=0)
    shift = _softmax_shift(q_norm_g, k_norm_g, q_scale)
    att = _attention(shift, lamv, q, k, vt, gates, sub_norm_g)

    out = _conv_out(yglu, conv_w[0], conv_b, conv_ln_g, conv_ln_b, gates, att, w_out[0], x2, mod)
    return out[None]
```

```python
import math

import jax
import jax.numpy as jnp
import numpy as np
from jax import lax
from jax.experimental import pallas as pl
from jax.experimental.pallas import tpu as pltpu

D_MODEL = 2048
SEQ = 8192
GRID_W = 64
CTX_LEN = 256
L_ALL = SEQ + CTX_LEN
HEADS = 8
HEAD_DIM = 64
V_DIM = 2 * HEAD_DIM
ATT_W = HEADS * V_DIM
CONV_W = D_MODEL - ATT_W
CONV_K = 31
CONV_HALO = 16
ROPE_BASE = 10000.0
EPS = 1e-6
LAM_INIT = 0.8 - 0.6 * math.exp(-0.3 * 0)
LOG2E = math.log2(math.e)

Q0 = 0
K0 = Q0 + HEADS * 2 * HEAD_DIM
V0 = K0 + HEADS * 2 * HEAD_DIM
GA0 = V0 + ATT_W
U0 = GA0 + ATT_W
GC0 = U0 + 2 * CONV_W
IN_W = GC0 + CONV_W

SUBLANES = 8
MOD_ROWS = SUBLANES
ADA_TN = 1024
MOD_TM = 1024
MOD_ROWS_PER_PASS = 64
PROJ_TM = 1024
QK_ROWS = 256
QG_TM = 512
PROJ_TM_ALL = 768
PROJ_TN = 1024
ATT_TQ = 512
ATT_KB = PROJ_TM_ALL
N_KB = L_ALL // ATT_KB
FAST_KB = N_KB * ATT_KB
OUT_TM = 256
CONV_LANES = 128
CONV_CHUNKS = CONV_W // CONV_LANES
OUT_LANES = D_MODEL // CONV_CHUNKS
CONV_FIRST = CONV_HALO - CONV_K // 2
CONV_SPAN_EXTRA = SUBLANES * ((CONV_FIRST + CONV_K - 1) // SUBLANES)
NORM_GROUP = 256
MIB = 1024 * 1024
V7X_VMEM_BYTES = 64 * MIB
VMEM_LIMIT = 48 * MIB
CONV_OUT_VMEM_LIMIT = V7X_VMEM_BYTES - 6 * MIB
MAX_FIXED_SHIFT = 40.0
SHIFT_MARGIN = 1.01

_F32 = jnp.float32
_BF16 = jnp.bfloat16


def _sigmoid(x):
    return 1.0 / (1.0 + jnp.exp(-x))


def _silu(x):
    return x * _sigmoid(x)


def _adaln_kernel(cc_ref, w_ref, b_ref, o_ref):
    s = _silu(cc_ref[...]).astype(_BF16)
    o_ref[...] = jnp.dot(s, w_ref[...].astype(_BF16), preferred_element_type=_F32) + b_ref[...]


def _adaln(cc, w, b):
    n = w.shape[1]
    return pl.pallas_call(
        _adaln_kernel,
        grid=(n // ADA_TN,),
        in_specs=[pl.BlockSpec((MOD_ROWS, D_MODEL), lambda j: (0, 0)),
                  pl.BlockSpec((D_MODEL, ADA_TN), lambda j: (0, j)),
                  pl.BlockSpec((1, ADA_TN), lambda j: (0, j))],
        out_specs=pl.BlockSpec((MOD_ROWS, ADA_TN), lambda j: (0, j)),
        out_shape=jax.ShapeDtypeStruct((MOD_ROWS, n), _F32),
        compiler_params=pltpu.CompilerParams(vmem_limit_bytes=VMEM_LIMIT),
        name="adaln",
    )(cc, w, b)


def _modulate_rows(xin, g, shift, scale):
    r = lax.rsqrt(jnp.mean(xin * xin, axis=-1, keepdims=True) + EPS)
    return ((xin * r) * g) * (1.0 + scale) + shift


def _modulate_kernel(x_ref, ctx_ref, mod_ref, g_ref, h_ref):
    i = pl.program_id(0)
    n_lat = pl.num_programs(0) - 1
    g = g_ref[...]

    def run(src_ref, mod_row, n_rows):
        shift = mod_ref[mod_row:mod_row + 1, 0:D_MODEL]
        scale = mod_ref[mod_row:mod_row + 1, D_MODEL:2 * D_MODEL]

        def rows(r, carry):
            sl = pl.ds(pl.multiple_of(r * MOD_ROWS_PER_PASS, MOD_ROWS_PER_PASS), MOD_ROWS_PER_PASS)
            h_ref[sl, :] = _modulate_rows(src_ref[sl, :], g, shift, scale).astype(h_ref.dtype)
            return carry

        lax.fori_loop(0, n_rows // MOD_ROWS_PER_PASS, rows, 0)

    @pl.when(i < n_lat)
    def _():
        run(x_ref, 0, MOD_TM)

    @pl.when(i == n_lat)
    def _():
        run(ctx_ref, 1, CTX_LEN)


def _modulate(x2, ctx2, mod, g):
    n_lat = SEQ // MOD_TM
    return pl.pallas_call(
        _modulate_kernel,
        grid=(n_lat + 1,),
        in_specs=[pl.BlockSpec((MOD_TM, D_MODEL), lambda i: (jnp.minimum(i, n_lat - 1), 0)),
                  pl.BlockSpec((CTX_LEN, D_MODEL), lambda i: (0, 0)),
                  pl.BlockSpec((MOD_ROWS, 3 * D_MODEL), lambda i: (0, 0)),
                  pl.BlockSpec((1, D_MODEL), lambda i: (0, 0))],
        out_specs=pl.BlockSpec((MOD_TM, D_MODEL), lambda i: (i, 0)),
        out_shape=jax.ShapeDtypeStruct((L_ALL, D_MODEL), _BF16),
        compiler_params=pltpu.CompilerParams(vmem_limit_bytes=VMEM_LIMIT),
        name="modulate",
    )(x2, ctx2, mod, g)


def _row_parts(h_ref):
    return [slice(r0, r0 + QK_ROWS) for r0 in range(0, h_ref.shape[0], QK_ROWS)]


def _glu_kernel(h_ref, wa_ref, wg_ref, o_ref):
    wa = wa_ref[...].astype(_BF16)
    wg = wg_ref[...].astype(_BF16)
    for rows in _row_parts(h_ref):
        h = h_ref[rows, :]
        a = jnp.dot(h, wa, preferred_element_type=_F32)
        g = jnp.dot(h, wg, preferred_element_type=_F32)
        o_ref[rows, :] = (a * _sigmoid(g)).astype(o_ref.dtype)


def _glu_proj(h, w):
    tm = PROJ_TM
    once = pl.Buffered(1)
    wspec = lambda c0: pl.BlockSpec((D_MODEL, CONV_W), lambda i: (0, c0 // CONV_W), pipeline_mode=once)
    return pl.pallas_call(
        _glu_kernel,
        grid=(SEQ // tm,),
        in_specs=[pl.BlockSpec((tm, D_MODEL), lambda i: (i, 0)), wspec(U0), wspec(U0 + CONV_W)],
        out_specs=pl.BlockSpec((tm, CONV_W), lambda i: (i, 0)),
        out_shape=jax.ShapeDtypeStruct((SEQ, CONV_W), _BF16),
        compiler_params=pltpu.CompilerParams(vmem_limit_bytes=CONV_OUT_VMEM_LIMIT),
        name="proj_glu",
    )(h, w, w)


def _qk_epilogue(p, g, avg, cos, sa, sb):
    tn = p.shape[1]
    ss = p * p
    hi = ss.astype(_BF16)
    lo = (ss - hi.astype(_F32)).astype(_BF16)
    ms = []
    for c in range(tn // NORM_GROUP):
        sl = slice(c * NORM_GROUP, (c + 1) * NORM_GROUP)
        ms.append(jnp.dot(hi[:, sl], avg, preferred_element_type=_F32)
                  + jnp.dot(lo[:, sl], avg, preferred_element_type=_F32))
    ms = jnp.concatenate(ms, axis=1)
    y = (p * lax.rsqrt(ms + EPS)) * g
    reps = tn // cos.shape[1]
    cos, sa, sb = (jnp.concatenate([t] * reps, axis=1) for t in (cos, sa, sb))
    quarter = HEAD_DIM // 4
    return y * cos + pltpu.roll(y, tn - quarter, 1) * sa + pltpu.roll(y, quarter, 1) * sb


def _kv_kernel(h_ref, wk_ref, wv_ref, g_ref, avg_ref, cos_ref, sa_ref, sb_ref, k_ref, vt_ref):
    wk = wk_ref[...].astype(_BF16)
    wv = wv_ref[...].astype(_BF16)
    avg = avg_ref[...]
    g = g_ref[...]
    for rows in _row_parts(h_ref):
        hp = h_ref[rows, :]
        pk = jnp.dot(hp, wk, preferred_element_type=_F32)
        out = _qk_epilogue(pk, g, avg, cos_ref[rows, :], sa_ref[rows, :], sb_ref[rows, :])
        k_ref[rows, :] = out.astype(k_ref.dtype)
        pt = jnp.dot(hp, wv, preferred_element_type=_F32).T.astype(vt_ref.dtype)
        for hd in range(HEADS):
            vt_ref[hd, 0, :, rows] = pt[hd * V_DIM:(hd + 1) * V_DIM, :]


def _qg_kernel(h_ref, wq_ref, wga_ref, wgc_ref, g_ref, avg_ref, cos_ref, sa_ref, sb_ref, q_ref, gates_ref):
    wq = wq_ref[...].astype(_BF16)
    wga = wga_ref[...].astype(_BF16)
    wgc = wgc_ref[...].astype(_BF16)
    avg = avg_ref[...]
    g = g_ref[...]
    tn = wq.shape[1]
    for rows in _row_parts(h_ref):
        hp = h_ref[rows, :]
        pq = jnp.dot(hp, wq, preferred_element_type=_F32)
        out = _qk_epilogue(pq, g, avg, cos_ref[rows, :], sa_ref[rows, :], sb_ref[rows, :])
        q_ref[rows, :] = out.astype(q_ref.dtype)
        gates_ref[rows, 0:tn] = _silu(jnp.dot(hp, wga, preferred_element_type=_F32)).astype(gates_ref.dtype)
        gates_ref[rows, tn:] = _silu(jnp.dot(hp, wgc, preferred_element_type=_F32)).astype(gates_ref.dtype)


def _qg_proj(h, w, gq, avg, cos, sa, sb):
    tm = QG_TM
    fixed = lambda i: (0, 0)
    row = lambda i: (i, 0)
    once = pl.Buffered(1)
    wspec = lambda c0: pl.BlockSpec((D_MODEL, PROJ_TN), lambda i: (0, c0 // PROJ_TN), pipeline_mode=once)
    return pl.pallas_call(
        _qg_kernel,
        grid=(SEQ // tm,),
        in_specs=[pl.BlockSpec((tm, D_MODEL), row), wspec(Q0), wspec(GA0), wspec(GC0),
                  pl.BlockSpec((1, PROJ_TN), fixed),
                  pl.BlockSpec((NORM_GROUP, NORM_GROUP), fixed),
                  pl.BlockSpec((tm, 2 * HEAD_DIM), row),
                  pl.BlockSpec((tm, 2 * HEAD_DIM), row),
                  pl.BlockSpec((tm, 2 * HEAD_DIM), row)],
        out_specs=[pl.BlockSpec((tm, PROJ_TN), row),
                   pl.BlockSpec((tm, 2 * PROJ_TN), row)],
        out_shape=[jax.ShapeDtypeStruct((SEQ, PROJ_TN), _BF16),
                   jax.ShapeDtypeStruct((SEQ, 2 * PROJ_TN), _BF16)],
        compiler_params=pltpu.CompilerParams(vmem_limit_bytes=VMEM_LIMIT),
        name="proj_qg",
    )(h, w, w, w, gq, avg, cos, sa, sb)


def _kv_proj(h, w, gk, avg, cos, sa, sb):
    tm = PROJ_TM_ALL
    fixed = lambda i: (0, 0)
    row = lambda i: (i, 0)
    once = pl.Buffered(1)
    return pl.pallas_call(
        _kv_kernel,
        grid=(L_ALL // tm,),
        in_specs=[pl.BlockSpec((tm, D_MODEL), row),
                  pl.BlockSpec((D_MODEL, PROJ_TN), lambda i: (0, K0 // PROJ_TN), pipeline_mode=once),
                  pl.BlockSpec((D_MODEL, PROJ_TN), lambda i: (0, V0 // PROJ_TN), pipeline_mode=once),
                  pl.BlockSpec((1, PROJ_TN), fixed),
                  pl.BlockSpec((NORM_GROUP, NORM_GROUP), fixed),
                  pl.BlockSpec((tm, 2 * HEAD_DIM), row),
                  pl.BlockSpec((tm, 2 * HEAD_DIM), row),
                  pl.BlockSpec((tm, 2 * HEAD_DIM), row)],
        out_specs=[pl.BlockSpec((tm, PROJ_TN), row),
                   pl.BlockSpec((HEADS, 1, V_DIM, tm), lambda i: (0, i, 0, 0))],
        out_shape=[jax.ShapeDtypeStruct((L_ALL, PROJ_TN), _BF16),
                   jax.ShapeDtypeStruct((HEADS, N_KB, V_DIM, tm), _BF16)],
        compiler_params=pltpu.CompilerParams(vmem_limit_bytes=VMEM_LIMIT),
        name="proj_kv",
    )(h, w, w, gk, avg, cos, sa, sb)


def _conv_chunk(c, tile, n_tiles, y_ref, yp_ref, yn_ref, cw_ref, cb_ref, win_ref, shift_ref):
    tm = y_ref.shape[0]
    halo = CONV_HALO
    span = shift_ref.shape[2]
    keep_prev = (tile > 0).astype(_F32)
    keep_next = (tile < n_tiles - 1).astype(_F32)
    cols = slice(c * CONV_LANES, (c + 1) * CONV_LANES)
    win_ref[c, 0:halo, :] = yp_ref[:, cols].astype(_F32) * keep_prev
    win_ref[c, halo:halo + tm, :] = y_ref[:, cols].astype(_F32)
    win_ref[c, halo + tm:, :] = yn_ref[:, cols].astype(_F32) * keep_next
    for res in range(SUBLANES):
        shift_ref[c, res] = win_ref[c, res:res + span, :]
    part = jnp.zeros((tm, CONV_LANES), _F32) + cb_ref[c]
    for t in range(CONV_K):
        res = (CONV_FIRST + t) % SUBLANES
        a = (CONV_FIRST + t) - res
        part = part + shift_ref[c, res, a:a + tm, :] * cw_ref[c, t:t + 1, :]
    return part


def _conv_tail(parts, lng_ref, lnb_ref, gc_ref):
    acc = jnp.concatenate(parts, axis=1)
    mu = jnp.mean(acc, axis=-1, keepdims=True)
    cen = acc - mu
    var = jnp.mean(cen * cen, axis=-1, keepdims=True)
    ln = (cen * lax.rsqrt(var + EPS)) * lng_ref[...] + lnb_ref[...]
    return (_silu(ln) * gc_ref[...].astype(_F32)).astype(_BF16)


def _attn_kernel(shift_ref, lamv_ref, q_ref, k_ref, vt_ref, ga_ref, subg_ref, o_ref):
    tq = q_ref.shape[0]
    qt = q_ref[...].T
    row = lax.broadcasted_iota(jnp.int32, qt.shape, 0)
    zero = jnp.zeros_like(qt)
    qst = jnp.concatenate([jnp.where(row < HEAD_DIM, qt, zero),
                           jnp.where(row >= HEAD_DIM, qt, zero)], axis=1)
    groups = ATT_KB // SUBLANES

    def scores(kb):
        start = kb * ATT_KB if isinstance(kb, int) else pl.multiple_of(kb * ATT_KB, ATT_KB)
        return jnp.dot(k_ref[pl.ds(start, ATT_KB), :], qst, preferred_element_type=_F32)

    def finish(l, acc):
        o = acc / jnp.sum(l, axis=0, keepdims=True)
        lv = lamv_ref[...]
        lam = (jnp.exp(jnp.sum(lv[0:1] * lv[1:2], axis=1, keepdims=True))
               - jnp.exp(jnp.sum(lv[2:3] * lv[3:4], axis=1, keepdims=True)) + LAM_INIT)
        od = o[:, :tq] - lam * o[:, tq:]
        r = lax.rsqrt(jnp.mean(od * od, axis=0, keepdims=True) + EPS)
        on = ((od * r).T * subg_ref[...]) * (1.0 - LAM_INIT)
        o_ref[...] = (on * ga_ref[...].astype(_F32)).astype(o_ref.dtype)

    l0 = jnp.zeros((SUBLANES, 2 * tq), _F32)
    a0 = jnp.zeros((V_DIM, 2 * tq), _F32)
    fixed_ok = shift_ref[1] > 0.5

    @pl.when(fixed_ok)
    def _():
        shift = shift_ref[0]
        l, acc = l0, a0
        for start in range(0, L_ALL, FAST_KB):
            size = min(FAST_KB, L_ALL - start)
            st = jnp.dot(k_ref[start:start + size, :], qst, preferred_element_type=_F32)
            p = jnp.exp2(st - shift)
            l = l + jnp.sum(p.reshape(size // SUBLANES, SUBLANES, 2 * tq), axis=0)
            pb = p.astype(_BF16)
            for off in range(0, size, ATT_KB):
                acc = acc + jnp.dot(vt_ref[0, (start + off) // ATT_KB], pb[off:off + ATT_KB, :],
                                    preferred_element_type=_F32)
        finish(l, acc)

    @pl.when(jnp.logical_not(fixed_ok))
    def _():
        def body(kb, carry):
            m, l, acc = carry
            st = scores(kb)
            mb = jnp.max(jnp.max(st.reshape(groups, SUBLANES, 2 * tq), axis=0), axis=0, keepdims=True)
            m_new = jnp.maximum(m, mb)
            alpha = jnp.exp2(m - m_new)
            p = jnp.exp2(st - m_new)
            l = alpha * l + jnp.sum(p.reshape(groups, SUBLANES, 2 * tq), axis=0)
            acc = alpha * acc + jnp.dot(vt_ref[0, kb], p.astype(_BF16), preferred_element_type=_F32)
            return m_new, l, acc

        m0 = jnp.full((1, 2 * tq), -1e30, _F32)
        _, l, acc = lax.fori_loop(0, N_KB, body, (m0, l0, a0))
        finish(l, acc)


def _attention(shift, lamv, q, k, vt, ga, subg):
    qrow = lambda h, i: (i, h)
    return pl.pallas_call(
        _attn_kernel,
        grid=(HEADS, SEQ // ATT_TQ),
        in_specs=[pl.BlockSpec(memory_space=pltpu.SMEM),
                  pl.BlockSpec(lamv.shape, lambda h, i: (0, 0)),
                  pl.BlockSpec((ATT_TQ, V_DIM), qrow),
                  pl.BlockSpec((L_ALL, V_DIM), lambda h, i: (0, h)),
                  pl.BlockSpec((1, N_KB, V_DIM, ATT_KB), lambda h, i: (h, 0, 0, 0)),
                  pl.BlockSpec((ATT_TQ, V_DIM), qrow),
                  pl.BlockSpec((1, V_DIM), lambda h, i: (0, 0))],
        out_specs=pl.BlockSpec((ATT_TQ, V_DIM), qrow),
        out_shape=jax.ShapeDtypeStruct((SEQ, ATT_W), _BF16),
        compiler_params=pltpu.CompilerParams(vmem_limit_bytes=VMEM_LIMIT),
        name="diff_attn",
    )(shift, lamv, q, k, vt, ga, subg)


def _conv_out_kernel(y_ref, yp_ref, yn_ref, cw_ref, cb_ref, lng_ref, lnb_ref, gc_ref, att_ref,
                     w_ref, x_ref, mod_ref, o_ref, win_ref, shift_ref, conv_ref, wbf_ref):
    i = pl.program_id(0)
    n_tiles = pl.num_programs(0) - 1
    tile = jnp.minimum(i, n_tiles - 1)

    @pl.when(i == 0)
    def _():
        conv_ref[1] = jnp.zeros(conv_ref.shape[1:], conv_ref.dtype)
        wbf_ref[...] = w_ref[...].astype(wbf_ref.dtype)

    att = att_ref[...]
    conv_prev = conv_ref[(i + 1) % 2]
    ys, parts = [], []
    for c in range(CONV_CHUNKS):
        cols = slice(c * OUT_LANES, (c + 1) * OUT_LANES)
        ys.append(jnp.dot(att, wbf_ref[0:ATT_W, cols], preferred_element_type=_F32)
                  + jnp.dot(conv_prev, wbf_ref[ATT_W:, cols], preferred_element_type=_F32))
        parts.append(_conv_chunk(c, tile, n_tiles, y_ref, yp_ref, yn_ref, cw_ref, cb_ref,
                                 win_ref, shift_ref))
    gate = mod_ref[0:1, 2 * D_MODEL:3 * D_MODEL]
    o_ref[...] = x_ref[...] + gate * jnp.concatenate(ys, axis=1)
    conv_ref[i % 2] = _conv_tail(parts, lng_ref, lnb_ref, gc_ref)


def _conv_out(yglu, cw, cb, lng, lnb, gc, att, w_out, x2, mod):
    tm = OUT_TM
    n_tiles = SEQ // tm
    nb = tm // CONV_HALO
    last = SEQ // CONV_HALO - 1
    cur = lambda i: (jnp.minimum(i, n_tiles - 1), 0)
    prev = lambda i: (jnp.maximum(i - 1, 0), 0)
    fixed = lambda i: (0, 0)
    fixed3 = lambda i: (0, 0, 0)
    cw3 = cw.reshape(CONV_K, CONV_CHUNKS, CONV_LANES).transpose(1, 0, 2)
    cb3 = cb.reshape(CONV_CHUNKS, 1, CONV_LANES)
    return pl.pallas_call(
        _conv_out_kernel,
        grid=(n_tiles + 1,),
        in_specs=[pl.BlockSpec((tm, CONV_W), cur),
                  pl.BlockSpec((CONV_HALO, CONV_W),
                               lambda i: (jnp.maximum(jnp.minimum(i, n_tiles - 1) * nb - 1, 0), 0)),
                  pl.BlockSpec((CONV_HALO, CONV_W),
                               lambda i: (jnp.minimum((jnp.minimum(i, n_tiles - 1) + 1) * nb, last), 0)),
                  pl.BlockSpec((CONV_CHUNKS, CONV_K, CONV_LANES), fixed3),
                  pl.BlockSpec((CONV_CHUNKS, 1, CONV_LANES), fixed3),
                  pl.BlockSpec((1, CONV_W), fixed),
                  pl.BlockSpec((1, CONV_W), fixed),
                  pl.BlockSpec((tm, CONV_W), lambda i: (jnp.minimum(i, n_tiles - 1), 1)),
                  pl.BlockSpec((tm, ATT_W), prev),
                  pl.BlockSpec((D_MODEL, D_MODEL), fixed, pipeline_mode=pl.Buffered(1)),
                  pl.BlockSpec((tm, D_MODEL), prev),
                  pl.BlockSpec((MOD_ROWS, 3 * D_MODEL), fixed)],
        out_specs=pl.BlockSpec((tm, D_MODEL), prev),
        out_shape=jax.ShapeDtypeStruct((SEQ, D_MODEL), _F32),
        scratch_shapes=[pltpu.VMEM((CONV_CHUNKS, tm + 2 * CONV_HALO, CONV_LANES), _F32),
                        pltpu.VMEM((CONV_CHUNKS, SUBLANES, tm + CONV_SPAN_EXTRA, CONV_LANES), _F32),
                        pltpu.VMEM((2, tm, CONV_W), _BF16),
                        pltpu.VMEM((D_MODEL, D_MODEL), _BF16)],
        compiler_params=pltpu.CompilerParams(vmem_limit_bytes=CONV_OUT_VMEM_LIMIT,
                                             dimension_semantics=("arbitrary",)),
        name="conv_out",
    )(yglu, yglu, yglu, cw3, cb3, lng, lnb, gc, att, w_out, x2, mod)


def _rope_tables():
    rows = SEQ // GRID_W
    row = np.repeat(np.arange(rows), GRID_W).astype(np.float64)
    col = np.tile(np.arange(GRID_W), rows).astype(np.float64)
    nf = HEAD_DIM // 4
    inv = ROPE_BASE ** (-np.arange(nf, dtype=np.float64) / nf)
    ang_r = row[:, None] * inv
    ang_c = col[:, None] * inv
    ang = np.concatenate([ang_r, ang_r, ang_c, ang_c], axis=-1)
    cos, sin = np.cos(ang), np.sin(ang)
    first_half = (np.arange(HEAD_DIM) % (2 * nf)) < nf
    sa = np.where(first_half, -sin, 0.0)
    sb = np.where(first_half, 0.0, sin)
    pad = lambda t, v: np.concatenate([t, np.full((CTX_LEN, HEAD_DIM), v)], axis=0)
    two = lambda t: jnp.asarray(np.concatenate([t, t], axis=1), _F32)
    return two(pad(cos, 1.0)), two(pad(sa, 0.0)), two(pad(sb, 0.0))


def _chunk_avg():
    r = np.arange(NORM_GROUP) // HEAD_DIM
    return jnp.asarray(np.where(r[:, None] == r[None, :], 1.0 / HEAD_DIM, 0.0), _BF16)


def _softmax_shift(q_norm_g, k_norm_g, q_scale):
    bound = (HEAD_DIM * q_scale * SHIFT_MARGIN) * jnp.max(jnp.abs(q_norm_g)) * jnp.max(jnp.abs(k_norm_g))
    ok = (bound <= MAX_FIXED_SHIFT).astype(_F32)
    return jnp.stack([jnp.where(ok > 0.5, bound, 0.0), ok]).astype(_F32)


def kernel(x, c, ctx, c_ctx, w_ada, b_ada, norm_g, w_in, q_norm_g, k_norm_g, lam_q1, lam_k1,
           lam_q2, lam_k2, sub_norm_g, conv_w, conv_b, conv_ln_g, conv_ln_b, w_out):
    assert x.shape == (1, SEQ, D_MODEL) and ctx.shape == (1, CTX_LEN, D_MODEL)
    assert w_ada.shape[0] == 1 and w_in.shape == (1, D_MODEL, IN_W)
    x2 = x[0]
    ctx2 = ctx[0]

    cc = jnp.zeros((MOD_ROWS, D_MODEL), _F32).at[0].set(c[0]).at[1].set(c_ctx)
    mod = _adaln(cc, w_ada[0], b_ada)
    h = _modulate(x2, ctx2, mod, norm_g)

    w = w_in[0]
    cos, sa, sb = _rope_tables()
    q_scale = LOG2E / math.sqrt(HEAD_DIM)
    avg = _chunk_avg()
    reps = PROJ_TN // HEAD_DIM
    gq = jnp.tile(q_norm_g, (1, reps)) * q_scale
    gk = jnp.tile(k_norm_g, (1, reps))

    q, gates = _qg_proj(h, w, gq, avg, cos, sa, sb)
    k, vt = _kv_proj(h, w, gk, avg, cos, sa, sb)
    yglu = _glu_proj(h, w)

    lamv = jnp.concatenate([lam_q1, lam_k1, lam_q2, lam_k2], axis=0)
    shift = _softmax_shift(q_norm_g, k_norm_g, q_scale)
    att = _attention(shift, lamv, q, k, vt, gates, sub_norm_g)

    out = _conv_out(yglu, conv_w[0], conv_b, conv_ln_g, conv_ln_b, gates, att, w_out[0], x2, mod)
    return out[None]
```

```python
import math

import jax
import jax.numpy as jnp
import numpy as np
from jax import lax
from jax.experimental import pallas as pl
from jax.experimental.pallas import tpu as pltpu

D_MODEL = 2048
SEQ = 8192
GRID_W = 64
CTX_LEN = 256
L_ALL = SEQ + CTX_LEN
HEADS = 8
HEAD_DIM = 64
V_DIM = 2 * HEAD_DIM
ATT_W = HEADS * V_DIM
CONV_W = D_MODEL - ATT_W
CONV_K = 31
CONV_HALO = 16
ROPE_BASE = 10000.0
EPS = 1e-6
LAM_INIT = 0.8 - 0.6 * math.exp(-0.3 * 0)
LOG2E = math.log2(math.e)

Q0 = 0
K0 = Q0 + HEADS * 2 * HEAD_DIM
V0 = K0 + HEADS * 2 * HEAD_DIM
GA0 = V0 + ATT_W
U0 = GA0 + ATT_W
GC0 = U0 + 2 * CONV_W
IN_W = GC0 + CONV_W

SUBLANES = 8
MOD_ROWS = SUBLANES
ADA_TN = 1024
MOD_TM = 1024
MOD_ROWS_PER_PASS = 64
PROJ_TM = 1024
QK_ROWS = 256
QG_TM = 512
PROJ_TM_ALL = 768
PROJ_TN = 1024
GLU_TN = 512
ATT_TQ = 512
ATT_KB = PROJ_TM_ALL
N_KB = L_ALL // ATT_KB
FAST_KB = N_KB * ATT_KB
OUT_TM = 256
CONV_LANES = 128
CONV_CHUNKS = CONV_W // CONV_LANES
OUT_LANES = D_MODEL // CONV_CHUNKS
CONV_FIRST = CONV_HALO - CONV_K // 2
CONV_SPAN_EXTRA = SUBLANES * ((CONV_FIRST + CONV_K - 1) // SUBLANES)
NORM_GROUP = 256
MIB = 1024 * 1024
V7X_VMEM_BYTES = 64 * MIB
VMEM_LIMIT = 48 * MIB
SMALL_VMEM_LIMIT = 32 * MIB
CONV_OUT_VMEM_LIMIT = V7X_VMEM_BYTES - 6 * MIB
MAX_FIXED_SHIFT = 40.0
SHIFT_MARGIN = 1.01

_F32 = jnp.float32
_BF16 = jnp.bfloat16


def _sigmoid(x):
    return 1.0 / (1.0 + jnp.exp(-x))


def _silu(x):
    return x * _sigmoid(x)


def _adaln_kernel(cc_ref, w_ref, b_ref, o_ref):
    s = _silu(cc_ref[...]).astype(_BF16)
    o_ref[...] = jnp.dot(s, w_ref[...].astype(_BF16), preferred_element_type=_F32) + b_ref[...]


def _adaln(cc, w, b):
    n = w.shape[1]
    return pl.pallas_call(
        _adaln_kernel,
        grid=(n // ADA_TN,),
        in_specs=[pl.BlockSpec((MOD_ROWS, D_MODEL), lambda j: (0, 0)),
                  pl.BlockSpec((D_MODEL, ADA_TN), lambda j: (0, j)),
                  pl.BlockSpec((1, ADA_TN), lambda j: (0, j))],
        out_specs=pl.BlockSpec((MOD_ROWS, ADA_TN), lambda j: (0, j)),
        out_shape=jax.ShapeDtypeStruct((MOD_ROWS, n), _F32),
        compiler_params=pltpu.CompilerParams(vmem_limit_bytes=SMALL_VMEM_LIMIT),
        name="adaln",
    )(cc, w, b)


def _modulate_rows(xin, g, shift, scale):
    r = lax.rsqrt(jnp.mean(xin * xin, axis=-1, keepdims=True) + EPS)
    return ((xin * r) * g) * (1.0 + scale) + shift


def _modulate_kernel(x_ref, ctx_ref, mod_ref, g_ref, h_ref):
    i = pl.program_id(0)
    n_lat = pl.num_programs(0) - 1
    g = g_ref[...]

    def run(src_ref, mod_row, n_rows):
        shift = mod_ref[mod_row:mod_row + 1, 0:D_MODEL]
        scale = mod_ref[mod_row:mod_row + 1, D_MODEL:2 * D_MODEL]

        def rows(r, carry):
            sl = pl.ds(pl.multiple_of(r * MOD_ROWS_PER_PASS, MOD_ROWS_PER_PASS), MOD_ROWS_PER_PASS)
            h_ref[sl, :] = _modulate_rows(src_ref[sl, :], g, shift, scale).astype(h_ref.dtype)
            return carry

        lax.fori_loop(0, n_rows // MOD_ROWS_PER_PASS, rows, 0)

    @pl.when(i < n_lat)
    def _():
        run(x_ref, 0, MOD_TM)

    @pl.when(i == n_lat)
    def _():
        run(ctx_ref, 1, CTX_LEN)


def _modulate(x2, ctx2, mod, g):
    n_lat = SEQ // MOD_TM
    return pl.pallas_call(
        _modulate_kernel,
        grid=(n_lat + 1,),
        in_specs=[pl.BlockSpec((MOD_TM, D_MODEL), lambda i: (jnp.minimum(i, n_lat - 1), 0)),
                  pl.BlockSpec((CTX_LEN, D_MODEL), lambda i: (0, 0)),
                  pl.BlockSpec((MOD_ROWS, 3 * D_MODEL), lambda i: (0, 0)),
                  pl.BlockSpec((1, D_MODEL), lambda i: (0, 0))],
        out_specs=pl.BlockSpec((MOD_TM, D_MODEL), lambda i: (i, 0)),
        out_shape=jax.ShapeDtypeStruct((L_ALL, D_MODEL), _BF16),
        compiler_params=pltpu.CompilerParams(vmem_limit_bytes=SMALL_VMEM_LIMIT),
        name="modulate",
    )(x2, ctx2, mod, g)


def _row_parts(h_ref):
    return [slice(r0, r0 + QK_ROWS) for r0 in range(0, h_ref.shape[0], QK_ROWS)]


def _glu_kernel(h_ref, wa_ref, wg_ref, o_ref):
    wa = wa_ref[...].astype(_BF16)
    wg = wg_ref[...].astype(_BF16)
    for rows in _row_parts(h_ref):
        h = h_ref[rows, :]
        a = jnp.dot(h, wa, preferred_element_type=_F32)
        g = jnp.dot(h, wg, preferred_element_type=_F32)
        o_ref[rows, :] = (a * _sigmoid(g)).astype(o_ref.dtype)


def _glu_proj(h, w):
    tm, tn = PROJ_TM, GLU_TN
    wspec = lambda c0: pl.BlockSpec((D_MODEL, tn), lambda j, i: (0, c0 // tn + j))
    return pl.pallas_call(
        _glu_kernel,
        grid=(CONV_W // tn, SEQ // tm),
        in_specs=[pl.BlockSpec((tm, D_MODEL), lambda j, i: (i, 0)), wspec(U0), wspec(U0 + CONV_W)],
        out_specs=pl.BlockSpec((tm, tn), lambda j, i: (i, j)),
        out_shape=jax.ShapeDtypeStruct((SEQ, CONV_W), _BF16),
        compiler_params=pltpu.CompilerParams(vmem_limit_bytes=VMEM_LIMIT),
        name="proj_glu",
    )(h, w, w)


def _qk_epilogue(p, g, avg, cos, sa, sb):
    tn = p.shape[1]
    ss = p * p
    hi = ss.astype(_BF16)
    lo = (ss - hi.astype(_F32)).astype(_BF16)
    ms = []
    for c in range(tn // NORM_GROUP):
        sl = slice(c * NORM_GROUP, (c + 1) * NORM_GROUP)
        ms.append(jnp.dot(hi[:, sl], avg, preferred_element_type=_F32)
                  + jnp.dot(lo[:, sl], avg, preferred_element_type=_F32))
    ms = jnp.concatenate(ms, axis=1)
    y = (p * lax.rsqrt(ms + EPS)) * g
    reps = tn // cos.shape[1]
    cos, sa, sb = (jnp.concatenate([t] * reps, axis=1) for t in (cos, sa, sb))
    quarter = HEAD_DIM // 4
    return y * cos + pltpu.roll(y, tn - quarter, 1) * sa + pltpu.roll(y, quarter, 1) * sb


def _kv_kernel(h_ref, wk_ref, wv_ref, g_ref, avg_ref, cos_ref, sa_ref, sb_ref, k_ref, vt_ref):
    wk = wk_ref[...].astype(_BF16)
    wv = wv_ref[...].astype(_BF16)
    avg = avg_ref[...]
    g = g_ref[...]
    for rows in _row_parts(h_ref):
        hp = h_ref[rows, :]
        pk = jnp.dot(hp, wk, preferred_element_type=_F32)
        out = _qk_epilogue(pk, g, avg, cos_ref[rows, :], sa_ref[rows, :], sb_ref[rows, :])
        k_ref[rows, :] = out.astype(k_ref.dtype)
        pt = jnp.dot(hp, wv, preferred_element_type=_F32).T.astype(vt_ref.dtype)
        for hd in range(HEADS):
            vt_ref[hd, 0, :, rows] = pt[hd * V_DIM:(hd + 1) * V_DIM, :]


def _qg_kernel(h_ref, wq_ref, wga_ref, wgc_ref, g_ref, avg_ref, cos_ref, sa_ref, sb_ref, q_ref, gates_ref):
    wq = wq_ref[...].astype(_BF16)
    wga = wga_ref[...].astype(_BF16)
    wgc = wgc_ref[...].astype(_BF16)
    avg = avg_ref[...]
    g = g_ref[...]
    tn = wq.shape[1]
    for rows in _row_parts(h_ref):
        hp = h_ref[rows, :]
        pq = jnp.dot(hp, wq, preferred_element_type=_F32)
        out = _qk_epilogue(pq, g, avg, cos_ref[rows, :], sa_ref[rows, :], sb_ref[rows, :])
        q_ref[rows, :] = out.astype(q_ref.dtype)
        gates_ref[rows, 0:tn] = _silu(jnp.dot(hp, wga, preferred_element_type=_F32)).astype(gates_ref.dtype)
        gates_ref[rows, tn:] = _silu(jnp.dot(hp, wgc, preferred_element_type=_F32)).astype(gates_ref.dtype)


def _qg_proj(h, w, gq, avg, cos, sa, sb):
    tm = QG_TM
    fixed = lambda i: (0, 0)
    row = lambda i: (i, 0)
    once = pl.Buffered(1)
    wspec = lambda c0: pl.BlockSpec((D_MODEL, PROJ_TN), lambda i: (0, c0 // PROJ_TN), pipeline_mode=once)
    return pl.pallas_call(
        _qg_kernel,
        grid=(SEQ // tm,),
        in_specs=[pl.BlockSpec((tm, D_MODEL), row), wspec(Q0), wspec(GA0), wspec(GC0),
                  pl.BlockSpec((1, PROJ_TN), fixed),
                  pl.BlockSpec((NORM_GROUP, NORM_GROUP), fixed),
                  pl.BlockSpec((tm, 2 * HEAD_DIM), row),
                  pl.BlockSpec((tm, 2 * HEAD_DIM), row),
                  pl.BlockSpec((tm, 2 * HEAD_DIM), row)],
        out_specs=[pl.BlockSpec((tm, PROJ_TN), row),
                   pl.BlockSpec((tm, 2 * PROJ_TN), row)],
        out_shape=[jax.ShapeDtypeStruct((SEQ, PROJ_TN), _BF16),
                   jax.ShapeDtypeStruct((SEQ, 2 * PROJ_TN), _BF16)],
        compiler_params=pltpu.CompilerParams(vmem_limit_bytes=VMEM_LIMIT),
        name="proj_qg",
    )(h, w, w, w, gq, avg, cos, sa, sb)


def _kv_proj(h, w, gk, avg, cos, sa, sb):
    tm = PROJ_TM_ALL
    fixed = lambda i: (0, 0)
    row = lambda i: (i, 0)
    once = pl.Buffered(1)
    return pl.pallas_call(
        _kv_kernel,
        grid=(L_ALL // tm,),
        in_specs=[pl.BlockSpec((tm, D_MODEL), row),
                  pl.BlockSpec((D_MODEL, PROJ_TN), lambda i: (0, K0 // PROJ_TN), pipeline_mode=once),
                  pl.BlockSpec((D_MODEL, PROJ_TN), lambda i: (0, V0 // PROJ_TN), pipeline_mode=once),
                  pl.BlockSpec((1, PROJ_TN), fixed),
                  pl.BlockSpec((NORM_GROUP, NORM_GROUP), fixed),
                  pl.BlockSpec((tm, 2 * HEAD_DIM), row),
                  pl.BlockSpec((tm, 2 * HEAD_DIM), row),
                  pl.BlockSpec((tm, 2 * HEAD_DIM), row)],
        out_specs=[pl.BlockSpec((tm, PROJ_TN), row),
                   pl.BlockSpec((HEADS, 1, V_DIM, tm), lambda i: (0, i, 0, 0))],
        out_shape=[jax.ShapeDtypeStruct((L_ALL, PROJ_TN), _BF16),
                   jax.ShapeDtypeStruct((HEADS, N_KB, V_DIM, tm), _BF16)],
        compiler_params=pltpu.CompilerParams(vmem_limit_bytes=VMEM_LIMIT),
        name="proj_kv",
    )(h, w, w, gk, avg, cos, sa, sb)


def _conv_chunk(c, tile, n_tiles, y_ref, yp_ref, yn_ref, cw_ref, cb_ref, win_ref, shift_ref):
    tm = y_ref.shape[0]
    halo = CONV_HALO
    span = shift_ref.shape[2]
    keep_prev = (tile > 0).astype(_F32)
    keep_next = (tile < n_tiles - 1).astype(_F32)
    cols = slice(c * CONV_LANES, (c + 1) * CONV_LANES)
    win_ref[c, 0:halo, :] = yp_ref[:, cols].astype(_F32) * keep_prev
    win_ref[c, halo:halo + tm, :] = y_ref[:, cols].astype(_F32)
    win_ref[c, halo + tm:, :] = yn_ref[:, cols].astype(_F32) * keep_next
    for res in range(SUBLANES):
        shift_ref[c, res] = win_ref[c, res:res + span, :]
    part = jnp.zeros((tm, CONV_LANES), _F32) + cb_ref[c]
    for t in range(CONV_K):
        res = (CONV_FIRST + t) % SUBLANES
        a = (CONV_FIRST + t) - res
        part = part + shift_ref[c, res, a:a + tm, :] * cw_ref[c, t:t + 1, :]
    return part


def _conv_tail(parts, lng_ref, lnb_ref, gc_ref):
    acc = jnp.concatenate(parts, axis=1)
    mu = jnp.mean(acc, axis=-1, keepdims=True)
    cen = acc - mu
    var = jnp.mean(cen * cen, axis=-1, keepdims=True)
    ln = (cen * lax.rsqrt(var + EPS)) * lng_ref[...] + lnb_ref[...]
    return (_silu(ln) * gc_ref[...].astype(_F32)).astype(_BF16)


def _attn_kernel(shift_ref, lamv_ref, q_ref, k_ref, vt_ref, ga_ref, subg_ref, o_ref):
    tq = q_ref.shape[0]
    qt = q_ref[...].T
    row = lax.broadcasted_iota(jnp.int32, qt.shape, 0)
    zero = jnp.zeros_like(qt)
    qst = jnp.concatenate([jnp.where(row < HEAD_DIM, qt, zero),
                           jnp.where(row >= HEAD_DIM, qt, zero)], axis=1)
    groups = ATT_KB // SUBLANES

    def scores(kb):
        start = kb * ATT_KB if isinstance(kb, int) else pl.multiple_of(kb * ATT_KB, ATT_KB)
        return jnp.dot(k_ref[pl.ds(start, ATT_KB), :], qst, preferred_element_type=_F32)

    def finish(l, acc):
        o = acc / jnp.sum(l, axis=0, keepdims=True)
        lv = lamv_ref[...]
        lam = (jnp.exp(jnp.sum(lv[0:1] * lv[1:2], axis=1, keepdims=True))
               - jnp.exp(jnp.sum(lv[2:3] * lv[3:4], axis=1, keepdims=True)) + LAM_INIT)
        od = o[:, :tq] - lam * o[:, tq:]
        r = lax.rsqrt(jnp.mean(od * od, axis=0, keepdims=True) + EPS)
        on = ((od * r).T * subg_ref[...]) * (1.0 - LAM_INIT)
        o_ref[...] = (on * ga_ref[...].astype(_F32)).astype(o_ref.dtype)

    l0 = jnp.zeros((SUBLANES, 2 * tq), _F32)
    a0 = jnp.zeros((V_DIM, 2 * tq), _F32)
    fixed_ok = shift_ref[1] > 0.5

    @pl.when(fixed_ok)
    def _():
        shift = shift_ref[0]
        l, acc = l0, a0
        for start in range(0, L_ALL, FAST_KB):
            size = min(FAST_KB, L_ALL - start)
            st = jnp.dot(k_ref[start:start + size, :], qst, preferred_element_type=_F32)
            p = jnp.exp2(st - shift)
            l = l + jnp.sum(p.reshape(size // SUBLANES, SUBLANES, 2 * tq), axis=0)
            pb = p.astype(_BF16)
            for off in range(0, size, ATT_KB):
                acc = acc + jnp.dot(vt_ref[0, (start + off) // ATT_KB], pb[off:off + ATT_KB, :],
                                    preferred_element_type=_F32)
        finish(l, acc)

    @pl.when(jnp.logical_not(fixed_ok))
    def _():
        def body(kb, carry):
            m, l, acc = carry
            st = scores(kb)
            mb = jnp.max(jnp.max(st.reshape(groups, SUBLANES, 2 * tq), axis=0), axis=0, keepdims=True)
            m_new = jnp.maximum(m, mb)
            alpha = jnp.exp2(m - m_new)
            p = jnp.exp2(st - m_new)
            l = alpha * l + jnp.sum(p.reshape(groups, SUBLANES, 2 * tq), axis=0)
            acc = alpha * acc + jnp.dot(vt_ref[0, kb], p.astype(_BF16), preferred_element_type=_F32)
            return m_new, l, acc

        m0 = jnp.full((1, 2 * tq), -1e30, _F32)
        _, l, acc = lax.fori_loop(0, N_KB, body, (m0, l0, a0))
        finish(l, acc)


def _attention(shift, lamv, q, k, vt, ga, subg):
    qrow = lambda h, i: (i, h)
    return pl.pallas_call(
        _attn_kernel,
        grid=(HEADS, SEQ // ATT_TQ),
        in_specs=[pl.BlockSpec(memory_space=pltpu.SMEM),
                  pl.BlockSpec(lamv.shape, lambda h, i: (0, 0)),
                  pl.BlockSpec((ATT_TQ, V_DIM), qrow),
                  pl.BlockSpec((L_ALL, V_DIM), lambda h, i: (0, h)),
                  pl.BlockSpec((1, N_KB, V_DIM, ATT_KB), lambda h, i: (h, 0, 0, 0)),
                  pl.BlockSpec((ATT_TQ, V_DIM), qrow),
                  pl.BlockSpec((1, V_DIM), lambda h, i: (0, 0))],
        out_specs=pl.BlockSpec((ATT_TQ, V_DIM), qrow),
        out_shape=jax.ShapeDtypeStruct((SEQ, ATT_W), _BF16),
        compiler_params=pltpu.CompilerParams(vmem_limit_bytes=VMEM_LIMIT),
        name="diff_attn",
    )(shift, lamv, q, k, vt, ga, subg)


def _conv_out_kernel(y_ref, yp_ref, yn_ref, cw_ref, cb_ref, lng_ref, lnb_ref, gc_ref, att_ref,
                     w_ref, x_ref, mod_ref, o_ref, win_ref, shift_ref, conv_ref, wbf_ref):
    i = pl.program_id(0)
    n_tiles = pl.num_programs(0) - 1
    tile = jnp.minimum(i, n_tiles - 1)

    @pl.when(i == 0)
    def _():
        conv_ref[1] = jnp.zeros(conv_ref.shape[1:], conv_ref.dtype)
        wbf_ref[...] = w_ref[...].astype(wbf_ref.dtype)

    att = att_ref[...]
    conv_prev = conv_ref[(i + 1) % 2]
    ys, parts = [], []
    for c in range(CONV_CHUNKS):
        cols = slice(c * OUT_LANES, (c + 1) * OUT_LANES)
        ys.append(jnp.dot(att, wbf_ref[0:ATT_W, cols], preferred_element_type=_F32)
                  + jnp.dot(conv_prev, wbf_ref[ATT_W:, cols], preferred_element_type=_F32))
        parts.append(_conv_chunk(c, tile, n_tiles, y_ref, yp_ref, yn_ref, cw_ref, cb_ref,
                                 win_ref, shift_ref))
    gate = mod_ref[0:1, 2 * D_MODEL:3 * D_MODEL]
    o_ref[...] = x_ref[...] + gate * jnp.concatenate(ys, axis=1)
    conv_ref[i % 2] = _conv_tail(parts, lng_ref, lnb_ref, gc_ref)


def _conv_out(yglu, cw, cb, lng, lnb, gc, att, w_out, x2, mod):
    tm = OUT_TM
    n_tiles = SEQ // tm
    nb = tm // CONV_HALO
    last = SEQ // CONV_HALO - 1
    cur = lambda i: (jnp.minimum(i, n_tiles - 1), 0)
    prev = lambda i: (jnp.maximum(i - 1, 0), 0)
    fixed = lambda i: (0, 0)
    fixed3 = lambda i: (0, 0, 0)
    cw3 = cw.reshape(CONV_K, CONV_CHUNKS, CONV_LANES).transpose(1, 0, 2)
    cb3 = cb.reshape(CONV_CHUNKS, 1, CONV_LANES)
    return pl.pallas_call(
        _conv_out_kernel,
        grid=(n_tiles + 1,),
        in_specs=[pl.BlockSpec((tm, CONV_W), cur),
                  pl.BlockSpec((CONV_HALO, CONV_W),
                               lambda i: (jnp.maximum(jnp.minimum(i, n_tiles - 1) * nb - 1, 0), 0)),
                  pl.BlockSpec((CONV_HALO, CONV_W),
                               lambda i: (jnp.minimum((jnp.minimum(i, n_tiles - 1) + 1) * nb, last), 0)),
                  pl.BlockSpec((CONV_CHUNKS, CONV_K, CONV_LANES), fixed3),
                  pl.BlockSpec((CONV_CHUNKS, 1, CONV_LANES), fixed3),
                  pl.BlockSpec((1, CONV_W), fixed),
                  pl.BlockSpec((1, CONV_W), fixed),
                  pl.BlockSpec((tm, CONV_W), lambda i: (jnp.minimum(i, n_tiles - 1), 1)),
                  pl.BlockSpec((tm, ATT_W), prev),
                  pl.BlockSpec((D_MODEL, D_MODEL), fixed, pipeline_mode=pl.Buffered(1)),
                  pl.BlockSpec((tm, D_MODEL), prev),
                  pl.BlockSpec((MOD_ROWS, 3 * D_MODEL), fixed)],
        out_specs=pl.BlockSpec((tm, D_MODEL), prev),
        out_shape=jax.ShapeDtypeStruct((SEQ, D_MODEL), _F32),
        scratch_shapes=[pltpu.VMEM((CONV_CHUNKS, tm + 2 * CONV_HALO, CONV_LANES), _F32),
                        pltpu.VMEM((CONV_CHUNKS, SUBLANES, tm + CONV_SPAN_EXTRA, CONV_LANES), _F32),
                        pltpu.VMEM((2, tm, CONV_W), _BF16),
                        pltpu.VMEM((D_MODEL, D_MODEL), _BF16)],
        compiler_params=pltpu.CompilerParams(vmem_limit_bytes=CONV_OUT_VMEM_LIMIT,
                                             dimension_semantics=("arbitrary",)),
        name="conv_out",
    )(yglu, yglu, yglu, cw3, cb3, lng, lnb, gc, att, w_out, x2, mod)


def _rope_tables():
    rows = SEQ // GRID_W
    row = np.repeat(np.arange(rows), GRID_W).astype(np.float64)
    col = np.tile(np.arange(GRID_W), rows).astype(np.float64)
    nf = HEAD_DIM // 4
    inv = ROPE_BASE ** (-np.arange(nf, dtype=np.float64) / nf)
    ang_r = row[:, None] * inv
    ang_c = col[:, None] * inv
    ang = np.concatenate([ang_r, ang_r, ang_c, ang_c], axis=-1)
    cos, sin = np.cos(ang), np.sin(ang)
    first_half = (np.arange(HEAD_DIM) % (2 * nf)) < nf
    sa = np.where(first_half, -sin, 0.0)
    sb = np.where(first_half, 0.0, sin)
    pad = lambda t, v: np.concatenate([t, np.full((CTX_LEN, HEAD_DIM), v)], axis=0)
    two = lambda t: jnp.asarray(np.concatenate([t, t], axis=1), _F32)
    return two(pad(cos, 1.0)), two(pad(sa, 0.0)), two(pad(sb, 0.0))


def _chunk_avg():
    r = np.arange(NORM_GROUP) // HEAD_DIM
    return jnp.asarray(np.where(r[:, None] == r[None, :], 1.0 / HEAD_DIM, 0.0), _BF16)


def _softmax_shift(q_norm_g, k_norm_g, q_scale):
    bound = (HEAD_DIM * q_scale * SHIFT_MARGIN) * jnp.max(jnp.abs(q_norm_g)) * jnp.max(jnp.abs(k_norm_g))
    ok = (bound <= MAX_FIXED_SHIFT).astype(_F32)
    return jnp.stack([jnp.where(ok > 0.5, bound, 0.0), ok]).astype(_F32)


def kernel(x, c, ctx, c_ctx, w_ada, b_ada, norm_g, w_in, q_norm_g, k_norm_g, lam_q1, lam_k1,
           lam_q2, lam_k2, sub_norm_g, conv_w, conv_b, conv_ln_g, conv_ln_b, w_out):
    assert x.shape == (1, SEQ, D_MODEL) and ctx.shape == (1, CTX_LEN, D_MODEL)
    assert w_ada.shape[0] == 1 and w_in.shape == (1, D_MODEL, IN_W)
    x2 = x[0]
    ctx2 = ctx[0]

    cc = jnp.zeros((MOD_ROWS, D_MODEL), _F32).at[0].set(c[0]).at[1].set(c_ctx)
    mod = _adaln(cc, w_ada[0], b_ada)
    h = _modulate(x2, ctx2, mod, norm_g)

    w = w_in[0]
    cos, sa, sb = _rope_tables()
    q_scale = LOG2E / math.sqrt(HEAD_DIM)
    avg = _chunk_avg()
    reps = PROJ_TN // HEAD_DIM
    gq = jnp.tile(q_norm_g, (1, reps)) * q_scale
    gk = jnp.tile(k_norm_g, (1, reps))

    q, gates = _qg_proj(h, w, gq, avg, cos, sa, sb)
    k, vt = _kv_proj(h, w, gk, avg, cos, sa, sb)
    yglu = _glu_proj(h, w)

    lamv = jnp.concatenate([lam_q1, lam_k1, lam_q2, lam_k2], axis=0)
    shift = _softmax_shift(q_norm_g, k_norm_g, q_scale)
    att = _attention(shift, lamv, q, k, vt, gates, sub_norm_g)

    out = _conv_out(yglu, conv_w[0], conv_b, conv_ln_g, conv_ln_b, gates, att, w_out[0], x2, mod)
    return out[None]
```
